```python
import math
import jax, jax.numpy as jnp
from jax import lax
import numpy as np

D_MODEL = 2048
BATCH = 8
SEQ = 4096
DEPTH = 2

MIX_WIDTH = D_MODEL
ATTN_WIDTH = MIX_WIDTH // 2
SSM_WIDTH = MIX_WIDTH - ATTN_WIDTH
HEAD_DIM = 64
N_Q_HEADS = ATTN_WIDTH // HEAD_DIM
KV_RATIO = 8
N_KV_HEADS = N_Q_HEADS // KV_RATIO
KV_DIM = N_KV_HEADS * HEAD_DIM
WINDOW = 128
SSM_GROUP = 16
N_SSM_GROUPS = SSM_WIDTH // SSM_GROUP
STATE = 64
IN_COLS = ATTN_WIDTH + 2 * KV_DIM + SSM_WIDTH
D_FF = ((8 * D_MODEL // 3 + 255) // 256) * 256
CONV_WIDTH = 3
EPS = 1e-6
NEG = -1e30

kernel_name = "hybrid_swa_s5_convffn_sandwich_adaln"


def rmsnorm(x, g):
    x32 = x.astype(jnp.float32)
    y = x32 * lax.rsqrt(jnp.mean(x32 * x32, axis=-1, keepdims=True) + EPS)
    return y.astype(x.dtype) * g


def sliding_window_attention(q, k, v, sinks):
    bsz, seq = q.shape[0], q.shape[1]
    nb = seq // WINDOW
    grp = N_Q_HEADS // N_KV_HEADS
    qb = q.reshape(bsz, nb, WINDOW, N_KV_HEADS, grp, HEAD_DIM).astype(jnp.float32)

    def band(t):
        tb = t.reshape(bsz, nb, WINDOW, N_KV_HEADS, HEAD_DIM)
        prev = jnp.pad(tb, ((0, 0), (1, 0), (0, 0), (0, 0), (0, 0)))[:, :-1]
        return jnp.concatenate([prev, tb], axis=2)

    kb = band(k).astype(jnp.float32)
    vb = band(v)
    s = jnp.einsum('bnqhgd,bnkhd->bnhgqk', qb, kb) * (HEAD_DIM ** -0.5)
    qi = jnp.arange(WINDOW)[:, None]
    kj = jnp.arange(2 * WINDOW)[None, :]
    in_band = (kj > qi) & (kj <= qi + WINDOW)
    blk = jnp.arange(nb)[:, None, None]
    valid = in_band[None] & ((blk * WINDOW + kj[None] - WINDOW) >= 0)
    s = jnp.where(valid[None, :, None, None], s, NEG)
    sink = sinks.astype(jnp.float32).reshape(1, 1, N_KV_HEADS, grp, 1, 1)
    m = jnp.maximum(jnp.max(s, axis=-1, keepdims=True), sink)
    e = jnp.exp(s - m)
    p = e / (jnp.sum(e, axis=-1, keepdims=True) + jnp.exp(sink - m))
    out = jnp.einsum('bnhgqk,bnkhd->bnqhgd', p.astype(v.dtype), vb)
    return out.reshape(bsz, seq, N_Q_HEADS * HEAD_DIM)


def s5_ssm(u, lam_re, lam_im, log_step, b_re, b_im, c_re, c_im, d_skip):
    bsz, seq = u.shape[0], u.shape[1]
    dtype = u.dtype
    u4 = u.reshape(bsz, seq, N_SSM_GROUPS, SSM_GROUP).astype(jnp.float32)
    lr = lam_re.astype(jnp.float32)
    li = lam_im.astype(jnp.float32)
    dt = jnp.exp(log_step.astype(jnp.float32))[:, None]
    mag = jnp.exp(lr * dt)
    ang = li * dt
    ab_re = mag * jnp.cos(ang)
    ab_im = mag * jnp.sin(ang)
    den = lr * lr + li * li
    f_re = ((ab_re - 1.0) * lr + ab_im * li) / den
    f_im = (ab_im * lr - (ab_re - 1.0) * li) / den
    br = b_re.astype(jnp.float32)
    bi = b_im.astype(jnp.float32)
    bb_re = f_re[..., None] * br - f_im[..., None] * bi
    bb_im = f_re[..., None] * bi + f_im[..., None] * br
    bu_re = jnp.einsum('bsgh,gph->bsgp', u4, bb_re)
    bu_im = jnp.einsum('bsgh,gph->bsgp', u4, bb_im)
    a_re = jnp.broadcast_to(ab_re[None, None], (1, seq, N_SSM_GROUPS, STATE))
    a_im = jnp.broadcast_to(ab_im[None, None], (1, seq, N_SSM_GROUPS, STATE))

    def combine(e1, e2):
        a1r, a1i, b1r, b1i = e1
        a2r, a2i, b2r, b2i = e2
        return (a2r * a1r - a2i * a1i,
                a2r * a1i + a2i * a1r,
                a2r * b1r - a2i * b1i + b2r,
                a2r * b1i + a2i * b1r + b2i)

    _, _, xr, xi = lax.associative_scan(combine, (a_re, a_im, bu_re, bu_im), axis=1)
    y = (jnp.einsum('bsgp,ghp->bsgh', xr, c_re.astype(jnp.float32))
         - jnp.einsum('bsgp,ghp->bsgh', xi, c_im.astype(jnp.float32))
         + d_skip.astype(jnp.float32)[None, None] * u4)
    return y.reshape(bsz, seq, SSM_WIDTH).astype(dtype)


def causal_depthwise_conv(h, w, b):
    seq = h.shape[1]
    hp = jnp.pad(h, ((0, 0), (CONV_WIDTH - 1, 0), (0, 0)))
    out = b
    for k in range(CONV_WIDTH):
        out = out + hp[:, k:k + seq] * w[k]
    return out


def setup_inputs(seed: int = 0) -> dict:
    key = jax.random.key(seed)
    ks = jax.random.split(key, 32)
    nrm = jax.random.normal
    G, P, H = N_SSM_GROUPS, STATE, SSM_GROUP
    return {
        "x": nrm(ks[0], (BATCH, SEQ, D_MODEL), jnp.float32),
        "c": nrm(ks[1], (BATCH, D_MODEL), jnp.float32),
        "w_ada": nrm(ks[2], (DEPTH, D_MODEL, 6 * D_MODEL), jnp.float32) * (0.5 * D_MODEL ** -0.5),
        "b_ada": nrm(ks[3], (DEPTH, 6 * D_MODEL), jnp.float32) * 0.02,
        "g_pre_mix": 1.0 + 0.1 * nrm(ks[4], (DEPTH, D_MODEL), jnp.float32),
        "g_post_mix": 1.0 + 0.1 * nrm(ks[5], (DEPTH, D_MODEL), jnp.float32),
        "w_in": nrm(ks[6], (DEPTH, D_MODEL, IN_COLS), jnp.float32) * D_MODEL ** -0.5,
        "attn_sinks": nrm(ks[7], (DEPTH, N_Q_HEADS), jnp.float32),
        "lam_re": -0.5 + 0.01 * nrm(ks[8], (DEPTH, G, P), jnp.float32),
        "lam_im": jnp.pi * jnp.arange(P, dtype=jnp.float32)[None, None, :]
                  + 0.01 * nrm(ks[9], (DEPTH, G, P), jnp.float32),
        "log_step": jax.random.uniform(ks[10], (DEPTH, G), jnp.float32,
                                       minval=math.log(1e-3), maxval=math.log(1e-1)),
        "ssm_b_re": nrm(ks[11], (DEPTH, G, P, H), jnp.float32) * (2 * H) ** -0.5,
        "ssm_b_im": nrm(ks[12], (DEPTH, G, P, H), jnp.float32) * (2 * H) ** -0.5,
        "ssm_c_re": nrm(ks[13], (DEPTH, G, H, P), jnp.float32) * 0.5,
        "ssm_c_im": nrm(ks[14], (DEPTH, G, H, P), jnp.float32) * 0.5,
        "ssm_d": nrm(ks[15], (DEPTH, G, H), jnp.float32),
        "w_glu": nrm(ks[16], (DEPTH, SSM_WIDTH, SSM_WIDTH), jnp.float32) * SSM_WIDTH ** -0.5,
        "g_attn_out": 1.0 + 0.1 * nrm(ks[17], (DEPTH, ATTN_WIDTH), jnp.float32),
        "g_ssm_out": 1.0 + 0.1 * nrm(ks[18], (DEPTH, SSM_WIDTH), jnp.float32),
        "w_out": nrm(ks[19], (DEPTH, MIX_WIDTH, D_MODEL), jnp.float32) * MIX_WIDTH ** -0.5,
        "g_pre_ffn": 1.0 + 0.1 * nrm(ks[20], (DEPTH, D_MODEL), jnp.float32),
        "g_post_ffn": 1.0 + 0.1 * nrm(ks[21], (DEPTH, D_MODEL), jnp.float32),
        "w_up": nrm(ks[22], (DEPTH, D_MODEL, 2 * D_FF), jnp.float32) * D_MODEL ** -0.5,
        "conv_w": nrm(ks[23], (DEPTH, CONV_WIDTH, 2 * D_FF), jnp.float32) * CONV_WIDTH ** -0.5,
        "conv_b": nrm(ks[24], (DEPTH, 2 * D_FF), jnp.float32) * 0.01,
        "w_down": nrm(ks[25], (DEPTH, D_FF, D_MODEL), jnp.float32) * D_FF ** -0.5,
    }


def reference(x, c, w_ada, b_ada, g_pre_mix, g_post_mix, w_in, attn_sinks, lam_re, lam_im,
              log_step, ssm_b_re, ssm_b_im, ssm_c_re, ssm_c_im, ssm_d, w_glu, g_attn_out,
              g_ssm_out, w_out, g_pre_ffn, g_post_ffn, w_up, conv_w, conv_b, w_down):
    bsz, seq = x.shape[0], x.shape[1]
    c_act = jax.nn.silu(c)
    for l in range(DEPTH):
        ada = c_act @ w_ada[l] + b_ada[l]
        sh_m, sc_m, gt_m, sh_f, sc_f, gt_f = [t[:, None, :] for t in jnp.split(ada, 6, axis=-1)]

        h = rmsnorm(x, g_pre_mix[l]) * (1.0 + sc_m) + sh_m
        proj = h @ w_in[l]
        q = proj[..., :ATTN_WIDTH].reshape(bsz, seq, N_Q_HEADS, HEAD_DIM)
        k = proj[..., ATTN_WIDTH:ATTN_WIDTH + KV_DIM].reshape(bsz, seq, N_KV_HEADS, HEAD_DIM)
        v = proj[..., ATTN_WIDTH + KV_DIM:ATTN_WIDTH + 2 * KV_DIM].reshape(bsz, seq, N_KV_HEADS, HEAD_DIM)
        u = proj[..., ATTN_WIDTH + 2 * KV_DIM:]

        attn = sliding_window_attention(q, k, v, attn_sinks[l])
        y = s5_ssm(u, lam_re[l], lam_im[l], log_step[l], ssm_b_re[l], ssm_b_im[l],
                   ssm_c_re[l], ssm_c_im[l], ssm_d[l])
        z = jax.nn.gelu(y, approximate=True)
        ssm = z * jax.nn.sigmoid(z @ w_glu[l])

        merged = jnp.concatenate([rmsnorm(attn, g_attn_out[l]), rmsnorm(ssm, g_ssm_out[l])], axis=-1)
        mix = merged @ w_out[l]
        x = x + (1.0 + gt_m) * rmsnorm(mix, g_post_mix[l])

        h = rmsnorm(x, g_pre_ffn[l]) * (1.0 + sc_f) + sh_f
        up = causal_depthwise_conv(h @ w_up[l], conv_w[l], conv_b[l])
        val, gate = up[..., :D_FF], up[..., D_FF:]
        ff = (jax.nn.gelu(gate, approximate=True) * val) @ w_down[l]
        x = x + (1.0 + gt_f) * rmsnorm(ff, g_post_ffn[l])
    return x
```

```python
import functools
import math

import jax
import jax.numpy as jnp
from jax import lax
from jax.experimental import pallas as pl
from jax.experimental.pallas import tpu as pltpu

HEAD_DIM = 64
KV_RATIO = 8
WINDOW = 128
SSM_GROUP = 16
STATE = 64
CONV_WIDTH = 3
EPS = 1e-6
NEG = -1e30

LANES = 128
SUBLANES = 8
BF16_ROWS = 16
VMEM_LIMIT = 56 * 1024 * 1024

_BF16 = jnp.bfloat16
_F32 = jnp.float32


def _gelu_tanh(x):
    return x * (0.5 * (1.0 + jnp.tanh(math.sqrt(2.0 / math.pi) * (x + 0.044715 * (x * x * x)))))


def _sigmoid(x):
    return 1.0 / (1.0 + jnp.exp(-x))


def _rms_scale(x):
    return x * lax.rsqrt(jnp.mean(x * x, axis=-1, keepdims=True) + EPS)


def _resident(shape, index_map):
    return pl.BlockSpec(shape, index_map, pipeline_mode=pl.Buffered(1))


def _adaln_kernel(c_ref, w_ref, b_ref, o_ref):
    c = c_ref[...]
    ca = c * _sigmoid(c)
    o_ref[0] = jnp.dot(ca, w_ref[0], preferred_element_type=_F32,
                       precision=lax.Precision.HIGHEST) + b_ref[0]


def _adaln(c, w_ada, b_ada, tn=1024):
    depth, d, n = w_ada.shape
    bsz = c.shape[0]
    return pl.pallas_call(
        _adaln_kernel,
        grid=(depth, n // tn),
        in_specs=[
            pl.BlockSpec((bsz, d), lambda l, j: (0, 0)),
            pl.BlockSpec((1, d, tn), lambda l, j: (l, 0, j)),
            pl.BlockSpec((1, 1, tn), lambda l, j: (l, 0, j)),
        ],
        out_specs=pl.BlockSpec((1, bsz, tn), lambda l, j: (l, 0, j)),
        out_shape=jax.ShapeDtypeStruct((depth, bsz, n), _F32),
        compiler_params=pltpu.CompilerParams(
            dimension_semantics=("arbitrary", "arbitrary"), vmem_limit_bytes=VMEM_LIMIT),
        name="adaln",
    )(c, w_ada, b_ada.reshape(depth, 1, n))


def _modulated_norm(xv, g, scale, shift):
    return _rms_scale(xv) * (g * (1.0 + scale)) + shift


def _in_proj_kernel(x_ref, ada_ref, g_ref, w_ref, q_ref, kv_ref, u_ref, *, d, attn_w, kv_w):
    ada = ada_ref[0]
    h = _modulated_norm(x_ref[0], g_ref[...], ada[:, d:2 * d], ada[:, 0:d])
    p = jnp.dot(h.astype(_BF16), w_ref[...], preferred_element_type=_F32)
    q_ref[0] = (p[:, :attn_w] * (HEAD_DIM ** -0.5)).astype(_BF16)
    kv_ref[0] = p[:, attn_w:attn_w + kv_w].astype(_BF16)
    u_ref[...] = p[:, attn_w + kv_w:].astype(_BF16)


def _in_proj(x, ada, g, w_bf16, attn_w, kv_w, tm):
    bsz, seq, d = x.shape
    ncols = w_bf16.shape[1]
    ssm_w = ncols - attn_w - kv_w
    kern = functools.partial(_in_proj_kernel, d=d, attn_w=attn_w, kv_w=kv_w)
    return pl.pallas_call(
        kern,
        grid=(bsz, seq // tm),
        in_specs=[
            pl.BlockSpec((1, tm, d), lambda b, i: (b, i, 0)),
            pl.BlockSpec((1, 1, 6 * d), lambda b, i: (b, 0, 0)),
            pl.BlockSpec((1, d), lambda b, i: (0, 0)),
            _resident((d, ncols), lambda b, i: (0, 0)),
        ],
        out_specs=[
            pl.BlockSpec((1, tm, attn_w), lambda b, i: (b, i, 0)),
            pl.BlockSpec((1, tm, kv_w), lambda b, i: (b, i, 0)),
            pl.BlockSpec((tm, ssm_w), lambda b, i: (i, b)),
        ],
        out_shape=[
            jax.ShapeDtypeStruct((bsz, seq, attn_w), _BF16),
            jax.ShapeDtypeStruct((bsz, seq, kv_w), _BF16),
            jax.ShapeDtypeStruct((seq, bsz * ssm_w), _BF16),
        ],
        compiler_params=pltpu.CompilerParams(
            dimension_semantics=("arbitrary", "arbitrary"), vmem_limit_bytes=VMEM_LIMIT),
        name="in_proj",
    )(x, ada, g.reshape(1, d), w_bf16)


def _attention_kernel(sink_ref, q_ref, kvc_ref, kvp_ref, g_ref, o_ref, *, n_kv):
    n = pl.program_id(1)
    w = WINDOW
    kvc = kvc_ref[0].astype(_F32)
    kvp = kvp_ref[0].astype(_F32)
    kv = jnp.concatenate([kvp, kvc], axis=0)
    kw = n_kv * HEAD_DIM
    k2 = kv[:, :kw]
    v2 = kv[:, kw:]
    k2r = pltpu.roll(k2, HEAD_DIM, 1)
    v2r = pltpu.roll(v2, HEAD_DIM, 1)
    lane = lax.broadcasted_iota(jnp.int32, (2 * w, LANES), 1)
    lo = lane < HEAD_DIM
    kk = (jnp.where(lo, k2, k2r).astype(_BF16), jnp.where(lo, k2r, k2).astype(_BF16))
    v_lo = (jnp.where(lo, v2, 0.0).astype(_BF16), jnp.where(lo, v2r, 0.0).astype(_BF16))
    v_hi = (jnp.where(lo, 0.0, v2r).astype(_BF16), jnp.where(lo, 0.0, v2).astype(_BF16))

    qi = lax.broadcasted_iota(jnp.int32, (w, 2 * w), 0)
    kj = lax.broadcasted_iota(jnp.int32, (w, 2 * w), 1)
    valid = (kj > qi) & (kj <= qi + w) & ((kj >= w) | (n > 0))
    qlane_lo = lax.broadcasted_iota(jnp.int32, (w, LANES), 1) < HEAD_DIM

    def probs(qm, kkj, sink):
        s = lax.dot_general(qm, kkj, (((1,), (1,)), ((), ())), preferred_element_type=_F32)
        s = jnp.where(valid, s, NEG)
        m = jnp.maximum(jnp.max(s, axis=-1, keepdims=True), sink)
        e = jnp.exp(s - m)
        denom = jnp.sum(e, axis=-1, keepdims=True) + jnp.exp(sink - m)
        return (e * (1.0 / denom)).astype(_BF16)

    pairs_per_kv = KV_RATIO // 2
    outs = []
    for j in range(n_kv):
        for mth in range(pairs_per_kv):
            pair = j * pairs_per_kv + mth
            q2 = q_ref[0, :, pair * LANES:(pair + 1) * LANES]
            zero = jnp.zeros_like(q2)
            pa = probs(jnp.where(qlane_lo, q2, zero), kk[j], sink_ref[2 * pair])
            pb = probs(jnp.where(qlane_lo, zero, q2), kk[j], sink_ref[2 * pair + 1])
            o2 = (jnp.dot(pa, v_lo[j], preferred_element_type=_F32)
                  + jnp.dot(pb, v_hi[j], preferred_element_type=_F32))
            outs.append(o2)
    o = jnp.concatenate(outs, axis=1)
    o_ref[0] = (_rms_scale(o) * g_ref[...]).astype(_BF16)


def _attention(q, kv, sinks, g):
    bsz, seq, attn_w = q.shape
    kv_w = kv.shape[-1]
    n_kv = kv_w // (2 * HEAD_DIM)
    assert n_kv * HEAD_DIM == LANES and attn_w == n_kv * KV_RATIO * HEAD_DIM
    nb = seq // WINDOW
    kern = functools.partial(_attention_kernel, n_kv=n_kv)
    return pl.pallas_call(
        kern,
        grid=(bsz, nb),
        in_specs=[
            pl.BlockSpec(memory_space=pltpu.SMEM),
            pl.BlockSpec((1, WINDOW, attn_w), lambda b, n: (b, n, 0)),
            pl.BlockSpec((1, WINDOW, kv_w), lambda b, n: (b, n, 0)),
            pl.BlockSpec((1, WINDOW, kv_w), lambda b, n: (b, jnp.maximum(n - 1, 0), 0)),
            pl.BlockSpec((1, attn_w), lambda b, n: (0, 0)),
        ],
        out_specs=pl.BlockSpec((1, WINDOW, attn_w), lambda b, n: (b, n, 0)),
        out_shape=jax.ShapeDtypeStruct((bsz, seq, attn_w), _BF16),
        compiler_params=pltpu.CompilerParams(
            dimension_semantics=("arbitrary", "arbitrary"), vmem_limit_bytes=VMEM_LIMIT),
        name="swa",
    )(sinks, q, kv, kv, g.reshape(1, attn_w))


def _ssm_kernel(u_ref, wb_ref, cw_ref, ar_ref, ai_ref, d_ref, z_ref, xs_ref, st_ref, *, bsz, tc, ns):
    @pl.when(pl.program_id(1) == 0)
    def _():
        st_ref[...] = jnp.zeros_like(st_ref)

    uv = u_ref[...]
    xs_ref[...] = jnp.dot(uv, wb_ref[0], preferred_element_type=_F32)
    ar = jnp.broadcast_to(ar_ref[0], (bsz, ns))
    ai = jnp.broadcast_to(ai_ref[0], (bsz, ns))

    def step(t, carry):
        xr, xi = carry
        r = pl.multiple_of(t * bsz, bsz)
        nxr = ar * xr - ai * xi + xs_ref[pl.ds(r, bsz), 0:ns]
        nxi = ar * xi + ai * xr + xs_ref[pl.ds(r, bsz), ns:2 * ns]
        xs_ref[pl.ds(r, bsz), 0:ns] = nxr
        xs_ref[pl.ds(r, bsz), ns:2 * ns] = nxi
        return nxr, nxi

    xr, xi = lax.fori_loop(0, tc, step, (st_ref[:, 0:ns], st_ref[:, ns:2 * ns]), unroll=4)
    st_ref[:, 0:ns] = xr
    st_ref[:, ns:2 * ns] = xi
    y = (jnp.dot(xs_ref[...].astype(_BF16), cw_ref[0], preferred_element_type=_F32)
         + d_ref[...] * uv.astype(_F32))
    z_ref[...] = _gelu_tanh(y).astype(_BF16)


def _ssm_params(lam_re, lam_im, log_step, b_re, b_im, c_re, c_im, slab):
    g, p = lam_re.shape
    h = SSM_GROUP
    gs = slab // h
    n_slabs = g // gs
    dt = jnp.exp(log_step)[:, None]
    mag = jnp.exp(lam_re * dt)
    ang = lam_im * dt
    ab_re = mag * jnp.cos(ang)
    ab_im = mag * jnp.sin(ang)
    den = lam_re * lam_re + lam_im * lam_im
    f_re = ((ab_re - 1.0) * lam_re + ab_im * lam_im) / den
    f_im = (ab_im * lam_re - (ab_re - 1.0) * lam_im) / den
    bb_re = f_re[..., None] * b_re - f_im[..., None] * b_im
    bb_im = f_re[..., None] * b_im + f_im[..., None] * b_re
    eye = jnp.eye(gs, dtype=_F32)

    def block_diag_in(bb):
        t = bb.reshape(n_slabs, gs, p, h)
        return jnp.einsum('sgph,gk->sghkp', t, eye).reshape(n_slabs, gs * h, gs * p)

    def block_diag_out(cc):
        t = cc.reshape(n_slabs, gs, h, p)
        return jnp.einsum('sghp,gk->sgpkh', t, eye).reshape(n_slabs, gs * p, gs * h)

    wb = jnp.concatenate([block_diag_in(bb_re), block_diag_in(bb_im)], axis=2).astype(_BF16)
    cw = jnp.concatenate([block_diag_out(c_re), block_diag_out(-c_im)], axis=1).astype(_BF16)
    a_re = ab_re.reshape(n_slabs, 1, gs * p)
    a_im = ab_im.reshape(n_slabs, 1, gs * p)
    return wb, cw, a_re, a_im


def _ssm(u_tm, bsz, wb, cw, a_re, a_im, d_skip, tc):
    rows, width = u_tm.shape
    n_slabs, slab, ns2 = wb.shape
    ns = ns2 // 2
    seq = rows // bsz
    kern = functools.partial(_ssm_kernel, bsz=bsz, tc=tc, ns=ns)
    return pl.pallas_call(
        kern,
        grid=(n_slabs, seq // tc),
        in_specs=[
            pl.BlockSpec((tc * bsz, slab), lambda s, t: (t, s)),
            pl.BlockSpec((1, slab, ns2), lambda s, t: (s, 0, 0)),
            pl.BlockSpec((1, ns2, slab), lambda s, t: (s, 0, 0)),
            pl.BlockSpec((1, 1, ns), lambda s, t: (s, 0, 0)),
            pl.BlockSpec((1, 1, ns), lambda s, t: (s, 0, 0)),
            pl.BlockSpec((1, slab), lambda s, t: (0, s)),
        ],
        out_specs=pl.BlockSpec((tc * bsz, slab), lambda s, t: (t, s)),
        out_shape=jax.ShapeDtypeStruct((rows, width), _BF16),
        scratch_shapes=[
            pltpu.VMEM((tc * bsz, ns2), _F32),
            pltpu.VMEM((bsz, ns2), _F32),
        ],
        compiler_params=pltpu.CompilerParams(
            dimension_semantics=("arbitrary", "arbitrary"), vmem_limit_bytes=VMEM_LIMIT),
        name="s5",
    )(u_tm, wb, cw, a_re, a_im, d_skip.reshape(1, width))


def _post_mix_kernel(x_ref, attn_ref, z_ref, ada_ref, wglu_ref, wout_ref, gssm_ref, gpost_ref,
                     o_ref, *, d, attn_w):
    z = z_ref[...]
    zf = z.astype(_F32)
    ssm = zf * _sigmoid(jnp.dot(z, wglu_ref[...], preferred_element_type=_F32))
    ssm_n = (_rms_scale(ssm) * gssm_ref[...]).astype(_BF16)
    mix = (jnp.dot(attn_ref[0], wout_ref[0:attn_w, :], preferred_element_type=_F32)
           + jnp.dot(ssm_n, wout_ref[attn_w:, :], preferred_element_type=_F32))
    gate = ada_ref[0][:, 2 * d:3 * d]
    o_ref[0] = x_ref[0] + (1.0 + gate) * (_rms_scale(mix) * gpost_ref[...])


def _post_mix(x, attn_n, z_tm, ada, wglu_bf16, wout_bf16, g_ssm, g_post, tm):
    bsz, seq, d = x.shape
    attn_w = attn_n.shape[-1]
    ssm_w = wglu_bf16.shape[0]
    kern = functools.partial(_post_mix_kernel, d=d, attn_w=attn_w)
    return pl.pallas_call(
        kern,
        grid=(bsz, seq // tm),
        in_specs=[
            pl.BlockSpec((1, tm, d), lambda b, i: (b, i, 0)),
            pl.BlockSpec((1, tm, attn_w), lambda b, i: (b, i, 0)),
            pl.BlockSpec((tm, ssm_w), lambda b, i: (i, b)),
            pl.BlockSpec((1, 1, 6 * d), lambda b, i: (b, 0, 0)),
            _resident((ssm_w, ssm_w), lambda b, i: (0, 0)),
            _resident((attn_w + ssm_w, d), lambda b, i: (0, 0)),
            pl.BlockSpec((1, ssm_w), lambda b, i: (0, 0)),
            pl.BlockSpec((1, d), lambda b, i: (0, 0)),
        ],
        out_specs=pl.BlockSpec((1, tm, d), lambda b, i: (b, i, 0)),
        out_shape=jax.ShapeDtypeStruct((bsz, seq, d), _F32),
        compiler_params=pltpu.CompilerParams(
            dimension_semantics=("arbitrary", "arbitrary"), vmem_limit_bytes=VMEM_LIMIT),
        name="post_mix",
    )(x, attn_n, z_tm.reshape(seq, bsz * ssm_w), ada, wglu_bf16, wout_bf16,
      g_ssm.reshape(1, ssm_w), g_post.reshape(1, d))


HALO = BF16_ROWS


def _ffn_kernel(x_ref, xh_ref, ada_ref, gpre_ref, wv_ref, wg_ref, cwv_ref, cwg_ref, cbv_ref, cbg_ref,
                wd_ref, gpost_ref, o_ref, h_ref, acc_ref, upv_ref, upg_ref, *, d, tm):
    i = pl.program_id(1)
    j = pl.program_id(2)
    ada = ada_ref[0]

    @pl.when(j == 0)
    def _():
        g = gpre_ref[...]
        scale = ada[:, 4 * d:5 * d]
        shift = ada[:, 3 * d:4 * d]
        hh = _modulated_norm(xh_ref[0], g, scale, shift)
        h_ref[0:HALO, :] = jnp.where(i > 0, hh, 0.0).astype(_BF16)
        h_ref[HALO:, :] = _modulated_norm(x_ref[0], g, scale, shift).astype(_BF16)
        acc_ref[...] = jnp.zeros_like(acc_ref)

    hv = h_ref[...]
    upv_ref[...] = jnp.dot(hv, wv_ref[...], preferred_element_type=_F32)
    upg_ref[...] = jnp.dot(hv, wg_ref[...], preferred_element_type=_F32)

    def conv(up_ref, cw_ref, cb_ref):
        out = cb_ref[...]
        for k in range(CONV_WIDTH):
            off = HALO - (CONV_WIDTH - 1) + k
            out = out + up_ref[pl.ds(off, tm), :] * cw_ref[k:k + 1, :]
        return out

    val = conv(upv_ref, cwv_ref, cbv_ref)
    gate = conv(upg_ref, cwg_ref, cbg_ref)
    act = (_gelu_tanh(gate) * val).astype(_BF16)
    acc_ref[...] += jnp.dot(act, wd_ref[...], preferred_element_type=_F32)

    @pl.when(j == pl.num_programs(2) - 1)
    def _():
        gt = ada[:, 5 * d:6 * d]
        o_ref[0] = x_ref[0] + (1.0 + gt) * (_rms_scale(acc_ref[...]) * gpost_ref[...])


def _ffn(x, ada, g_pre, wup_bf16, conv_w, conv_b, wdown_bf16, g_post, tm, tf):
    bsz, seq, d = x.shape
    d_ff = wdown_bf16.shape[0]
    nj = d_ff // tf
    halo_blocks = tm // HALO
    kern = functools.partial(_ffn_kernel, d=d, tm=tm)
    cb = conv_b.reshape(1, 2 * d_ff)
    return pl.pallas_call(
        kern,
        grid=(bsz, seq // tm, nj),
        in_specs=[
            pl.BlockSpec((1, tm, d), lambda b, i, j: (b, i, 0)),
            pl.BlockSpec((1, HALO, d), lambda b, i, j: (b, jnp.maximum(i * halo_blocks - 1, 0), 0)),
            pl.BlockSpec((1, 1, 6 * d), lambda b, i, j: (b, 0, 0)),
            pl.BlockSpec((1, d), lambda b, i, j: (0, 0)),
            pl.BlockSpec((d, tf), lambda b, i, j: (0, j)),
            pl.BlockSpec((d, tf), lambda b, i, j: (0, nj + j)),
            pl.BlockSpec((CONV_WIDTH, tf), lambda b, i, j: (0, j)),
            pl.BlockSpec((CONV_WIDTH, tf), lambda b, i, j: (0, nj + j)),
            pl.BlockSpec((1, tf), lambda b, i, j: (0, j)),
            pl.BlockSpec((1, tf), lambda b, i, j: (0, nj + j)),
            pl.BlockSpec((tf, d), lambda b, i, j: (j, 0)),
            pl.BlockSpec((1, d), lambda b, i, j: (0, 0)),
        ],
        out_specs=pl.BlockSpec((1, tm, d), lambda b, i, j: (b, i, 0)),
        out_shape=jax.ShapeDtypeStruct((bsz, seq, d), _F32),
        scratch_shapes=[
            pltpu.VMEM((tm + HALO, d), _BF16),
            pltpu.VMEM((tm, d), _F32),
            pltpu.VMEM((tm + HALO, tf), _F32),
            pltpu.VMEM((tm + HALO, tf), _F32),
        ],
        compiler_params=pltpu.CompilerParams(
            dimension_semantics=("arbitrary", "arbitrary", "arbitrary"),
            vmem_limit_bytes=VMEM_LIMIT),
        name="conv_ffn",
    )(x, x, ada, g_pre.reshape(1, d), wup_bf16, wup_bf16, conv_w, conv_w, cb, cb,
      wdown_bf16, g_post.reshape(1, d))


def _largest_tile(n, cap, quantum):
    best = quantum
    for t in range(quantum, min(n, cap) + 1, quantum):
        if n % t == 0:
            best = t
    return best


def kernel(x, c, w_ada, b_ada, g_pre_mix, g_post_mix, w_in, attn_sinks, lam_re, lam_im, log_step,
           ssm_b_re, ssm_b_im, ssm_c_re, ssm_c_im, ssm_d, w_glu, g_attn_out, g_ssm_out, w_out,
           g_pre_ffn, g_post_ffn, w_up, conv_w, conv_b, w_down):
    bsz, seq, d = x.shape
    depth = w_in.shape[0]
    ssm_w = w_glu.shape[1]
    attn_w = w_out.shape[1] - ssm_w
    kv_w = w_in.shape[2] - attn_w - ssm_w
    d_ff = w_down.shape[1]
    assert bsz == SUBLANES, "the scan keeps one batch row per f32 sublane"

    tm = _largest_tile(seq, 512, WINDOW)
    tf = _largest_tile(d_ff, 512, LANES)
    tc = _largest_tile(seq, 128, SUBLANES)
    slab = 2 * LANES

    ada_all = _adaln(c, w_ada, b_ada)
    for l in range(depth):
        ada = ada_all[l][:, None, :]
        q, kv, u_tm = _in_proj(x, ada, g_pre_mix[l], w_in[l].astype(_BF16), attn_w, kv_w, tm)
        attn_n = _attention(q, kv, attn_sinks[l], g_attn_out[l])
        wb, cw, a_re, a_im = _ssm_params(lam_re[l], lam_im[l], log_step[l], ssm_b_re[l], ssm_b_im[l],
                                         ssm_c_re[l], ssm_c_im[l], slab)
        z_tm = _ssm(u_tm.reshape(seq * bsz, ssm_w), bsz, wb, cw, a_re, a_im,
                    ssm_d[l].reshape(ssm_w), tc)
        x = _post_mix(x, attn_n, z_tm, ada, w_glu[l].astype(_BF16), w_out[l].astype(_BF16),
                      g_ssm_out[l], g_post_mix[l], tm)
        x = _ffn(x, ada, g_pre_ffn[l], w_up[l].astype(_BF16), conv_w[l], conv_b[l],
                 w_down[l].astype(_BF16), g_post_ffn[l], tm, tf)
    return x
```

```python
import functools
import math

import jax
import jax.numpy as jnp
from jax import lax
from jax.experimental import pallas as pl
from jax.experimental.pallas import tpu as pltpu

HEAD_DIM = 64
KV_RATIO = 8
WINDOW = 128
SSM_GROUP = 16
STATE = 64
CONV_WIDTH = 3
EPS = 1e-6
NEG = -1e30

LANES = 128
SUBLANES = 8
BF16_ROWS = 16
VMEM_LIMIT = 56 * 1024 * 1024

_BF16 = jnp.bfloat16
_F32 = jnp.float32


def _gelu_tanh(x):
    return x * (0.5 * (1.0 + jnp.tanh(math.sqrt(2.0 / math.pi) * (x + 0.044715 * (x * x * x)))))


def _sigmoid(x):
    return 1.0 / (1.0 + jnp.exp(-x))


def _rms_scale(x):
    return x * lax.rsqrt(jnp.mean(x * x, axis=-1, keepdims=True) + EPS)


def _resident(shape, index_map):
    return pl.BlockSpec(shape, index_map, pipeline_mode=pl.Buffered(1))


def _adaln_kernel(c_ref, w_ref, b_ref, o_ref):
    c = c_ref[...]
    ca = c * _sigmoid(c)
    o_ref[0] = jnp.dot(ca, w_ref[0], preferred_element_type=_F32,
                       precision=lax.Precision.HIGHEST) + b_ref[0]


def _adaln(c, w_ada, b_ada, tn=1024):
    depth, d, n = w_ada.shape
    bsz = c.shape[0]
    return pl.pallas_call(
        _adaln_kernel,
        grid=(depth, n // tn),
        in_specs=[
            pl.BlockSpec((bsz, d), lambda l, j: (0, 0)),
            pl.BlockSpec((1, d, tn), lambda l, j: (l, 0, j)),
            pl.BlockSpec((1, 1, tn), lambda l, j: (l, 0, j)),
        ],
        out_specs=pl.BlockSpec((1, bsz, tn), lambda l, j: (l, 0, j)),
        out_shape=jax.ShapeDtypeStruct((depth, bsz, n), _F32),
        compiler_params=pltpu.CompilerParams(
            dimension_semantics=("arbitrary", "arbitrary"), vmem_limit_bytes=VMEM_LIMIT),
        name="adaln",
    )(c, w_ada, b_ada.reshape(depth, 1, n))


def _modulated_norm(xv, g, scale, shift):
    return _rms_scale(xv) * (g * (1.0 + scale)) + shift


def _in_proj_kernel(x_ref, ada_ref, g_ref, w_ref, q_ref, kv_ref, u_ref, *, d, attn_w, kv_w):
    ada = ada_ref[0]
    h = _modulated_norm(x_ref[0], g_ref[...], ada[:, d:2 * d], ada[:, 0:d])
    p = jnp.dot(h.astype(_BF16), w_ref[...], preferred_element_type=_F32)
    q_ref[0] = (p[:, :attn_w] * (HEAD_DIM ** -0.5)).astype(_BF16)
    kv_ref[0] = p[:, attn_w:attn_w + kv_w].astype(_BF16)
    u_ref[...] = p[:, attn_w + kv_w:].astype(_BF16)


def _in_proj(x, ada, g, w_bf16, attn_w, kv_w, tm):
    bsz, seq, d = x.shape
    ncols = w_bf16.shape[1]
    ssm_w = ncols - attn_w - kv_w
    kern = functools.partial(_in_proj_kernel, d=d, attn_w=attn_w, kv_w=kv_w)
    return pl.pallas_call(
        kern,
        grid=(bsz, seq // tm),
        in_specs=[
            pl.BlockSpec((1, tm, d), lambda b, i: (b, i, 0)),
            pl.BlockSpec((1, 1, 6 * d), lambda b, i: (b, 0, 0)),
            pl.BlockSpec((1, d), lambda b, i: (0, 0)),
            _resident((d, ncols), lambda b, i: (0, 0)),
        ],
        out_specs=[
            pl.BlockSpec((1, tm, attn_w), lambda b, i: (b, i, 0)),
            pl.BlockSpec((1, tm, kv_w), lambda b, i: (b, i, 0)),
            pl.BlockSpec((tm, ssm_w), lambda b, i: (i, b)),
        ],
        out_shape=[
            jax.ShapeDtypeStruct((bsz, seq, attn_w), _BF16),
            jax.ShapeDtypeStruct((bsz, seq, kv_w), _BF16),
            jax.ShapeDtypeStruct((seq, bsz * ssm_w), _BF16),
        ],
        compiler_params=pltpu.CompilerParams(
            dimension_semantics=("arbitrary", "arbitrary"), vmem_limit_bytes=VMEM_LIMIT),
        name="in_proj",
    )(x, ada, g.reshape(1, d), w_bf16)


def _attention_kernel(sink_ref, q_ref, kvc_ref, kvp_ref, g_ref, o_ref, *, n_kv, qb):
    step = pl.program_id(1)
    w = WINDOW
    rows = (qb + 1) * w
    kv = jnp.concatenate([kvp_ref[0], kvc_ref[0]], axis=0).astype(_F32)
    kw = n_kv * HEAD_DIM
    k2 = kv[:, :kw]
    v2 = kv[:, kw:]
    k2r = pltpu.roll(k2, HEAD_DIM, 1)
    v2r = pltpu.roll(v2, HEAD_DIM, 1)
    lo = lax.broadcasted_iota(jnp.int32, (rows, LANES), 1) < HEAD_DIM
    one_lo = jnp.where(lo, 1.0, 0.0).astype(_BF16)
    one_hi = jnp.where(lo, 0.0, 1.0).astype(_BF16)
    kk = (jnp.where(lo, k2, k2r).astype(_BF16), jnp.where(lo, k2r, k2).astype(_BF16))
    top = (jnp.concatenate([jnp.where(lo, v2, 0.0).astype(_BF16), one_lo], axis=1),
           jnp.concatenate([jnp.where(lo, v2r, 0.0).astype(_BF16), one_lo], axis=1))
    bot = (jnp.concatenate([jnp.where(lo, 0.0, v2r).astype(_BF16), one_hi], axis=1),
           jnp.concatenate([jnp.where(lo, 0.0, v2).astype(_BF16), one_hi], axis=1))

    qi = lax.broadcasted_iota(jnp.int32, (w, 2 * w), 0)
    kj = lax.broadcasted_iota(jnp.int32, (w, 2 * w), 1)
    in_band = (kj > qi) & (kj <= qi + w)
    qlane_lo = lax.broadcasted_iota(jnp.int32, (w, LANES), 1) < HEAD_DIM
    pairs_per_kv = KV_RATIO // 2
    n_pairs = n_kv * pairs_per_kv

    def scores(blk):
        out = []
        for pair in range(n_pairs):
            j = pair // pairs_per_kv
            q2 = q_ref[0, blk * w:(blk + 1) * w, pair * LANES:(pair + 1) * LANES]
            zero = jnp.zeros_like(q2)
            kkj = kk[j][blk * w:(blk + 2) * w]
            dn = (((1,), (1,)), ((), ()))
            out.append((lax.dot_general(jnp.where(qlane_lo, q2, zero), kkj, dn, preferred_element_type=_F32),
                        lax.dot_general(jnp.where(qlane_lo, zero, q2), kkj, dn, preferred_element_type=_F32)))
        return out

    def finish(blk, s_blk):
        valid = in_band if blk > 0 else in_band & ((kj >= w) | (step > 0))
        es, sk = [], []
        for pair in range(n_pairs):
            e2, k2_ = [], []
            for half in range(2):
                sink = sink_ref[2 * pair + half]
                s = jnp.where(valid, s_blk[pair][half], NEG)
                m = jnp.maximum(jnp.max(s, axis=-1, keepdims=True), sink)
                e2.append(jnp.exp(s - m).astype(_BF16))
                k2_.append(jnp.exp(sink - m))
            es.append(jnp.concatenate(e2, axis=1))
            sk.append(jnp.where(qlane_lo, k2_[0], k2_[1]))
        outs = []
        for pair in range(n_pairs):
            j = pair // pairs_per_kv
            r = jnp.concatenate([top[j][blk * w:(blk + 2) * w], bot[j][blk * w:(blk + 2) * w]], axis=0)
            ox = jnp.dot(es[pair], r, preferred_element_type=_F32)
            outs.append(ox[:, :LANES] / (ox[:, LANES:] + sk[pair]))
        o = jnp.concatenate(outs, axis=1)
        o_ref[0, blk * w:(blk + 1) * w, :] = (_rms_scale(o) * g_ref[...]).astype(_BF16)

    s_next = scores(0)
    for blk in range(qb):
        s_cur = s_next
        if blk + 1 < qb:
            s_next = scores(blk + 1)
        finish(blk, s_cur)


def _attention(q, kv, sinks, g, qb):
    bsz, seq, attn_w = q.shape
    kv_w = kv.shape[-1]
    n_kv = kv_w // (2 * HEAD_DIM)
    assert n_kv * HEAD_DIM == LANES and attn_w == n_kv * KV_RATIO * HEAD_DIM
    tq = qb * WINDOW
    kern = functools.partial(_attention_kernel, n_kv=n_kv, qb=qb)
    return pl.pallas_call(
        kern,
        grid=(bsz, seq // tq),
        in_specs=[
            pl.BlockSpec(memory_space=pltpu.SMEM),
            pl.BlockSpec((1, tq, attn_w), lambda b, n: (b, n, 0)),
            pl.BlockSpec((1, tq, kv_w), lambda b, n: (b, n, 0)),
            pl.BlockSpec((1, WINDOW, kv_w), lambda b, n: (b, jnp.maximum(n * qb - 1, 0), 0)),
            pl.BlockSpec((1, attn_w), lambda b, n: (0, 0)),
        ],
        out_specs=pl.BlockSpec((1, tq, attn_w), lambda b, n: (b, n, 0)),
        out_shape=jax.ShapeDtypeStruct((bsz, seq, attn_w), _BF16),
        compiler_params=pltpu.CompilerParams(
            dimension_semantics=("arbitrary", "arbitrary"), vmem_limit_bytes=VMEM_LIMIT),
        name="swa",
    )(sinks, q, kv, kv, g.reshape(1, attn_w))


def _ssm_kernel(u_ref, wb_ref, cw_ref, ar_ref, ai_ref, d_ref, z_ref, xs_ref, st_ref, *, bsz, tc, ns):
    @pl.when(pl.program_id(1) == 0)
    def _():
        st_ref[...] = jnp.zeros_like(st_ref)

    uv = u_ref[...]
    xs_ref[...] = jnp.dot(uv, wb_ref[0], preferred_element_type=_F32)
    ar = jnp.broadcast_to(ar_ref[0], (bsz, ns))
    ai = jnp.broadcast_to(ai_ref[0], (bsz, ns))

    def step(t, carry):
        xr, xi = carry
        r = pl.multiple_of(t * bsz, bsz)
        nxr = ar * xr - ai * xi + xs_ref[pl.ds(r, bsz), 0:ns]
        nxi = ar * xi + ai * xr + xs_ref[pl.ds(r, bsz), ns:2 * ns]
        xs_ref[pl.ds(r, bsz), 0:ns] = nxr
        xs_ref[pl.ds(r, bsz), ns:2 * ns] = nxi
        return nxr, nxi

    xr, xi = lax.fori_loop(0, tc, step, (st_ref[:, 0:ns], st_ref[:, ns:2 * ns]), unroll=4)
    st_ref[:, 0:ns] = xr
    st_ref[:, ns:2 * ns] = xi
    y = (jnp.dot(xs_ref[...].astype(_BF16), cw_ref[0], preferred_element_type=_F32)
         + d_ref[...] * uv.astype(_F32))
    z_ref[...] = _gelu_tanh(y).astype(_BF16)


def _ssm_params(lam_re, lam_im, log_step, b_re, b_im, c_re, c_im, slab):
    g, p = lam_re.shape
    h = SSM_GROUP
    gs = slab // h
    n_slabs = g // gs
    dt = jnp.exp(log_step)[:, None]
    mag = jnp.exp(lam_re * dt)
    ang = lam_im * dt
    ab_re = mag * jnp.cos(ang)
    ab_im = mag * jnp.sin(ang)
    den = lam_re * lam_re + lam_im * lam_im
    f_re = ((ab_re - 1.0) * lam_re + ab_im * lam_im) / den
    f_im = (ab_im * lam_re - (ab_re - 1.0) * lam_im) / den
    bb_re = f_re[..., None] * b_re - f_im[..., None] * b_im
    bb_im = f_re[..., None] * b_im + f_im[..., None] * b_re
    eye = jnp.eye(gs, dtype=_F32)

    def block_diag_in(bb):
        t = bb.reshape(n_slabs, gs, p, h)
        return jnp.einsum('sgph,gk->sghkp', t, eye).reshape(n_slabs, gs * h, gs * p)

    def block_diag_out(cc):
        t = cc.reshape(n_slabs, gs, h, p)
        return jnp.einsum('sghp,gk->sgpkh', t, eye).reshape(n_slabs, gs * p, gs * h)

    wb = jnp.concatenate([block_diag_in(bb_re), block_diag_in(bb_im)], axis=2).astype(_BF16)
    cw = jnp.concatenate([block_diag_out(c_re), block_diag_out(-c_im)], axis=1).astype(_BF16)
    a_re = ab_re.reshape(n_slabs, 1, gs * p)
    a_im = ab_im.reshape(n_slabs, 1, gs * p)
    return wb, cw, a_re, a_im


def _ssm(u_tm, bsz, wb, cw, a_re, a_im, d_skip, tc):
    rows, width = u_tm.shape
    n_slabs, slab, ns2 = wb.shape
    ns = ns2 // 2
    seq = rows // bsz
    kern = functools.partial(_ssm_kernel, bsz=bsz, tc=tc, ns=ns)
    return pl.pallas_call(
        kern,
        grid=(n_slabs, seq // tc),
        in_specs=[
            pl.BlockSpec((tc * bsz, slab), lambda s, t: (t, s)),
            pl.BlockSpec((1, slab, ns2), lambda s, t: (s, 0, 0)),
            pl.BlockSpec((1, ns2, slab), lambda s, t: (s, 0, 0)),
            pl.BlockSpec((1, 1, ns), lambda s, t: (s, 0, 0)),
            pl.BlockSpec((1, 1, ns), lambda s, t: (s, 0, 0)),
            pl.BlockSpec((1, slab), lambda s, t: (0, s)),
        ],
        out_specs=pl.BlockSpec((tc * bsz, slab), lambda s, t: (t, s)),
        out_shape=jax.ShapeDtypeStruct((rows, width), _BF16),
        scratch_shapes=[
            pltpu.VMEM((tc * bsz, ns2), _F32),
            pltpu.VMEM((bsz, ns2), _F32),
        ],
        compiler_params=pltpu.CompilerParams(
            dimension_semantics=("arbitrary", "arbitrary"), vmem_limit_bytes=VMEM_LIMIT),
        name="s5",
    )(u_tm, wb, cw, a_re, a_im, d_skip.reshape(1, width))


def _post_mix_kernel(x_ref, attn_ref, z_ref, ada_ref, wglu_ref, wout_ref, gssm_ref, gpost_ref,
                     o_ref, *, d, attn_w):
    z = z_ref[...]
    zf = z.astype(_F32)
    ssm = zf * _sigmoid(jnp.dot(z, wglu_ref[...], preferred_element_type=_F32))
    ssm_n = (_rms_scale(ssm) * gssm_ref[...]).astype(_BF16)
    mix = (jnp.dot(attn_ref[0], wout_ref[0:attn_w, :], preferred_element_type=_F32)
           + jnp.dot(ssm_n, wout_ref[attn_w:, :], preferred_element_type=_F32))
    gate = ada_ref[0][:, 2 * d:3 * d]
    o_ref[0] = x_ref[0] + (1.0 + gate) * (_rms_scale(mix) * gpost_ref[...])


def _post_mix(x, attn_n, z_tm, ada, wglu_bf16, wout_bf16, g_ssm, g_post, tm):
    bsz, seq, d = x.shape
    attn_w = attn_n.shape[-1]
    ssm_w = wglu_bf16.shape[0]
    kern = functools.partial(_post_mix_kernel, d=d, attn_w=attn_w)
    return pl.pallas_call(
        kern,
        grid=(bsz, seq // tm),
        in_specs=[
            pl.BlockSpec((1, tm, d), lambda b, i: (b, i, 0)),
            pl.BlockSpec((1, tm, attn_w), lambda b, i: (b, i, 0)),
            pl.BlockSpec((tm, ssm_w), lambda b, i: (i, b)),
            pl.BlockSpec((1, 1, 6 * d), lambda b, i: (b, 0, 0)),
            _resident((ssm_w, ssm_w), lambda b, i: (0, 0)),
            _resident((attn_w + ssm_w, d), lambda b, i: (0, 0)),
            pl.BlockSpec((1, ssm_w), lambda b, i: (0, 0)),
            pl.BlockSpec((1, d), lambda b, i: (0, 0)),
        ],
        out_specs=pl.BlockSpec((1, tm, d), lambda b, i: (b, i, 0)),
        out_shape=jax.ShapeDtypeStruct((bsz, seq, d), _F32),
        compiler_params=pltpu.CompilerParams(
            dimension_semantics=("arbitrary", "arbitrary"), vmem_limit_bytes=VMEM_LIMIT),
        name="post_mix",
    )(x, attn_n, z_tm.reshape(seq, bsz * ssm_w), ada, wglu_bf16, wout_bf16,
      g_ssm.reshape(1, ssm_w), g_post.reshape(1, d))


HALO = BF16_ROWS
FFN_SUB = 2 * LANES


def _pair_columns(a, d_ff):
    lead = a.shape[:-1]
    t = a.reshape(lead + (2, d_ff // FFN_SUB, FFN_SUB))
    return jnp.swapaxes(t, -3, -2).reshape(lead + (2 * d_ff,))


def _ffn_kernel(x_ref, xh_ref, ada_ref, gpre_ref, wu_ref, cw_ref, cb_ref,
                wd_ref, gpost_ref, o_ref, h_ref, up_ref, *, d, tm, tf):
    i = pl.program_id(1)
    j = pl.program_id(2)
    ada = ada_ref[0]

    @pl.when(j == 0)
    def _():
        g = gpre_ref[...]
        scale = ada[:, 4 * d:5 * d]
        shift = ada[:, 3 * d:4 * d]
        hh = _modulated_norm(xh_ref[0], g, scale, shift)
        h_ref[0:HALO, :] = jnp.where(i > 0, hh, 0.0).astype(_BF16)
        h_ref[HALO:, :] = _modulated_norm(x_ref[0], g, scale, shift).astype(_BF16)
        o_ref[...] = jnp.zeros_like(o_ref)

    hv = h_ref[...]
    n_sub = tf // FFN_SUB
    for s in range(n_sub):
        cs = slice(2 * s * FFN_SUB, 2 * (s + 1) * FFN_SUB)
        up_ref[:, cs] = jnp.dot(hv, wu_ref[:, cs], preferred_element_type=_F32)

    for s in range(n_sub):
        cs = slice(2 * s * FFN_SUB, 2 * (s + 1) * FFN_SUB)
        cv = cb_ref[:, cs]
        for k in range(CONV_WIDTH):
            off = HALO - (CONV_WIDTH - 1) + k
            cv = cv + up_ref[pl.ds(off, tm), cs] * cw_ref[k:k + 1, cs]
        act = (_gelu_tanh(cv[:, FFN_SUB:]) * cv[:, :FFN_SUB]).astype(_BF16)
        o_ref[0] += jnp.dot(act, wd_ref[s * FFN_SUB:(s + 1) * FFN_SUB, :], preferred_element_type=_F32)

    @pl.when(j == pl.num_programs(2) - 1)
    def _():
        gt = ada[:, 5 * d:6 * d]
        o_ref[0] = x_ref[0] + (1.0 + gt) * (_rms_scale(o_ref[0]) * gpost_ref[...])


def _ffn(x, ada, g_pre, wup_bf16, conv_w, conv_b, wdown_bf16, g_post, tm, tf):
    bsz, seq, d = x.shape
    d_ff = wdown_bf16.shape[0]
    nj = d_ff // tf
    halo_blocks = tm // HALO
    kern = functools.partial(_ffn_kernel, d=d, tm=tm, tf=tf)
    cw = _pair_columns(conv_w, d_ff)
    cb = _pair_columns(conv_b, d_ff).reshape(1, 2 * d_ff)
    return pl.pallas_call(
        kern,
        grid=(bsz, seq // tm, nj),
        in_specs=[
            pl.BlockSpec((1, tm, d), lambda b, i, j: (b, i, 0)),
            pl.BlockSpec((1, HALO, d), lambda b, i, j: (b, jnp.maximum(i * halo_blocks - 1, 0), 0)),
            pl.BlockSpec((1, 1, 6 * d), lambda b, i, j: (b, 0, 0)),
            pl.BlockSpec((1, d), lambda b, i, j: (0, 0)),
            pl.BlockSpec((d, 2 * tf), lambda b, i, j: (0, j)),
            pl.BlockSpec((CONV_WIDTH, 2 * tf), lambda b, i, j: (0, j)),
            pl.BlockSpec((1, 2 * tf), lambda b, i, j: (0, j)),
            pl.BlockSpec((tf, d), lambda b, i, j: (j, 0)),
            pl.BlockSpec((1, d), lambda b, i, j: (0, 0)),
        ],
        out_specs=pl.BlockSpec((1, tm, d), lambda b, i, j: (b, i, 0)),
        out_shape=jax.ShapeDtypeStruct((bsz, seq, d), _F32),
        scratch_shapes=[
            pltpu.VMEM((tm + HALO, d), _BF16),
            pltpu.VMEM((tm + HALO, 2 * tf), _F32),
        ],
        compiler_params=pltpu.CompilerParams(
            dimension_semantics=("arbitrary", "arbitrary", "arbitrary"),
            vmem_limit_bytes=VMEM_LIMIT),
        name="conv_ffn",
    )(x, x, ada, g_pre.reshape(1, d), wup_bf16, cw, cb, wdown_bf16, g_post.reshape(1, d))


def _largest_tile(n, cap, quantum):
    best = quantum
    for t in range(quantum, min(n, cap) + 1, quantum):
        if n % t == 0:
            best = t
    return best


def kernel(x, c, w_ada, b_ada, g_pre_mix, g_post_mix, w_in, attn_sinks, lam_re, lam_im, log_step,
           ssm_b_re, ssm_b_im, ssm_c_re, ssm_c_im, ssm_d, w_glu, g_attn_out, g_ssm_out, w_out,
           g_pre_ffn, g_post_ffn, w_up, conv_w, conv_b, w_down):
    bsz, seq, d = x.shape
    depth = w_in.shape[0]
    ssm_w = w_glu.shape[1]
    attn_w = w_out.shape[1] - ssm_w
    kv_w = w_in.shape[2] - attn_w - ssm_w
    d_ff = w_down.shape[1]
    assert bsz == SUBLANES, "the scan keeps one batch row per f32 sublane"

    tm = _largest_tile(seq, 512, WINDOW)
    tf = _largest_tile(d_ff, 512, FFN_SUB)
    tc = _largest_tile(seq, 128, SUBLANES)
    slab = 2 * LANES
    qb = _largest_tile(seq, 4 * WINDOW, WINDOW) // WINDOW

    ada_all = _adaln(c, w_ada, b_ada)
    for l in range(depth):
        ada = ada_all[l][:, None, :]
        q, kv, u_tm = _in_proj(x, ada, g_pre_mix[l], w_in[l].astype(_BF16), attn_w, kv_w, tm)
        attn_n = _attention(q, kv, attn_sinks[l], g_attn_out[l], qb)
        wb, cw, a_re, a_im = _ssm_params(lam_re[l], lam_im[l], log_step[l], ssm_b_re[l], ssm_b_im[l],
                                         ssm_c_re[l], ssm_c_im[l], slab)
        z_tm = _ssm(u_tm.reshape(seq * bsz, ssm_w), bsz, wb, cw, a_re, a_im,
                    ssm_d[l].reshape(ssm_w), tc)
        x = _post_mix(x, attn_n, z_tm, ada, w_glu[l].astype(_BF16), w_out[l].astype(_BF16),
                      g_ssm_out[l], g_post_mix[l], tm)
        x = _ffn(x, ada, g_pre_ffn[l], _pair_columns(w_up[l], d_ff).astype(_BF16), conv_w[l], conv_b[l],
                 w_down[l].astype(_BF16), g_post_ffn[l], tm, tf)
    return x
```

```python
import functools
import math

import jax
import jax.numpy as jnp
from jax import lax
from jax.experimental import pallas as pl
from jax.experimental.pallas import tpu as pltpu

HEAD_DIM = 64
KV_RATIO = 8
WINDOW = 128
SSM_GROUP = 16
STATE = 64
CONV_WIDTH = 3
EPS = 1e-6
NEG = -1e30

LANES = 128
SUBLANES = 8
BF16_ROWS = 16
VMEM_LIMIT = 56 * 1024 * 1024

_BF16 = jnp.bfloat16
_F32 = jnp.float32


def _gelu_tanh(x):
    return x * (0.5 * (1.0 + jnp.tanh(math.sqrt(2.0 / math.pi) * (x + 0.044715 * (x * x * x)))))


def _sigmoid(x):
    return 1.0 / (1.0 + jnp.exp(-x))


def _rms_scale(x):
    return x * lax.rsqrt(jnp.mean(x * x, axis=-1, keepdims=True) + EPS)


def _resident(shape, index_map):
    return pl.BlockSpec(shape, index_map, pipeline_mode=pl.Buffered(1))


def _adaln_kernel(c_ref, w_ref, b_ref, o_ref):
    c = c_ref[...]
    ca = c * _sigmoid(c)
    o_ref[0] = jnp.dot(ca, w_ref[0], preferred_element_type=_F32,
                       precision=lax.Precision.HIGHEST) + b_ref[0]


def _adaln(c, w_ada, b_ada, tn=1024):
    depth, d, n = w_ada.shape
    bsz = c.shape[0]
    return pl.pallas_call(
        _adaln_kernel,
        grid=(depth, n // tn),
        in_specs=[
            pl.BlockSpec((bsz, d), lambda l, j: (0, 0)),
            pl.BlockSpec((1, d, tn), lambda l, j: (l, 0, j)),
            pl.BlockSpec((1, 1, tn), lambda l, j: (l, 0, j)),
        ],
        out_specs=pl.BlockSpec((1, bsz, tn), lambda l, j: (l, 0, j)),
        out_shape=jax.ShapeDtypeStruct((depth, bsz, n), _F32),
        compiler_params=pltpu.CompilerParams(
            dimension_semantics=("arbitrary", "arbitrary"), vmem_limit_bytes=VMEM_LIMIT),
        name="adaln",
    )(c, w_ada, b_ada.reshape(depth, 1, n))


def _modulated_norm(xv, g, scale, shift):
    return _rms_scale(xv) * (g * (1.0 + scale)) + shift


def _in_proj_kernel(x_ref, ada_ref, g_ref, w_ref, q_ref, kv_ref, u_ref, *, d, attn_w, kv_w):
    ada = ada_ref[0]
    h = _modulated_norm(x_ref[0], g_ref[...], ada[:, d:2 * d], ada[:, 0:d])
    p = jnp.dot(h.astype(_BF16), w_ref[...], preferred_element_type=_F32)
    q_ref[0] = (p[:, :attn_w] * (HEAD_DIM ** -0.5)).astype(_BF16)
    kv_ref[0] = p[:, attn_w:attn_w + kv_w].astype(_BF16)
    u_ref[...] = p[:, attn_w + kv_w:].astype(_BF16)


def _in_proj(x, ada, g, w_bf16, attn_w, kv_w, tm):
    bsz, seq, d = x.shape
    ncols = w_bf16.shape[1]
    ssm_w = ncols - attn_w - kv_w
    kern = functools.partial(_in_proj_kernel, d=d, attn_w=attn_w, kv_w=kv_w)
    return pl.pallas_call(
        kern,
        grid=(bsz, seq // tm),
        in_specs=[
            pl.BlockSpec((1, tm, d), lambda b, i: (b, i, 0)),
            pl.BlockSpec((1, 1, 6 * d), lambda b, i: (b, 0, 0)),
            pl.BlockSpec((1, d), lambda b, i: (0, 0)),
            _resident((d, ncols), lambda b, i: (0, 0)),
        ],
        out_specs=[
            pl.BlockSpec((1, tm, attn_w), lambda b, i: (b, i, 0)),
            pl.BlockSpec((1, tm, kv_w), lambda b, i: (b, i, 0)),
            pl.BlockSpec((tm, ssm_w), lambda b, i: (i, b)),
        ],
        out_shape=[
            jax.ShapeDtypeStruct((bsz, seq, attn_w), _BF16),
            jax.ShapeDtypeStruct((bsz, seq, kv_w), _BF16),
            jax.ShapeDtypeStruct((seq, bsz * ssm_w), _BF16),
        ],
        compiler_params=pltpu.CompilerParams(
            dimension_semantics=("arbitrary", "arbitrary"), vmem_limit_bytes=VMEM_LIMIT),
        name="in_proj",
    )(x, ada, g.reshape(1, d), w_bf16)


def _attention_kernel(sink_ref, q_ref, kvc_ref, kvp_ref, g_ref, o_ref, *, n_kv, qb):
    step = pl.program_id(1)
    w = WINDOW
    rows = (qb + 1) * w
    kv = jnp.concatenate([kvp_ref[0], kvc_ref[0]], axis=0).astype(_F32)
    kw = n_kv * HEAD_DIM
    k2 = kv[:, :kw]
    v2 = kv[:, kw:]
    k2r = pltpu.roll(k2, HEAD_DIM, 1)
    v2r = pltpu.roll(v2, HEAD_DIM, 1)
    lo = lax.broadcasted_iota(jnp.int32, (rows, LANES), 1) < HEAD_DIM
    one_lo = jnp.where(lo, 1.0, 0.0).astype(_BF16)
    one_hi = jnp.where(lo, 0.0, 1.0).astype(_BF16)
    kk = (jnp.where(lo, k2, k2r).astype(_BF16), jnp.where(lo, k2r, k2).astype(_BF16))
    top = (jnp.concatenate([jnp.where(lo, v2, 0.0).astype(_BF16), one_lo], axis=1),
           jnp.concatenate([jnp.where(lo, v2r, 0.0).astype(_BF16), one_lo], axis=1))
    bot = (jnp.concatenate([jnp.where(lo, 0.0, v2r).astype(_BF16), one_hi], axis=1),
           jnp.concatenate([jnp.where(lo, 0.0, v2).astype(_BF16), one_hi], axis=1))

    qi = lax.broadcasted_iota(jnp.int32, (w, 2 * w), 0)
    kj = lax.broadcasted_iota(jnp.int32, (w, 2 * w), 1)
    in_band = (kj > qi) & (kj <= qi + w)
    qlane_lo = lax.broadcasted_iota(jnp.int32, (w, LANES), 1) < HEAD_DIM
    pairs_per_kv = KV_RATIO // 2
    n_pairs = n_kv * pairs_per_kv

    def scores(blk):
        out = []
        for pair in range(n_pairs):
            j = pair // pairs_per_kv
            q2 = q_ref[0, blk * w:(blk + 1) * w, pair * LANES:(pair + 1) * LANES]
            zero = jnp.zeros_like(q2)
            kkj = kk[j][blk * w:(blk + 2) * w]
            dn = (((1,), (1,)), ((), ()))
            out.append((lax.dot_general(jnp.where(qlane_lo, q2, zero), kkj, dn, preferred_element_type=_F32),
                        lax.dot_general(jnp.where(qlane_lo, zero, q2), kkj, dn, preferred_element_type=_F32)))
        return out

    def finish(blk, s_blk):
        valid = in_band if blk > 0 else in_band & ((kj >= w) | (step > 0))
        es, sk = [], []
        for pair in range(n_pairs):
            e2, k2_ = [], []
            for half in range(2):
                sink = sink_ref[2 * pair + half]
                s = jnp.where(valid, s_blk[pair][half], NEG)
                m = jnp.maximum(jnp.max(s, axis=-1, keepdims=True), sink)
                e2.append(jnp.exp(s - m).astype(_BF16))
                k2_.append(jnp.exp(sink - m))
            es.append(jnp.concatenate(e2, axis=1))
            sk.append(jnp.where(qlane_lo, k2_[0], k2_[1]))
        outs = []
        for pair in range(n_pairs):
            j = pair // pairs_per_kv
            r = jnp.concatenate([top[j][blk * w:(blk + 2) * w], bot[j][blk * w:(blk + 2) * w]], axis=0)
            ox = jnp.dot(es[pair], r, preferred_element_type=_F32)
            outs.append(ox[:, :LANES] / (ox[:, LANES:] + sk[pair]))
        o = jnp.concatenate(outs, axis=1)
        o_ref[0, blk * w:(blk + 1) * w, :] = (_rms_scale(o) * g_ref[...]).astype(_BF16)

    s_next = scores(0)
    for blk in range(qb):
        s_cur = s_next
        if blk + 1 < qb:
            s_next = scores(blk + 1)
        finish(blk, s_cur)


def _attention(q, kv, sinks, g, qb):
    bsz, seq, attn_w = q.shape
    kv_w = kv.shape[-1]
    n_kv = kv_w // (2 * HEAD_DIM)
    assert n_kv * HEAD_DIM == LANES and attn_w == n_kv * KV_RATIO * HEAD_DIM
    tq = qb * WINDOW
    kern = functools.partial(_attention_kernel, n_kv=n_kv, qb=qb)
    return pl.pallas_call(
        kern,
        grid=(bsz, seq // tq),
        in_specs=[
            pl.BlockSpec(memory_space=pltpu.SMEM),
            pl.BlockSpec((1, tq, attn_w), lambda b, n: (b, n, 0)),
            pl.BlockSpec((1, tq, kv_w), lambda b, n: (b, n, 0)),
            pl.BlockSpec((1, WINDOW, kv_w), lambda b, n: (b, jnp.maximum(n * qb - 1, 0), 0)),
            pl.BlockSpec((1, attn_w), lambda b, n: (0, 0)),
        ],
        out_specs=pl.BlockSpec((1, tq, attn_w), lambda b, n: (b, n, 0)),
        out_shape=jax.ShapeDtypeStruct((bsz, seq, attn_w), _BF16),
        compiler_params=pltpu.CompilerParams(
            dimension_semantics=("arbitrary", "arbitrary"), vmem_limit_bytes=VMEM_LIMIT),
        name="swa",
    )(sinks, q, kv, kv, g.reshape(1, attn_w))


def _ssm_kernel(u_ref, wb_ref, cw_ref, ar_ref, ai_ref, d_ref, z_ref, xs_ref, st_ref, *, bsz, tc, ns):
    @pl.when(pl.program_id(1) == 0)
    def _():
        st_ref[...] = jnp.zeros_like(st_ref)

    uv = u_ref[...]
    xs_ref[...] = jnp.dot(uv, wb_ref[0], preferred_element_type=_F32)
    ar = jnp.broadcast_to(ar_ref[0], (bsz, ns))
    ai = jnp.broadcast_to(ai_ref[0], (bsz, ns))

    def step(t, carry):
        xr, xi = carry
        r = pl.multiple_of(t * bsz, bsz)
        nxr = ar * xr - ai * xi + xs_ref[pl.ds(r, bsz), 0:ns]
        nxi = ar * xi + ai * xr + xs_ref[pl.ds(r, bsz), ns:2 * ns]
        xs_ref[pl.ds(r, bsz), 0:ns] = nxr
        xs_ref[pl.ds(r, bsz), ns:2 * ns] = nxi
        return nxr, nxi

    xr, xi = lax.fori_loop(0, tc, step, (st_ref[:, 0:ns], st_ref[:, ns:2 * ns]), unroll=4)
    st_ref[:, 0:ns] = xr
    st_ref[:, ns:2 * ns] = xi
    y = (jnp.dot(xs_ref[...].astype(_BF16), cw_ref[0], preferred_element_type=_F32)
         + d_ref[...] * uv.astype(_F32))
    z_ref[...] = _gelu_tanh(y).astype(_BF16)


def _ssm_params(lam_re, lam_im, log_step, b_re, b_im, c_re, c_im, slab):
    g, p = lam_re.shape
    h = SSM_GROUP
    gs = slab // h
    n_slabs = g // gs
    dt = jnp.exp(log_step)[:, None]
    mag = jnp.exp(lam_re * dt)
    ang = lam_im * dt
    ab_re = mag * jnp.cos(ang)
    ab_im = mag * jnp.sin(ang)
    den = lam_re * lam_re + lam_im * lam_im
    f_re = ((ab_re - 1.0) * lam_re + ab_im * lam_im) / den
    f_im = (ab_im * lam_re - (ab_re - 1.0) * lam_im) / den
    bb_re = f_re[..., None] * b_re - f_im[..., None] * b_im
    bb_im = f_re[..., None] * b_im + f_im[..., None] * b_re
    eye = jnp.eye(gs, dtype=_F32)

    def block_diag_in(bb):
        t = bb.reshape(n_slabs, gs, p, h)
        return jnp.einsum('sgph,gk->sghkp', t, eye).reshape(n_slabs, gs * h, gs * p)

    def block_diag_out(cc):
        t = cc.reshape(n_slabs, gs, h, p)
        return jnp.einsum('sghp,gk->sgpkh', t, eye).reshape(n_slabs, gs * p, gs * h)

    wb = jnp.concatenate([block_diag_in(bb_re), block_diag_in(bb_im)], axis=2).astype(_BF16)
    cw = jnp.concatenate([block_diag_out(c_re), block_diag_out(-c_im)], axis=1).astype(_BF16)
    a_re = ab_re.reshape(n_slabs, 1, gs * p)
    a_im = ab_im.reshape(n_slabs, 1, gs * p)
    return wb, cw, a_re, a_im


def _ssm(u_tm, bsz, wb, cw, a_re, a_im, d_skip, tc):
    rows, width = u_tm.shape
    n_slabs, slab, ns2 = wb.shape
    ns = ns2 // 2
    seq = rows // bsz
    kern = functools.partial(_ssm_kernel, bsz=bsz, tc=tc, ns=ns)
    return pl.pallas_call(
        kern,
        grid=(n_slabs, seq // tc),
        in_specs=[
            pl.BlockSpec((tc * bsz, slab), lambda s, t: (t, s)),
            pl.BlockSpec((1, slab, ns2), lambda s, t: (s, 0, 0)),
            pl.BlockSpec((1, ns2, slab), lambda s, t: (s, 0, 0)),
            pl.BlockSpec((1, 1, ns), lambda s, t: (s, 0, 0)),
            pl.BlockSpec((1, 1, ns), lambda s, t: (s, 0, 0)),
            pl.BlockSpec((1, slab), lambda s, t: (0, s)),
        ],
        out_specs=pl.BlockSpec((tc * bsz, slab), lambda s, t: (t, s)),
        out_shape=jax.ShapeDtypeStruct((rows, width), _BF16),
        scratch_shapes=[
            pltpu.VMEM((tc * bsz, ns2), _F32),
            pltpu.VMEM((bsz, ns2), _F32),
        ],
        compiler_params=pltpu.CompilerParams(
            dimension_semantics=("arbitrary", "arbitrary"), vmem_limit_bytes=VMEM_LIMIT),
        name="s5",
    )(u_tm, wb, cw, a_re, a_im, d_skip.reshape(1, width))


def _post_mix_kernel(x_ref, attn_ref, z_ref, ada_ref, wglu_ref, wout_ref, gssm_ref, gpost_ref,
                     o_ref, *, d, attn_w):
    z = z_ref[...]
    zf = z.astype(_F32)
    ssm = zf * _sigmoid(jnp.dot(z, wglu_ref[...], preferred_element_type=_F32))
    ssm_n = (_rms_scale(ssm) * gssm_ref[...]).astype(_BF16)
    mix = (jnp.dot(attn_ref[0], wout_ref[0:attn_w, :], preferred_element_type=_F32)
           + jnp.dot(ssm_n, wout_ref[attn_w:, :], preferred_element_type=_F32))
    gate = ada_ref[0][:, 2 * d:3 * d]
    o_ref[0] = x_ref[0] + (1.0 + gate) * (_rms_scale(mix) * gpost_ref[...])


def _post_mix(x, attn_n, z_tm, ada, wglu_bf16, wout_bf16, g_ssm, g_post, tm):
    bsz, seq, d = x.shape
    attn_w = attn_n.shape[-1]
    ssm_w = wglu_bf16.shape[0]
    kern = functools.partial(_post_mix_kernel, d=d, attn_w=attn_w)
    return pl.pallas_call(
        kern,
        grid=(bsz, seq // tm),
        in_specs=[
            pl.BlockSpec((1, tm, d), lambda b, i: (b, i, 0)),
            pl.BlockSpec((1, tm, attn_w), lambda b, i: (b, i, 0)),
            pl.BlockSpec((tm, ssm_w), lambda b, i: (i, b)),
            pl.BlockSpec((1, 1, 6 * d), lambda b, i: (b, 0, 0)),
            _resident((ssm_w, ssm_w), lambda b, i: (0, 0)),
            _resident((attn_w + ssm_w, d), lambda b, i: (0, 0)),
            pl.BlockSpec((1, ssm_w), lambda b, i: (0, 0)),
            pl.BlockSpec((1, d), lambda b, i: (0, 0)),
        ],
        out_specs=pl.BlockSpec((1, tm, d), lambda b, i: (b, i, 0)),
        out_shape=jax.ShapeDtypeStruct((bsz, seq, d), _F32),
        compiler_params=pltpu.CompilerParams(
            dimension_semantics=("arbitrary", "arbitrary"), vmem_limit_bytes=VMEM_LIMIT),
        name="post_mix",
    )(x, attn_n, z_tm.reshape(seq, bsz * ssm_w), ada, wglu_bf16, wout_bf16,
      g_ssm.reshape(1, ssm_w), g_post.reshape(1, d))


HALO = BF16_ROWS
FFN_SUB = 2 * LANES
FFN_ROWS = 64


def _ffn_kernel(x_ref, xh_ref, ada_ref, gpre_ref, wv_ref, wg_ref, cwv_ref, cwg_ref, cbv_ref, cbg_ref,
                wd_ref, wdt_ref, gpost_ref, o_ref, h_ref, upv_ref, upg_ref, act_ref, *, d, tm):
    i = pl.program_id(1)
    j = pl.program_id(2)
    ada = ada_ref[0]
    sub = FFN_SUB

    @pl.when(j == 0)
    def _():
        g = gpre_ref[...]
        scale = ada[:, 4 * d:5 * d]
        shift = ada[:, 3 * d:4 * d]
        hh = _modulated_norm(xh_ref[0], g, scale, shift)
        h_ref[0:HALO, :] = jnp.where(i > 0, hh, 0.0).astype(_BF16)
        h_ref[HALO:, :] = _modulated_norm(x_ref[0], g, scale, shift).astype(_BF16)
        o_ref[...] = jnp.zeros_like(o_ref)
        act_ref[0, :, 0:sub] = jnp.zeros((tm, sub), _BF16)

    hv = h_ref[...]
    tiles = sub // LANES
    for s in range(2):
        cs = slice(s * sub, (s + 1) * sub)
        rv = jnp.dot(hv, wv_ref[:, cs], preferred_element_type=_F32)
        rg = jnp.dot(hv, wg_ref[:, cs], preferred_element_type=_F32)
        for c in range(tiles):
            upv_ref[s * tiles + c] = rv[:, c * LANES:(c + 1) * LANES]
            upg_ref[s * tiles + c] = rg[:, c * LANES:(c + 1) * LANES]

    def conv(up_ref, cw_ref, cb_ref, t, r):
        cs = slice(t * LANES, (t + 1) * LANES)
        out = cb_ref[:, cs]
        for k in range(CONV_WIDTH):
            off = HALO - (CONV_WIDTH - 1) + k + r
            out = out + up_ref[t, pl.ds(off, FFN_ROWS, stride=1), :] * cw_ref[k:k + 1, cs]
        return out

    def activation(dst, s, col0):
        for c in range(tiles):
            t = s * tiles + c
            for r in range(0, tm, FFN_ROWS):
                a = (_gelu_tanh(conv(upg_ref, cwg_ref, cbg_ref, t, r))
                     * conv(upv_ref, cwv_ref, cbv_ref, t, r))
                dst[r:r + FFN_ROWS, col0 + c * LANES:col0 + (c + 1) * LANES] = a.astype(_BF16)

    slot = j % 2
    cur = act_ref.at[slot]
    nxt = act_ref.at[1 - slot]
    activation(cur, 0, sub)
    o_ref[0] += jnp.dot(cur[...], wd_ref[...], preferred_element_type=_F32)
    activation(nxt, 1, 0)

    @pl.when(j == pl.num_programs(2) - 1)
    def _():
        ff = o_ref[0] + jnp.dot(nxt[:, 0:sub], wdt_ref[...], preferred_element_type=_F32)
        gt = ada[:, 5 * d:6 * d]
        o_ref[0] = x_ref[0] + (1.0 + gt) * (_rms_scale(ff) * gpost_ref[...])


def _ffn(x, ada, g_pre, wup_bf16, conv_w, conv_b, wdown_bf16, g_post, tm):
    bsz, seq, d = x.shape
    d_ff = wdown_bf16.shape[0]
    tf = 2 * FFN_SUB
    nj = d_ff // tf
    assert nj * tf == d_ff
    halo_blocks = tm // HALO
    kern = functools.partial(_ffn_kernel, d=d, tm=tm)
    cb = conv_b.reshape(1, 2 * d_ff)
    wd = jnp.concatenate([jnp.zeros((FFN_SUB, d), _BF16), wdown_bf16], axis=0)
    return pl.pallas_call(
        kern,
        grid=(bsz, seq // tm, nj),
        in_specs=[
            pl.BlockSpec((1, tm, d), lambda b, i, j: (b, i, 0)),
            pl.BlockSpec((1, HALO, d), lambda b, i, j: (b, jnp.maximum(i * halo_blocks - 1, 0), 0)),
            pl.BlockSpec((1, 1, 6 * d), lambda b, i, j: (b, 0, 0)),
            pl.BlockSpec((1, d), lambda b, i, j: (0, 0)),
            pl.BlockSpec((d, tf), lambda b, i, j: (0, j)),
            pl.BlockSpec((d, tf), lambda b, i, j: (0, nj + j)),
            pl.BlockSpec((CONV_WIDTH, tf), lambda b, i, j: (0, j)),
            pl.BlockSpec((CONV_WIDTH, tf), lambda b, i, j: (0, nj + j)),
            pl.BlockSpec((1, tf), lambda b, i, j: (0, j)),
            pl.BlockSpec((1, tf), lambda b, i, j: (0, nj + j)),
            pl.BlockSpec((tf, d), lambda b, i, j: (j, 0)),
            _resident((FFN_SUB, d), lambda b, i, j: (d_ff // FFN_SUB, 0)),
            pl.BlockSpec((1, d), lambda b, i, j: (0, 0)),
        ],
        out_specs=pl.BlockSpec((1, tm, d), lambda b, i, j: (b, i, 0)),
        out_shape=jax.ShapeDtypeStruct((bsz, seq, d), _F32),
        scratch_shapes=[
            pltpu.VMEM((tm + HALO, d), _BF16),
            pltpu.VMEM((tf // LANES, tm + HALO, LANES), _F32),
            pltpu.VMEM((tf // LANES, tm + HALO, LANES), _F32),
            pltpu.VMEM((2, tm, tf), _BF16),
        ],
        compiler_params=pltpu.CompilerParams(
            dimension_semantics=("arbitrary", "arbitrary", "arbitrary"),
            vmem_limit_bytes=VMEM_LIMIT),
        name="conv_ffn",
    )(x, x, ada, g_pre.reshape(1, d), wup_bf16, wup_bf16, conv_w, conv_w, cb, cb,
      wd, wd, g_post.reshape(1, d))


def _largest_tile(n, cap, quantum):
    best = quantum
    for t in range(quantum, min(n, cap) + 1, quantum):
        if n % t == 0:
            best = t
    return best


def kernel(x, c, w_ada, b_ada, g_pre_mix, g_post_mix, w_in, attn_sinks, lam_re, lam_im, log_step,
           ssm_b_re, ssm_b_im, ssm_c_re, ssm_c_im, ssm_d, w_glu, g_attn_out, g_ssm_out, w_out,
           g_pre_ffn, g_post_ffn, w_up, conv_w, conv_b, w_down):
    bsz, seq, d = x.shape
    depth = w_in.shape[0]
    ssm_w = w_glu.shape[1]
    attn_w = w_out.shape[1] - ssm_w
    kv_w = w_in.shape[2] - attn_w - ssm_w
    d_ff = w_down.shape[1]
    assert bsz == SUBLANES, "the scan keeps one batch row per f32 sublane"

    tm = _largest_tile(seq, 512, WINDOW)
    tc = _largest_tile(seq, 128, SUBLANES)
    slab = 2 * LANES
    qb = _largest_tile(seq, 4 * WINDOW, WINDOW) // WINDOW

    ada_all = _adaln(c, w_ada, b_ada)
    for l in range(depth):
        ada = ada_all[l][:, None, :]
        q, kv, u_tm = _in_proj(x, ada, g_pre_mix[l], w_in[l].astype(_BF16), attn_w, kv_w, tm)
        attn_n = _attention(q, kv, attn_sinks[l], g_attn_out[l], qb)
        wb, cw, a_re, a_im = _ssm_params(lam_re[l], lam_im[l], log_step[l], ssm_b_re[l], ssm_b_im[l],
                                         ssm_c_re[l], ssm_c_im[l], slab)
        z_tm = _ssm(u_tm.reshape(seq * bsz, ssm_w), bsz, wb, cw, a_re, a_im,
                    ssm_d[l].reshape(ssm_w), tc)
        x = _post_mix(x, attn_n, z_tm, ada, w_glu[l].astype(_BF16), w_out[l].astype(_BF16),
                      g_ssm_out[l], g_post_mix[l], tm)
        x = _ffn(x, ada, g_pre_ffn[l], w_up[l].astype(_BF16), conv_w[l], conv_b[l],
                 w_down[l].astype(_BF16), g_post_ffn[l], tm)
    return x
```

```python
import functools
import math

import jax
import jax.numpy as jnp
from jax import lax
from jax.experimental import pallas as pl
from jax.experimental.pallas import tpu as pltpu

HEAD_DIM = 64
KV_RATIO = 8
WINDOW = 128
SSM_GROUP = 16
STATE = 64
CONV_WIDTH = 3
EPS = 1e-6
NEG = -1e30

LANES = 128
SUBLANES = 8
BF16_ROWS = 16
NORM_ROWS = BF16_ROWS
VMEM_LIMIT = 56 * 1024 * 1024

_BF16 = jnp.bfloat16
_F32 = jnp.float32


def _gelu_tanh(x):
    return x * (0.5 * (1.0 + jnp.tanh(math.sqrt(2.0 / math.pi) * (x + 0.044715 * (x * x * x)))))


def _sigmoid(x):
    return 1.0 / (1.0 + jnp.exp(-x))


def _rms_scale(x):
    return x * lax.rsqrt(jnp.mean(x * x, axis=-1, keepdims=True) + EPS)


def _resident(shape, index_map):
    return pl.BlockSpec(shape, index_map, pipeline_mode=pl.Buffered(1))


def _adaln_kernel(c_ref, w_ref, b_ref, o_ref):
    c = c_ref[...]
    ca = c * _sigmoid(c)
    o_ref[0] = jnp.dot(ca, w_ref[0], preferred_element_type=_F32,
                       precision=lax.Precision.HIGHEST) + b_ref[0]


def _adaln(c, w_ada, b_ada, tn=1024):
    depth, d, n = w_ada.shape
    bsz = c.shape[0]
    return pl.pallas_call(
        _adaln_kernel,
        grid=(depth, n // tn),
        in_specs=[
            pl.BlockSpec((bsz, d), lambda l, j: (0, 0)),
            pl.BlockSpec((1, d, tn), lambda l, j: (l, 0, j)),
            pl.BlockSpec((1, 1, tn), lambda l, j: (l, 0, j)),
        ],
        out_specs=pl.BlockSpec((1, bsz, tn), lambda l, j: (l, 0, j)),
        out_shape=jax.ShapeDtypeStruct((depth, bsz, n), _F32),
        compiler_params=pltpu.CompilerParams(
            dimension_semantics=("arbitrary", "arbitrary"), vmem_limit_bytes=VMEM_LIMIT),
        name="adaln",
    )(c, w_ada, b_ada.reshape(depth, 1, n))


def _modulated_norm(xv, g, scale, shift):
    return _rms_scale(xv) * (g * (1.0 + scale)) + shift


def _in_proj_kernel(x_ref, ada_ref, g_ref, w_ref, q_ref, kv_ref, u_ref, *, d, attn_w, kv_w):
    ada = ada_ref[0]
    h = _modulated_norm(x_ref[0], g_ref[...], ada[:, d:2 * d], ada[:, 0:d])
    p = jnp.dot(h.astype(_BF16), w_ref[...], preferred_element_type=_F32)
    q_ref[0] = (p[:, :attn_w] * (HEAD_DIM ** -0.5)).astype(_BF16)
    kv_ref[0] = p[:, attn_w:attn_w + kv_w].astype(_BF16)
    u_ref[...] = p[:, attn_w + kv_w:].astype(_BF16)


def _in_proj(x, ada, g, w_bf16, attn_w, kv_w, tm):
    bsz, seq, d = x.shape
    ncols = w_bf16.shape[1]
    ssm_w = ncols - attn_w - kv_w
    kern = functools.partial(_in_proj_kernel, d=d, attn_w=attn_w, kv_w=kv_w)
    return pl.pallas_call(
        kern,
        grid=(bsz, seq // tm),
        in_specs=[
            pl.BlockSpec((1, tm, d), lambda b, i: (b, i, 0)),
            pl.BlockSpec((1, 1, 6 * d), lambda b, i: (b, 0, 0)),
            pl.BlockSpec((1, d), lambda b, i: (0, 0)),
            _resident((d, ncols), lambda b, i: (0, 0)),
        ],
        out_specs=[
            pl.BlockSpec((1, tm, attn_w), lambda b, i: (b, i, 0)),
            pl.BlockSpec((1, tm, kv_w), lambda b, i: (b, i, 0)),
            pl.BlockSpec((tm, ssm_w), lambda b, i: (i, b)),
        ],
        out_shape=[
            jax.ShapeDtypeStruct((bsz, seq, attn_w), _BF16),
            jax.ShapeDtypeStruct((bsz, seq, kv_w), _BF16),
            jax.ShapeDtypeStruct((seq, bsz * ssm_w), _BF16),
        ],
        compiler_params=pltpu.CompilerParams(
            dimension_semantics=("arbitrary", "arbitrary"), vmem_limit_bytes=VMEM_LIMIT),
        name="in_proj",
    )(x, ada, g.reshape(1, d), w_bf16)


def _attention_kernel(sink_ref, q_ref, kvc_ref, kvp_ref, g_ref, o_ref, *, n_kv, qb):
    step = pl.program_id(1)
    w = WINDOW
    rows = (qb + 1) * w
    kv = jnp.concatenate([kvp_ref[0], kvc_ref[0]], axis=0).astype(_F32)
    kw = n_kv * HEAD_DIM
    k2 = kv[:, :kw]
    v2 = kv[:, kw:]
    k2r = pltpu.roll(k2, HEAD_DIM, 1)
    v2r = pltpu.roll(v2, HEAD_DIM, 1)
    lo = lax.broadcasted_iota(jnp.int32, (rows, LANES), 1) < HEAD_DIM
    one_lo = jnp.where(lo, 1.0, 0.0).astype(_BF16)
    one_hi = jnp.where(lo, 0.0, 1.0).astype(_BF16)
    kk = (jnp.where(lo, k2, k2r).astype(_BF16), jnp.where(lo, k2r, k2).astype(_BF16))
    top = (jnp.concatenate([jnp.where(lo, v2, 0.0).astype(_BF16), one_lo], axis=1),
           jnp.concatenate([jnp.where(lo, v2r, 0.0).astype(_BF16), one_lo], axis=1))
    bot = (jnp.concatenate([jnp.where(lo, 0.0, v2r).astype(_BF16), one_hi], axis=1),
           jnp.concatenate([jnp.where(lo, 0.0, v2).astype(_BF16), one_hi], axis=1))

    qi = lax.broadcasted_iota(jnp.int32, (w, 2 * w), 0)
    kj = lax.broadcasted_iota(jnp.int32, (w, 2 * w), 1)
    in_band = (kj > qi) & (kj <= qi + w)
    qlane_lo = lax.broadcasted_iota(jnp.int32, (w, LANES), 1) < HEAD_DIM
    pairs_per_kv = KV_RATIO // 2
    n_pairs = n_kv * pairs_per_kv

    def scores(blk):
        out = []
        for pair in range(n_pairs):
            j = pair // pairs_per_kv
            q2 = q_ref[0, blk * w:(blk + 1) * w, pair * LANES:(pair + 1) * LANES]
            zero = jnp.zeros_like(q2)
            kkj = kk[j][blk * w:(blk + 2) * w]
            dn = (((1,), (1,)), ((), ()))
            out.append((lax.dot_general(jnp.where(qlane_lo, q2, zero), kkj, dn, preferred_element_type=_F32),
                        lax.dot_general(jnp.where(qlane_lo, zero, q2), kkj, dn, preferred_element_type=_F32)))
        return out

    def finish(blk, s_blk):
        valid = in_band if blk > 0 else in_band & ((kj >= w) | (step > 0))
        es, sk = [], []
        for pair in range(n_pairs):
            e2, k2_ = [], []
            for half in range(2):
                sink = sink_ref[2 * pair + half]
                s = jnp.where(valid, s_blk[pair][half], NEG)
                m = jnp.maximum(jnp.max(s, axis=-1, keepdims=True), sink)
                e2.append(jnp.exp(s - m).astype(_BF16))
                k2_.append(jnp.exp(sink - m))
            es.append(jnp.concatenate(e2, axis=1))
            sk.append(jnp.where(qlane_lo, k2_[0], k2_[1]))
        outs = []
        for pair in range(n_pairs):
            j = pair // pairs_per_kv
            r = jnp.concatenate([top[j][blk * w:(blk + 2) * w], bot[j][blk * w:(blk + 2) * w]], axis=0)
            ox = jnp.dot(es[pair], r, preferred_element_type=_F32)
            outs.append(ox[:, :LANES] / (ox[:, LANES:] + sk[pair]))
        o = jnp.concatenate(outs, axis=1)
        o_ref[0, blk * w:(blk + 1) * w, :] = (_rms_scale(o) * g_ref[...]).astype(_BF16)

    s_next = scores(0)
    for blk in range(qb):
        s_cur = s_next
        if blk + 1 < qb:
            s_next = scores(blk + 1)
        finish(blk, s_cur)


def _attention(q, kv, sinks, g, qb):
    bsz, seq, attn_w = q.shape
    kv_w = kv.shape[-1]
    n_kv = kv_w // (2 * HEAD_DIM)
    assert n_kv * HEAD_DIM == LANES and attn_w == n_kv * KV_RATIO * HEAD_DIM
    tq = qb * WINDOW
    kern = functools.partial(_attention_kernel, n_kv=n_kv, qb=qb)
    return pl.pallas_call(
        kern,
        grid=(bsz, seq // tq),
        in_specs=[
            pl.BlockSpec(memory_space=pltpu.SMEM),
            pl.BlockSpec((1, tq, attn_w), lambda b, n: (b, n, 0)),
            pl.BlockSpec((1, tq, kv_w), lambda b, n: (b, n, 0)),
            pl.BlockSpec((1, WINDOW, kv_w), lambda b, n: (b, jnp.maximum(n * qb - 1, 0), 0)),
            pl.BlockSpec((1, attn_w), lambda b, n: (0, 0)),
        ],
        out_specs=pl.BlockSpec((1, tq, attn_w), lambda b, n: (b, n, 0)),
        out_shape=jax.ShapeDtypeStruct((bsz, seq, attn_w), _BF16),
        compiler_params=pltpu.CompilerParams(
            dimension_semantics=("arbitrary", "arbitrary"), vmem_limit_bytes=VMEM_LIMIT),
        name="swa",
    )(sinks, q, kv, kv, g.reshape(1, attn_w))


def _ssm_kernel(u_ref, wb_ref, cw_ref, ar_ref, ai_ref, d_ref, z_ref, xs_ref, st_ref, *, bsz, tc, ns):
    @pl.when(pl.program_id(1) == 0)
    def _():
        st_ref[...] = jnp.zeros_like(st_ref)

    uv = u_ref[...]
    hr = tc * bsz // 2
    xs_ref[0:hr, :] = jnp.dot(uv[0:hr], wb_ref[0], preferred_element_type=_F32)
    xs_ref[hr:, :] = jnp.dot(uv[hr:], wb_ref[0], preferred_element_type=_F32)
    ar = jnp.broadcast_to(ar_ref[0], (bsz, ns))
    ai = jnp.broadcast_to(ai_ref[0], (bsz, ns))

    def step(t, carry):
        xr, xi = carry
        r = pl.multiple_of(t * bsz, bsz)
        nxr = ar * xr - ai * xi + xs_ref[pl.ds(r, bsz), 0:ns]
        nxi = ar * xi + ai * xr + xs_ref[pl.ds(r, bsz), ns:2 * ns]
        xs_ref[pl.ds(r, bsz), 0:ns] = nxr
        xs_ref[pl.ds(r, bsz), ns:2 * ns] = nxi
        return nxr, nxi

    xr, xi = lax.fori_loop(0, tc, step, (st_ref[:, 0:ns], st_ref[:, ns:2 * ns]), unroll=4)
    st_ref[:, 0:ns] = xr
    st_ref[:, ns:2 * ns] = xi
    y = (jnp.dot(xs_ref[:, 0:ns].astype(_BF16), cw_ref[0, 0:ns, :], preferred_element_type=_F32)
         + jnp.dot(xs_ref[:, ns:2 * ns].astype(_BF16), cw_ref[0, ns:2 * ns, :], preferred_element_type=_F32)
         + d_ref[...] * uv.astype(_F32))
    z_ref[...] = _gelu_tanh(y).astype(_BF16)


def _ssm_params(lam_re, lam_im, log_step, b_re, b_im, c_re, c_im, slab):
    g, p = lam_re.shape
    h = SSM_GROUP
    gs = slab // h
    n_slabs = g // gs
    dt = jnp.exp(log_step)[:, None]
    mag = jnp.exp(lam_re * dt)
    ang = lam_im * dt
    ab_re = mag * jnp.cos(ang)
    ab_im = mag * jnp.sin(ang)
    den = lam_re * lam_re + lam_im * lam_im
    f_re = ((ab_re - 1.0) * lam_re + ab_im * lam_im) / den
    f_im = (ab_im * lam_re - (ab_re - 1.0) * lam_im) / den
    bb_re = f_re[..., None] * b_re - f_im[..., None] * b_im
    bb_im = f_re[..., None] * b_im + f_im[..., None] * b_re
    eye = jnp.eye(gs, dtype=_F32)

    def block_diag_in(bb):
        t = bb.reshape(n_slabs, gs, p, h)
        return jnp.einsum('sgph,gk->sghkp', t, eye).reshape(n_slabs, gs * h, gs * p)

    def block_diag_out(cc):
        t = cc.reshape(n_slabs, gs, h, p)
        return jnp.einsum('sghp,gk->sgpkh', t, eye).reshape(n_slabs, gs * p, gs * h)

    wb = jnp.concatenate([block_diag_in(bb_re), block_diag_in(bb_im)], axis=2).astype(_BF16)
    cw = jnp.concatenate([block_diag_out(c_re), block_diag_out(-c_im)], axis=1).astype(_BF16)
    a_re = ab_re.reshape(n_slabs, 1, gs * p)
    a_im = ab_im.reshape(n_slabs, 1, gs * p)
    return wb, cw, a_re, a_im


def _ssm(u_tm, bsz, wb, cw, a_re, a_im, d_skip, tc):
    rows, width = u_tm.shape
    n_slabs, slab, ns2 = wb.shape
    ns = ns2 // 2
    seq = rows // bsz
    kern = functools.partial(_ssm_kernel, bsz=bsz, tc=tc, ns=ns)
    return pl.pallas_call(
        kern,
        grid=(n_slabs, seq // tc),
        in_specs=[
            pl.BlockSpec((tc * bsz, slab), lambda s, t: (t, s)),
            pl.BlockSpec((1, slab, ns2), lambda s, t: (s, 0, 0)),
            pl.BlockSpec((1, ns2, slab), lambda s, t: (s, 0, 0)),
            pl.BlockSpec((1, 1, ns), lambda s, t: (s, 0, 0)),
            pl.BlockSpec((1, 1, ns), lambda s, t: (s, 0, 0)),
            pl.BlockSpec((1, slab), lambda s, t: (0, s)),
        ],
        out_specs=pl.BlockSpec((tc * bsz, slab), lambda s, t: (t, s)),
        out_shape=jax.ShapeDtypeStruct((rows, width), _BF16),
        scratch_shapes=[
            pltpu.VMEM((tc * bsz, ns2), _F32),
            pltpu.VMEM((bsz, ns2), _F32),
        ],
        compiler_params=pltpu.CompilerParams(
            dimension_semantics=("arbitrary", "arbitrary"), vmem_limit_bytes=VMEM_LIMIT),
        name="s5",
    )(u_tm, wb, cw, a_re, a_im, d_skip.reshape(1, width))


def _post_mix_kernel(x_ref, attn_ref, z_ref, ada_ref, wglu_ref, wout_ref, gssm_ref, gpost_ref,
                     o_ref, *, d, attn_w):
    z = z_ref[...]
    zf = z.astype(_F32)
    ssm = zf * _sigmoid(jnp.dot(z, wglu_ref[...], preferred_element_type=_F32))
    ssm_n = (_rms_scale(ssm) * gssm_ref[...]).astype(_BF16)
    o_ref[0] = (jnp.dot(attn_ref[0], wout_ref[0:attn_w, :], preferred_element_type=_F32)
                + jnp.dot(ssm_n, wout_ref[attn_w:, :], preferred_element_type=_F32))
    gg = (1.0 + ada_ref[0][:, 2 * d:3 * d]) * gpost_ref[...]
    for r in range(0, o_ref.shape[1], NORM_ROWS):
        rows = slice(r, r + NORM_ROWS)
        o_ref[0, rows, :] = x_ref[0, rows, :] + _rms_scale(o_ref[0, rows, :]) * gg


def _post_mix(x, attn_n, z_tm, ada, wglu_bf16, wout_bf16, g_ssm, g_post, tm):
    bsz, seq, d = x.shape
    attn_w = attn_n.shape[-1]
    ssm_w = wglu_bf16.shape[0]
    kern = functools.partial(_post_mix_kernel, d=d, attn_w=attn_w)
    return pl.pallas_call(
        kern,
        grid=(bsz, seq // tm),
        in_specs=[
            pl.BlockSpec((1, tm, d), lambda b, i: (b, i, 0)),
            pl.BlockSpec((1, tm, attn_w), lambda b, i: (b, i, 0)),
            pl.BlockSpec((tm, ssm_w), lambda b, i: (i, b)),
            pl.BlockSpec((1, 1, 6 * d), lambda b, i: (b, 0, 0)),
            _resident((ssm_w, ssm_w), lambda b, i: (0, 0)),
            _resident((attn_w + ssm_w, d), lambda b, i: (0, 0)),
            pl.BlockSpec((1, ssm_w), lambda b, i: (0, 0)),
            pl.BlockSpec((1, d), lambda b, i: (0, 0)),
        ],
        out_specs=pl.BlockSpec((1, tm, d), lambda b, i: (b, i, 0)),
        out_shape=jax.ShapeDtypeStruct((bsz, seq, d), _F32),
        compiler_params=pltpu.CompilerParams(
            dimension_semantics=("arbitrary", "arbitrary"), vmem_limit_bytes=VMEM_LIMIT),
        name="post_mix",
    )(x, attn_n, z_tm.reshape(seq, bsz * ssm_w), ada, wglu_bf16, wout_bf16,
      g_ssm.reshape(1, ssm_w), g_post.reshape(1, d))


HALO = BF16_ROWS
FFN_SUB = 2 * LANES
FFN_ROWS = 64


def _ffn_kernel(x_ref, xh_ref, ada_ref, gpre_ref, wv_ref, wg_ref, cwv_ref, cwg_ref, cbv_ref, cbg_ref,
                wd_ref, wdt_ref, gpost_ref, o_ref, h_ref, upv_ref, upg_ref, act_ref, carry_ref, *, d, tm):
    i = pl.program_id(1)
    j = pl.program_id(2)
    ada = ada_ref[0]
    sub = FFN_SUB

    @pl.when(j == 0)
    def _():
        g = gpre_ref[...]
        scale = ada[:, 4 * d:5 * d]
        shift = ada[:, 3 * d:4 * d]
        hh = _modulated_norm(xh_ref[0], g, scale, shift)
        h_ref[0:HALO, :] = jnp.where(i > 0, hh, 0.0).astype(_BF16)
        gs = g * (1.0 + scale)
        for r in range(0, tm, NORM_ROWS):
            xb = x_ref[0, r:r + NORM_ROWS, :]
            h_ref[HALO + r:HALO + r + NORM_ROWS, :] = (_rms_scale(xb) * gs + shift).astype(_BF16)
        o_ref[...] = jnp.zeros_like(o_ref)
        carry_ref[...] = jnp.zeros_like(carry_ref)

    hv = h_ref[...]
    tiles = sub // LANES
    for s in range(2):
        cs = slice(s * sub, (s + 1) * sub)
        rv = jnp.dot(hv, wv_ref[:, cs], preferred_element_type=_F32)
        rg = jnp.dot(hv, wg_ref[:, cs], preferred_element_type=_F32)
        for c in range(tiles):
            upv_ref[s * tiles + c] = rv[:, c * LANES:(c + 1) * LANES]
            upg_ref[s * tiles + c] = rg[:, c * LANES:(c + 1) * LANES]

    def conv(up_ref, cw_ref, cb_ref, t, r):
        cs = slice(t * LANES, (t + 1) * LANES)
        out = cb_ref[:, cs]
        for k in range(CONV_WIDTH):
            off = HALO - (CONV_WIDTH - 1) + k + r
            out = out + up_ref[t, pl.ds(off, FFN_ROWS, stride=1), :] * cw_ref[k:k + 1, cs]
        return out

    def activation(dst, s, col0):
        for c in range(tiles):
            t = s * tiles + c
            for r in range(0, tm, FFN_ROWS):
                a = (_gelu_tanh(conv(upg_ref, cwg_ref, cbg_ref, t, r))
                     * conv(upv_ref, cwv_ref, cbv_ref, t, r))
                dst[r:r + FFN_ROWS, col0 + c * LANES:col0 + (c + 1) * LANES] = a.astype(_BF16)

    act_ref[:, 0:sub] = carry_ref[...]
    activation(act_ref, 0, sub)
    o_ref[0] += jnp.dot(act_ref[...], wd_ref[...], preferred_element_type=_F32)
    activation(carry_ref, 1, 0)

    @pl.when(j == pl.num_programs(2) - 1)
    def _():
        o_ref[0] += jnp.dot(carry_ref[...], wdt_ref[...], preferred_element_type=_F32)
        gg = (1.0 + ada[:, 5 * d:6 * d]) * gpost_ref[...]
        for r in range(0, tm, NORM_ROWS):
            rows = slice(r, r + NORM_ROWS)
            o_ref[0, rows, :] = x_ref[0, rows, :] + _rms_scale(o_ref[0, rows, :]) * gg


def _ffn(x, ada, g_pre, wup_bf16, conv_w, conv_b, wdown_bf16, g_post, tm):
    bsz, seq, d = x.shape
    d_ff = wdown_bf16.shape[0]
    tf = 2 * FFN_SUB
    nj = d_ff // tf
    assert nj * tf == d_ff
    halo_blocks = tm // HALO
    kern = functools.partial(_ffn_kernel, d=d, tm=tm)
    cb = conv_b.reshape(1, 2 * d_ff)
    wd = jnp.concatenate([jnp.zeros((FFN_SUB, d), _BF16), wdown_bf16], axis=0)
    return pl.pallas_call(
        kern,
        grid=(bsz, seq // tm, nj),
        in_specs=[
            pl.BlockSpec((1, tm, d), lambda b, i, j: (b, i, 0)),
            pl.BlockSpec((1, HALO, d), lambda b, i, j: (b, jnp.maximum(i * halo_blocks - 1, 0), 0)),
            pl.BlockSpec((1, 1, 6 * d), lambda b, i, j: (b, 0, 0)),
            pl.BlockSpec((1, d), lambda b, i, j: (0, 0)),
            pl.BlockSpec((d, tf), lambda b, i, j: (0, j)),
            pl.BlockSpec((d, tf), lambda b, i, j: (0, nj + j)),
            pl.BlockSpec((CONV_WIDTH, tf), lambda b, i, j: (0, j)),
            pl.BlockSpec((CONV_WIDTH, tf), lambda b, i, j: (0, nj + j)),
            pl.BlockSpec((1, tf), lambda b, i, j: (0, j)),
            pl.BlockSpec((1, tf), lambda b, i, j: (0, nj + j)),
            pl.BlockSpec((tf, d), lambda b, i, j: (j, 0)),
            _resident((FFN_SUB, d), lambda b, i, j: (d_ff // FFN_SUB, 0)),
            pl.BlockSpec((1, d), lambda b, i, j: (0, 0)),
        ],
        out_specs=pl.BlockSpec((1, tm, d), lambda b, i, j: (b, i, 0)),
        out_shape=jax.ShapeDtypeStruct((bsz, seq, d), _F32),
        scratch_shapes=[
            pltpu.VMEM((tm + HALO, d), _BF16),
            pltpu.VMEM((tf // LANES, tm + HALO, LANES), _F32),
            pltpu.VMEM((tf // LANES, tm + HALO, LANES), _F32),
            pltpu.VMEM((tm, tf), _BF16),
            pltpu.VMEM((tm, FFN_SUB), _BF16),
        ],
        compiler_params=pltpu.CompilerParams(
            dimension_semantics=("arbitrary", "arbitrary", "arbitrary"),
            vmem_limit_bytes=VMEM_LIMIT),
        name="conv_ffn",
    )(x, x, ada, g_pre.reshape(1, d), wup_bf16, wup_bf16, conv_w, conv_w, cb, cb,
      wd, wd, g_post.reshape(1, d))


def _largest_tile(n, cap, quantum):
    best = quantum
    for t in range(quantum, min(n, cap) + 1, quantum):
        if n % t == 0:
            best = t
    return best


def kernel(x, c, w_ada, b_ada, g_pre_mix, g_post_mix, w_in, attn_sinks, lam_re, lam_im, log_step,
           ssm_b_re, ssm_b_im, ssm_c_re, ssm_c_im, ssm_d, w_glu, g_attn_out, g_ssm_out, w_out,
           g_pre_ffn, g_post_ffn, w_up, conv_w, conv_b, w_down):
    bsz, seq, d = x.shape
    depth = w_in.shape[0]
    ssm_w = w_glu.shape[1]
    attn_w = w_out.shape[1] - ssm_w
    kv_w = w_in.shape[2] - attn_w - ssm_w
    d_ff = w_down.shape[1]
    assert bsz == SUBLANES, "the scan keeps one batch row per f32 sublane"

    tm = _largest_tile(seq, 512, WINDOW)
    tc = _largest_tile(seq, 128, SUBLANES)
    slab = 2 * LANES
    qb = _largest_tile(seq, 4 * WINDOW, WINDOW) // WINDOW

    ada_all = _adaln(c, w_ada, b_ada)
    for l in range(depth):
        ada = ada_all[l][:, None, :]
        q, kv, u_tm = _in_proj(x, ada, g_pre_mix[l], w_in[l].astype(_BF16), attn_w, kv_w, tm)
        attn_n = _attention(q, kv, attn_sinks[l], g_attn_out[l], qb)
        wb, cw, a_re, a_im = _ssm_params(lam_re[l], lam_im[l], log_step[l], ssm_b_re[l], ssm_b_im[l],
                                         ssm_c_re[l], ssm_c_im[l], slab)
        z_tm = _ssm(u_tm.reshape(seq * bsz, ssm_w), bsz, wb, cw, a_re, a_im,
                    ssm_d[l].reshape(ssm_w), tc)
        x = _post_mix(x, attn_n, z_tm, ada, w_glu[l].astype(_BF16), w_out[l].astype(_BF16),
                      g_ssm_out[l], g_post_mix[l], tm)
        x = _ffn(x, ada, g_pre_ffn[l], w_up[l].astype(_BF16), conv_w[l], conv_b[l],
                 w_down[l].astype(_BF16), g_post_ffn[l], tm)
    return x
```

```python
import functools
import math

import jax
import jax.numpy as jnp
from jax import lax
from jax.experimental import pallas as pl
from jax.experimental.pallas import tpu as pltpu

HEAD_DIM = 64
KV_RATIO = 8
WINDOW = 128
SSM_GROUP = 16
STATE = 64
CONV_WIDTH = 3
EPS = 1e-6
NEG = -1e30

LANES = 128
SUBLANES = 8
BF16_ROWS = 16
NORM_ROWS = BF16_ROWS
VMEM_LIMIT = 56 * 1024 * 1024

_BF16 = jnp.bfloat16
_F32 = jnp.float32


def _gelu_tanh(x):
    return x * (0.5 * (1.0 + jnp.tanh(math.sqrt(2.0 / math.pi) * (x + 0.044715 * (x * x * x)))))


def _sigmoid(x):
    return 1.0 / (1.0 + jnp.exp(-x))


def _rms_scale(x):
    return x * lax.rsqrt(jnp.mean(x * x, axis=-1, keepdims=True) + EPS)


def _resident(shape, index_map):
    return pl.BlockSpec(shape, index_map, pipeline_mode=pl.Buffered(1))


def _adaln_kernel(c_ref, w_ref, b_ref, o_ref):
    c = c_ref[...]
    ca = c * _sigmoid(c)
    o_ref[0] = jnp.dot(ca, w_ref[0], preferred_element_type=_F32,
                       precision=lax.Precision.HIGHEST) + b_ref[0]


def _adaln(c, w_ada, b_ada, tn=1024):
    depth, d, n = w_ada.shape
    bsz = c.shape[0]
    return pl.pallas_call(
        _adaln_kernel,
        grid=(depth, n // tn),
        in_specs=[
            pl.BlockSpec((bsz, d), lambda l, j: (0, 0)),
            pl.BlockSpec((1, d, tn), lambda l, j: (l, 0, j)),
            pl.BlockSpec((1, 1, tn), lambda l, j: (l, 0, j)),
        ],
        out_specs=pl.BlockSpec((1, bsz, tn), lambda l, j: (l, 0, j)),
        out_shape=jax.ShapeDtypeStruct((depth, bsz, n), _F32),
        compiler_params=pltpu.CompilerParams(
            dimension_semantics=("arbitrary", "arbitrary"), vmem_limit_bytes=VMEM_LIMIT),
        name="adaln",
    )(c, w_ada, b_ada.reshape(depth, 1, n))


def _modulated_norm(xv, g, scale, shift):
    return _rms_scale(xv) * (g * (1.0 + scale)) + shift


def _in_proj_kernel(x_ref, ada_ref, g_ref, w_ref, q_ref, kv_ref, u_ref, *, d, attn_w, kv_w):
    ada = ada_ref[0]
    h = _modulated_norm(x_ref[0], g_ref[...], ada[:, d:2 * d], ada[:, 0:d])
    p = jnp.dot(h.astype(_BF16), w_ref[...], preferred_element_type=_F32)
    q_ref[0] = (p[:, :attn_w] * (HEAD_DIM ** -0.5)).astype(_BF16)
    kv_ref[0] = p[:, attn_w:attn_w + kv_w].astype(_BF16)
    u_ref[0] = p[:, attn_w + kv_w:].astype(_BF16)


def _in_proj(x, ada, g, w_bf16, attn_w, kv_w, tm):
    bsz, seq, d = x.shape
    ncols = w_bf16.shape[1]
    ssm_w = ncols - attn_w - kv_w
    kern = functools.partial(_in_proj_kernel, d=d, attn_w=attn_w, kv_w=kv_w)
    return pl.pallas_call(
        kern,
        grid=(bsz, seq // tm),
        in_specs=[
            pl.BlockSpec((1, tm, d), lambda b, i: (b, i, 0)),
            pl.BlockSpec((1, 1, 6 * d), lambda b, i: (b, 0, 0)),
            pl.BlockSpec((1, d), lambda b, i: (0, 0)),
            _resident((d, ncols), lambda b, i: (0, 0)),
        ],
        out_specs=[
            pl.BlockSpec((1, tm, attn_w), lambda b, i: (b, i, 0)),
            pl.BlockSpec((1, tm, kv_w), lambda b, i: (b, i, 0)),
            pl.BlockSpec((1, tm, ssm_w), lambda b, i: (b, i, 0)),
        ],
        out_shape=[
            jax.ShapeDtypeStruct((bsz, seq, attn_w), _BF16),
            jax.ShapeDtypeStruct((bsz, seq, kv_w), _BF16),
            jax.ShapeDtypeStruct((bsz, seq, ssm_w), _BF16),
        ],
        compiler_params=pltpu.CompilerParams(
            dimension_semantics=("arbitrary", "arbitrary"), vmem_limit_bytes=VMEM_LIMIT),
        name="in_proj",
    )(x, ada, g.reshape(1, d), w_bf16)


def _attention_kernel(sink_ref, q_ref, kvc_ref, kvp_ref, g_ref, o_ref, *, n_kv, qb):
    step = pl.program_id(1)
    w = WINDOW
    rows = (qb + 1) * w
    kv = jnp.concatenate([kvp_ref[0], kvc_ref[0]], axis=0).astype(_F32)
    kw = n_kv * HEAD_DIM
    k2 = kv[:, :kw]
    v2 = kv[:, kw:]
    k2r = pltpu.roll(k2, HEAD_DIM, 1)
    v2r = pltpu.roll(v2, HEAD_DIM, 1)
    lo = lax.broadcasted_iota(jnp.int32, (rows, LANES), 1) < HEAD_DIM
    one_lo = jnp.where(lo, 1.0, 0.0).astype(_BF16)
    one_hi = jnp.where(lo, 0.0, 1.0).astype(_BF16)
    kk = (jnp.where(lo, k2, k2r).astype(_BF16), jnp.where(lo, k2r, k2).astype(_BF16))
    top = (jnp.concatenate([jnp.where(lo, v2, 0.0).astype(_BF16), one_lo], axis=1),
           jnp.concatenate([jnp.where(lo, v2r, 0.0).astype(_BF16), one_lo], axis=1))
    bot = (jnp.concatenate([jnp.where(lo, 0.0, v2r).astype(_BF16), one_hi], axis=1),
           jnp.concatenate([jnp.where(lo, 0.0, v2).astype(_BF16), one_hi], axis=1))

    qi = lax.broadcasted_iota(jnp.int32, (w, 2 * w), 0)
    kj = lax.broadcasted_iota(jnp.int32, (w, 2 * w), 1)
    in_band = (kj > qi) & (kj <= qi + w)
    qlane_lo = lax.broadcasted_iota(jnp.int32, (w, LANES), 1) < HEAD_DIM
    pairs_per_kv = KV_RATIO // 2
    n_pairs = n_kv * pairs_per_kv

    def scores(blk):
        out = []
        for pair in range(n_pairs):
            j = pair // pairs_per_kv
            q2 = q_ref[0, blk * w:(blk + 1) * w, pair * LANES:(pair + 1) * LANES]
            zero = jnp.zeros_like(q2)
            kkj = kk[j][blk * w:(blk + 2) * w]
            dn = (((1,), (1,)), ((), ()))
            out.append((lax.dot_general(jnp.where(qlane_lo, q2, zero), kkj, dn, preferred_element_type=_F32),
                        lax.dot_general(jnp.where(qlane_lo, zero, q2), kkj, dn, preferred_element_type=_F32)))
        return out

    def finish(blk, s_blk):
        valid = in_band if blk > 0 else in_band & ((kj >= w) | (step > 0))
        es, sk = [], []
        for pair in range(n_pairs):
            e2, k2_ = [], []
            for half in range(2):
                sink = sink_ref[2 * pair + half]
                s = jnp.where(valid, s_blk[pair][half], NEG)
                m = jnp.maximum(jnp.max(s, axis=-1, keepdims=True), sink)
                e2.append(jnp.exp(s - m).astype(_BF16))
                k2_.append(jnp.exp(sink - m))
            es.append(jnp.concatenate(e2, axis=1))
            sk.append(jnp.where(qlane_lo, k2_[0], k2_[1]))
        outs = []
        for pair in range(n_pairs):
            j = pair // pairs_per_kv
            r = jnp.concatenate([top[j][blk * w:(blk + 2) * w], bot[j][blk * w:(blk + 2) * w]], axis=0)
            ox = jnp.dot(es[pair], r, preferred_element_type=_F32)
            outs.append(ox[:, :LANES] / (ox[:, LANES:] + sk[pair]))
        o = jnp.concatenate(outs, axis=1)
        o_ref[0, blk * w:(blk + 1) * w, :] = (_rms_scale(o) * g_ref[...]).astype(_BF16)

    s_next = scores(0)
    for blk in range(qb):
        s_cur = s_next
        if blk + 1 < qb:
            s_next = scores(blk + 1)
        finish(blk, s_cur)


def _attention(q, kv, sinks, g, qb):
    bsz, seq, attn_w = q.shape
    kv_w = kv.shape[-1]
    n_kv = kv_w // (2 * HEAD_DIM)
    assert n_kv * HEAD_DIM == LANES and attn_w == n_kv * KV_RATIO * HEAD_DIM
    tq = qb * WINDOW
    kern = functools.partial(_attention_kernel, n_kv=n_kv, qb=qb)
    return pl.pallas_call(
        kern,
        grid=(bsz, seq // tq),
        in_specs=[
            pl.BlockSpec(memory_space=pltpu.SMEM),
            pl.BlockSpec((1, tq, attn_w), lambda b, n: (b, n, 0)),
            pl.BlockSpec((1, tq, kv_w), lambda b, n: (b, n, 0)),
            pl.BlockSpec((1, WINDOW, kv_w), lambda b, n: (b, jnp.maximum(n * qb - 1, 0), 0)),
            pl.BlockSpec((1, attn_w), lambda b, n: (0, 0)),
        ],
        out_specs=pl.BlockSpec((1, tq, attn_w), lambda b, n: (b, n, 0)),
        out_shape=jax.ShapeDtypeStruct((bsz, seq, attn_w), _BF16),
        compiler_params=pltpu.CompilerParams(
            dimension_semantics=("arbitrary", "arbitrary"), vmem_limit_bytes=VMEM_LIMIT),
        name="swa",
    )(sinks, q, kv, kv, g.reshape(1, attn_w))


def _ssm_kernel(u_ref, wb_ref, cw_ref, ar_ref, ai_ref, d_ref, z_ref, xs_ref, st_ref, il_ref,
                *, bsz, tc, ns):
    @pl.when(pl.program_id(1) == 0)
    def _():
        st_ref[...] = jnp.zeros_like(st_ref)

    tiles = il_ref.shape[0]
    for b in range(bsz):
        ub = u_ref[b].astype(_F32)
        for c in range(tiles):
            il_ref[c, pl.ds(b, tc, stride=bsz), :] = ub[:, c * LANES:(c + 1) * LANES]
    u32 = jnp.concatenate([il_ref[c] for c in range(tiles)], axis=1)
    uv = u32.astype(_BF16)
    hr = tc * bsz // 2
    xs_ref[0:hr, :] = jnp.dot(uv[0:hr], wb_ref[0], preferred_element_type=_F32)
    xs_ref[hr:, :] = jnp.dot(uv[hr:], wb_ref[0], preferred_element_type=_F32)
    ar = jnp.broadcast_to(ar_ref[0], (bsz, ns))
    ai = jnp.broadcast_to(ai_ref[0], (bsz, ns))

    def step(t, carry):
        xr, xi = carry
        r = pl.multiple_of(t * bsz, bsz)
        nxr = ar * xr - ai * xi + xs_ref[pl.ds(r, bsz), 0:ns]
        nxi = ar * xi + ai * xr + xs_ref[pl.ds(r, bsz), ns:2 * ns]
        xs_ref[pl.ds(r, bsz), 0:ns] = nxr
        xs_ref[pl.ds(r, bsz), ns:2 * ns] = nxi
        return nxr, nxi

    xr, xi = lax.fori_loop(0, tc, step, (st_ref[:, 0:ns], st_ref[:, ns:2 * ns]), unroll=4)
    st_ref[:, 0:ns] = xr
    st_ref[:, ns:2 * ns] = xi
    y = (jnp.dot(xs_ref[:, 0:ns].astype(_BF16), cw_ref[0, 0:ns, :], preferred_element_type=_F32)
         + jnp.dot(xs_ref[:, ns:2 * ns].astype(_BF16), cw_ref[0, ns:2 * ns, :], preferred_element_type=_F32)
         + d_ref[...] * u32)
    z = _gelu_tanh(y)
    for c in range(tiles):
        il_ref[c] = z[:, c * LANES:(c + 1) * LANES]
    for b in range(bsz):
        zb = jnp.concatenate([il_ref[c, pl.ds(b, tc, stride=bsz), :] for c in range(tiles)], axis=1)
        z_ref[b] = zb.astype(_BF16)


def _ssm_params(lam_re, lam_im, log_step, b_re, b_im, c_re, c_im, slab):
    g, p = lam_re.shape
    h = SSM_GROUP
    gs = slab // h
    n_slabs = g // gs
    dt = jnp.exp(log_step)[:, None]
    mag = jnp.exp(lam_re * dt)
    ang = lam_im * dt
    ab_re = mag * jnp.cos(ang)
    ab_im = mag * jnp.sin(ang)
    den = lam_re * lam_re + lam_im * lam_im
    f_re = ((ab_re - 1.0) * lam_re + ab_im * lam_im) / den
    f_im = (ab_im * lam_re - (ab_re - 1.0) * lam_im) / den
    bb_re = f_re[..., None] * b_re - f_im[..., None] * b_im
    bb_im = f_re[..., None] * b_im + f_im[..., None] * b_re
    eye = jnp.eye(gs, dtype=_F32)

    def block_diag_in(bb):
        t = bb.reshape(n_slabs, gs, p, h)
        return jnp.einsum('sgph,gk->sghkp', t, eye).reshape(n_slabs, gs * h, gs * p)

    def block_diag_out(cc):
        t = cc.reshape(n_slabs, gs, h, p)
        return jnp.einsum('sghp,gk->sgpkh', t, eye).reshape(n_slabs, gs * p, gs * h)

    wb = jnp.concatenate([block_diag_in(bb_re), block_diag_in(bb_im)], axis=2).astype(_BF16)
    cw = jnp.concatenate([block_diag_out(c_re), block_diag_out(-c_im)], axis=1).astype(_BF16)
    a_re = ab_re.reshape(n_slabs, 1, gs * p)
    a_im = ab_im.reshape(n_slabs, 1, gs * p)
    return wb, cw, a_re, a_im


def _ssm(u, wb, cw, a_re, a_im, d_skip, tc):
    bsz, seq, width = u.shape
    n_slabs, slab, ns2 = wb.shape
    ns = ns2 // 2
    kern = functools.partial(_ssm_kernel, bsz=bsz, tc=tc, ns=ns)
    return pl.pallas_call(
        kern,
        grid=(n_slabs, seq // tc),
        in_specs=[
            pl.BlockSpec((bsz, tc, slab), lambda s, t: (0, t, s)),
            pl.BlockSpec((1, slab, ns2), lambda s, t: (s, 0, 0)),
            pl.BlockSpec((1, ns2, slab), lambda s, t: (s, 0, 0)),
            pl.BlockSpec((1, 1, ns), lambda s, t: (s, 0, 0)),
            pl.BlockSpec((1, 1, ns), lambda s, t: (s, 0, 0)),
            pl.BlockSpec((1, slab), lambda s, t: (0, s)),
        ],
        out_specs=pl.BlockSpec((bsz, tc, slab), lambda s, t: (0, t, s)),
        out_shape=jax.ShapeDtypeStruct((bsz, seq, width), _BF16),
        scratch_shapes=[
            pltpu.VMEM((tc * bsz, ns2), _F32),
            pltpu.VMEM((bsz, ns2), _F32),
            pltpu.VMEM((slab // LANES, tc * bsz, LANES), _F32),
        ],
        compiler_params=pltpu.CompilerParams(
            dimension_semantics=("arbitrary", "arbitrary"), vmem_limit_bytes=VMEM_LIMIT),
        name="s5",
    )(u, wb, cw, a_re, a_im, d_skip.reshape(1, width))


def _post_mix_kernel(x_ref, attn_ref, z_ref, ada_ref, wglu_ref, wout_ref, gssm_ref, gpost_ref,
                     o_ref, *, d, attn_w):
    z = z_ref[0]
    zf = z.astype(_F32)
    ssm = zf * _sigmoid(jnp.dot(z, wglu_ref[...], preferred_element_type=_F32))
    ssm_n = (_rms_scale(ssm) * gssm_ref[...]).astype(_BF16)
    o_ref[0] = (jnp.dot(attn_ref[0], wout_ref[0:attn_w, :], preferred_element_type=_F32)
                + jnp.dot(ssm_n, wout_ref[attn_w:, :], preferred_element_type=_F32))
    gg = (1.0 + ada_ref[0][:, 2 * d:3 * d]) * gpost_ref[...]
    for r in range(0, o_ref.shape[1], NORM_ROWS):
        rows = slice(r, r + NORM_ROWS)
        o_ref[0, rows, :] = x_ref[0, rows, :] + _rms_scale(o_ref[0, rows, :]) * gg


def _post_mix(x, attn_n, z, ada, wglu_bf16, wout_bf16, g_ssm, g_post, tm):
    bsz, seq, d = x.shape
    attn_w = attn_n.shape[-1]
    ssm_w = wglu_bf16.shape[0]
    kern = functools.partial(_post_mix_kernel, d=d, attn_w=attn_w)
    return pl.pallas_call(
        kern,
        grid=(bsz, seq // tm),
        in_specs=[
            pl.BlockSpec((1, tm, d), lambda b, i: (b, i, 0)),
            pl.BlockSpec((1, tm, attn_w), lambda b, i: (b, i, 0)),
            pl.BlockSpec((1, tm, ssm_w), lambda b, i: (b, i, 0)),
            pl.BlockSpec((1, 1, 6 * d), lambda b, i: (b, 0, 0)),
            _resident((ssm_w, ssm_w), lambda b, i: (0, 0)),
            _resident((attn_w + ssm_w, d), lambda b, i: (0, 0)),
            pl.BlockSpec((1, ssm_w), lambda b, i: (0, 0)),
            pl.BlockSpec((1, d), lambda b, i: (0, 0)),
        ],
        out_specs=pl.BlockSpec((1, tm, d), lambda b, i: (b, i, 0)),
        out_shape=jax.ShapeDtypeStruct((bsz, seq, d), _F32),
        compiler_params=pltpu.CompilerParams(
            dimension_semantics=("arbitrary", "arbitrary"), vmem_limit_bytes=VMEM_LIMIT),
        name="post_mix",
    )(x, attn_n, z, ada, wglu_bf16, wout_bf16,
      g_ssm.reshape(1, ssm_w), g_post.reshape(1, d))


HALO = BF16_ROWS
FFN_SUB = 2 * LANES
FFN_ROWS = 64


def _ffn_kernel(x_ref, xh_ref, ada_ref, gpre_ref, wv_ref, wg_ref, cwv_ref, cwg_ref, cbv_ref, cbg_ref,
                wd_ref, wdt_ref, gpost_ref, o_ref, h_ref, upv_ref, upg_ref, act_ref, carry_ref, *, d, tm):
    i = pl.program_id(1)
    j = pl.program_id(2)
    ada = ada_ref[0]
    sub = FFN_SUB

    @pl.when(j == 0)
    def _():
        g = gpre_ref[...]
        scale = ada[:, 4 * d:5 * d]
        shift = ada[:, 3 * d:4 * d]
        hh = _modulated_norm(xh_ref[0], g, scale, shift)
        h_ref[0:HALO, :] = jnp.where(i > 0, hh, 0.0).astype(_BF16)
        gs = g * (1.0 + scale)
        for r in range(0, tm, NORM_ROWS):
            xb = x_ref[0, r:r + NORM_ROWS, :]
            h_ref[HALO + r:HALO + r + NORM_ROWS, :] = (_rms_scale(xb) * gs + shift).astype(_BF16)
        o_ref[...] = jnp.zeros_like(o_ref)
        carry_ref[...] = jnp.zeros_like(carry_ref)

    hv = h_ref[...]
    tiles = sub // LANES
    for s in range(2):
        cs = slice(s * sub, (s + 1) * sub)
        rv = jnp.dot(hv, wv_ref[:, cs], preferred_element_type=_F32)
        rg = jnp.dot(hv, wg_ref[:, cs], preferred_element_type=_F32)
        for c in range(tiles):
            upv_ref[s * tiles + c] = rv[:, c * LANES:(c + 1) * LANES]
            upg_ref[s * tiles + c] = rg[:, c * LANES:(c + 1) * LANES]

    def conv(up_ref, cw_ref, cb_ref, t, r):
        cs = slice(t * LANES, (t + 1) * LANES)
        out = cb_ref[:, cs]
        for k in range(CONV_WIDTH):
            off = HALO - (CONV_WIDTH - 1) + k + r
            out = out + up_ref[t, pl.ds(off, FFN_ROWS, stride=1), :] * cw_ref[k:k + 1, cs]
        return out

    def activation(dst, s, col0):
        for c in range(tiles):
            t = s * tiles + c
            for r in range(0, tm, FFN_ROWS):
                a = (_gelu_tanh(conv(upg_ref, cwg_ref, cbg_ref, t, r))
                     * conv(upv_ref, cwv_ref, cbv_ref, t, r))
                dst[r:r + FFN_ROWS, col0 + c * LANES:col0 + (c + 1) * LANES] = a.astype(_BF16)

    act_ref[:, 0:sub] = carry_ref[...]
    activation(act_ref, 0, sub)
    o_ref[0] += jnp.dot(act_ref[...], wd_ref[...], preferred_element_type=_F32)
    activation(carry_ref, 1, 0)

    @pl.when(j == pl.num_programs(2) - 1)
    def _():
        o_ref[0] += jnp.dot(carry_ref[...], wdt_ref[...], preferred_element_type=_F32)
        gg = (1.0 + ada[:, 5 * d:6 * d]) * gpost_ref[...]
        for r in range(0, tm, NORM_ROWS):
            rows = slice(r, r + NORM_ROWS)
            o_ref[0, rows, :] = x_ref[0, rows, :] + _rms_scale(o_ref[0, rows, :]) * gg


def _ffn(x, ada, g_pre, wup_bf16, conv_w, conv_b, wdown_bf16, g_post, tm):
    bsz, seq, d = x.shape
    d_ff = wdown_bf16.shape[0]
    tf = 2 * FFN_SUB
    nj = d_ff // tf
    assert nj * tf == d_ff
    halo_blocks = tm // HALO
    kern = functools.partial(_ffn_kernel, d=d, tm=tm)
    cb = conv_b.reshape(1, 2 * d_ff)
    wd = jnp.concatenate([jnp.zeros((FFN_SUB, d), _BF16), wdown_bf16], axis=0)
    return pl.pallas_call(
        kern,
        grid=(bsz, seq // tm, nj),
        in_specs=[
            pl.BlockSpec((1, tm, d), lambda b, i, j: (b, i, 0)),
            pl.BlockSpec((1, HALO, d), lambda b, i, j: (b, jnp.maximum(i * halo_blocks - 1, 0), 0)),
            pl.BlockSpec((1, 1, 6 * d), lambda b, i, j: (b, 0, 0)),
            pl.BlockSpec((1, d), lambda b, i, j: (0, 0)),
            pl.BlockSpec((d, tf), lambda b, i, j: (0, j)),
            pl.BlockSpec((d, tf), lambda b, i, j: (0, nj + j)),
            pl.BlockSpec((CONV_WIDTH, tf), lambda b, i, j: (0, j)),
            pl.BlockSpec((CONV_WIDTH, tf), lambda b, i, j: (0, nj + j)),
            pl.BlockSpec((1, tf), lambda b, i, j: (0, j)),
            pl.BlockSpec((1, tf), lambda b, i, j: (0, nj + j)),
            pl.BlockSpec((tf, d), lambda b, i, j: (j, 0)),
            _resident((FFN_SUB, d), lambda b, i, j: (d_ff // FFN_SUB, 0)),
            pl.BlockSpec((1, d), lambda b, i, j: (0, 0)),
        ],
        out_specs=pl.BlockSpec((1, tm, d), lambda b, i, j: (b, i, 0)),
        out_shape=jax.ShapeDtypeStruct((bsz, seq, d), _F32),
        scratch_shapes=[
            pltpu.VMEM((tm + HALO, d), _BF16),
            pltpu.VMEM((tf // LANES, tm + HALO, LANES), _F32),
            pltpu.VMEM((tf // LANES, tm + HALO, LANES), _F32),
            pltpu.VMEM((tm, tf), _BF16),
            pltpu.VMEM((tm, FFN_SUB), _BF16),
        ],
        compiler_params=pltpu.CompilerParams(
            dimension_semantics=("arbitrary", "arbitrary", "arbitrary"),
            vmem_limit_bytes=VMEM_LIMIT),
        name="conv_ffn",
    )(x, x, ada, g_pre.reshape(1, d), wup_bf16, wup_bf16, conv_w, conv_w, cb, cb,
      wd, wd, g_post.reshape(1, d))


def _largest_tile(n, cap, quantum):
    best = quantum
    for t in range(quantum, min(n, cap) + 1, quantum):
        if n % t == 0:
            best = t
    return best


def kernel(x, c, w_ada, b_ada, g_pre_mix, g_post_mix, w_in, attn_sinks, lam_re, lam_im, log_step,
           ssm_b_re, ssm_b_im, ssm_c_re, ssm_c_im, ssm_d, w_glu, g_attn_out, g_ssm_out, w_out,
           g_pre_ffn, g_post_ffn, w_up, conv_w, conv_b, w_down):
    bsz, seq, d = x.shape
    depth = w_in.shape[0]
    ssm_w = w_glu.shape[1]
    attn_w = w_out.shape[1] - ssm_w
    kv_w = w_in.shape[2] - attn_w - ssm_w
    d_ff = w_down.shape[1]
    assert bsz == SUBLANES, "the scan keeps one batch row per f32 sublane"

    tm = _largest_tile(seq, 512, WINDOW)
    tc = _largest_tile(seq, 128, SUBLANES)
    slab = 2 * LANES
    qb = _largest_tile(seq, 4 * WINDOW, WINDOW) // WINDOW

    ada_all = _adaln(c, w_ada, b_ada)
    for l in range(depth):
        ada = ada_all[l][:, None, :]
        q, kv, u = _in_proj(x, ada, g_pre_mix[l], w_in[l].astype(_BF16), attn_w, kv_w, tm)
        attn_n = _attention(q, kv, attn_sinks[l], g_attn_out[l], qb)
        wb, cw, a_re, a_im = _ssm_params(lam_re[l], lam_im[l], log_step[l], ssm_b_re[l], ssm_b_im[l],
                                         ssm_c_re[l], ssm_c_im[l], slab)
        z = _ssm(u, wb, cw, a_re, a_im, ssm_d[l].reshape(ssm_w), tc)
        x = _post_mix(x, attn_n, z, ada, w_glu[l].astype(_BF16), w_out[l].astype(_BF16),
                      g_ssm_out[l], g_post_mix[l], tm)
        x = _ffn(x, ada, g_pre_ffn[l], w_up[l].astype(_BF16), conv_w[l], conv_b[l],
                 w_down[l].astype(_BF16), g_post_ffn[l], tm)
    return x
```

```python
import functools
import math

import jax
import jax.numpy as jnp
from jax import lax
from jax.experimental import pallas as pl
from jax.experimental.pallas import tpu as pltpu

HEAD_DIM = 64
KV_RATIO = 8
WINDOW = 128
SSM_GROUP = 16
S5_SUB = 64
STATE = 64
CONV_WIDTH = 3
EPS = 1e-6
NEG = -1e30

LANES = 128
SUBLANES = 8
BF16_ROWS = 16
NORM_ROWS = BF16_ROWS
VMEM_LIMIT = 56 * 1024 * 1024

_BF16 = jnp.bfloat16
_F32 = jnp.float32


def _gelu_tanh(x):
    c = math.sqrt(2.0 / math.pi)
    hx = 0.5 * x
    return hx + hx * jnp.tanh(x * (c + (0.044715 * c) * (x * x)))


def _sigmoid(x):
    return 1.0 / (1.0 + jnp.exp(-x))


def _rms_scale(x):
    return x * lax.rsqrt(jnp.mean(x * x, axis=-1, keepdims=True) + EPS)


def _resident(shape, index_map):
    return pl.BlockSpec(shape, index_map, pipeline_mode=pl.Buffered(1))


def _adaln_kernel(c_ref, w_ref, b_ref, o_ref):
    c = c_ref[...]
    ca = c * _sigmoid(c)
    o_ref[0] = jnp.dot(ca, w_ref[0], preferred_element_type=_F32,
                       precision=lax.Precision.HIGHEST) + b_ref[0]


def _adaln(c, w_ada, b_ada, tn=1024):
    depth, d, n = w_ada.shape
    bsz = c.shape[0]
    return pl.pallas_call(
        _adaln_kernel,
        grid=(depth, n // tn),
        in_specs=[
            pl.BlockSpec((bsz, d), lambda l, j: (0, 0)),
            pl.BlockSpec((1, d, tn), lambda l, j: (l, 0, j)),
            pl.BlockSpec((1, 1, tn), lambda l, j: (l, 0, j)),
        ],
        out_specs=pl.BlockSpec((1, bsz, tn), lambda l, j: (l, 0, j)),
        out_shape=jax.ShapeDtypeStruct((depth, bsz, n), _F32),
        compiler_params=pltpu.CompilerParams(
            dimension_semantics=("arbitrary", "arbitrary"), vmem_limit_bytes=VMEM_LIMIT),
        name="adaln",
    )(c, w_ada, b_ada.reshape(depth, 1, n))


def _modulated_norm(xv, g, scale, shift):
    return _rms_scale(xv) * (g * (1.0 + scale)) + shift


def _in_proj_kernel(x_ref, ada_ref, g_ref, w_ref, q_ref, kv_ref, u_ref, *, d, attn_w, kv_w):
    ada = ada_ref[0]
    h = _modulated_norm(x_ref[0], g_ref[...], ada[:, d:2 * d], ada[:, 0:d])
    p = jnp.dot(h.astype(_BF16), w_ref[...], preferred_element_type=_F32)
    q_ref[0] = (p[:, :attn_w] * (HEAD_DIM ** -0.5)).astype(_BF16)
    kv_ref[0] = p[:, attn_w:attn_w + kv_w].astype(_BF16)
    u_ref[0] = p[:, attn_w + kv_w:].astype(_BF16)


def _in_proj(x, ada, g, w_bf16, attn_w, kv_w, tm):
    bsz, seq, d = x.shape
    ncols = w_bf16.shape[1]
    ssm_w = ncols - attn_w - kv_w
    kern = functools.partial(_in_proj_kernel, d=d, attn_w=attn_w, kv_w=kv_w)
    return pl.pallas_call(
        kern,
        grid=(bsz, seq // tm),
        in_specs=[
            pl.BlockSpec((1, tm, d), lambda b, i: (b, i, 0)),
            pl.BlockSpec((1, 1, 6 * d), lambda b, i: (b, 0, 0)),
            pl.BlockSpec((1, d), lambda b, i: (0, 0)),
            _resident((d, ncols), lambda b, i: (0, 0)),
        ],
        out_specs=[
            pl.BlockSpec((1, tm, attn_w), lambda b, i: (b, i, 0)),
            pl.BlockSpec((1, tm, kv_w), lambda b, i: (b, i, 0)),
            pl.BlockSpec((1, tm, ssm_w), lambda b, i: (b, i, 0)),
        ],
        out_shape=[
            jax.ShapeDtypeStruct((bsz, seq, attn_w), _BF16),
            jax.ShapeDtypeStruct((bsz, seq, kv_w), _BF16),
            jax.ShapeDtypeStruct((bsz, seq, ssm_w), _BF16),
        ],
        compiler_params=pltpu.CompilerParams(
            dimension_semantics=("arbitrary", "arbitrary"), vmem_limit_bytes=VMEM_LIMIT),
        name="in_proj",
    )(x, ada, g.reshape(1, d), w_bf16)


def _attention_kernel(sink_ref, q_ref, kvc_ref, kvp_ref, g_ref, o_ref, *, n_kv, qb):
    step = pl.program_id(1)
    w = WINDOW
    rows = (qb + 1) * w
    kv = jnp.concatenate([kvp_ref[0], kvc_ref[0]], axis=0).astype(_F32)
    kw = n_kv * HEAD_DIM
    k2 = kv[:, :kw]
    v2 = kv[:, kw:]
    k2r = pltpu.roll(k2, HEAD_DIM, 1)
    v2r = pltpu.roll(v2, HEAD_DIM, 1)
    lo = lax.broadcasted_iota(jnp.int32, (rows, LANES), 1) < HEAD_DIM
    one_lo = jnp.where(lo, 1.0, 0.0).astype(_BF16)
    one_hi = jnp.where(lo, 0.0, 1.0).astype(_BF16)
    kk = (jnp.where(lo, k2, k2r).astype(_BF16), jnp.where(lo, k2r, k2).astype(_BF16))
    top = (jnp.concatenate([jnp.where(lo, v2, 0.0).astype(_BF16), one_lo], axis=1),
           jnp.concatenate([jnp.where(lo, v2r, 0.0).astype(_BF16), one_lo], axis=1))
    bot = (jnp.concatenate([jnp.where(lo, 0.0, v2r).astype(_BF16), one_hi], axis=1),
           jnp.concatenate([jnp.where(lo, 0.0, v2).astype(_BF16), one_hi], axis=1))

    qi = lax.broadcasted_iota(jnp.int32, (w, 2 * w), 0)
    kj = lax.broadcasted_iota(jnp.int32, (w, 2 * w), 1)
    in_band = (kj > qi) & (kj <= qi + w)
    qlane_lo = lax.broadcasted_iota(jnp.int32, (w, LANES), 1) < HEAD_DIM
    pairs_per_kv = KV_RATIO // 2
    n_pairs = n_kv * pairs_per_kv

    def scores(blk):
        out = []
        for pair in range(n_pairs):
            j = pair // pairs_per_kv
            q2 = q_ref[0, blk * w:(blk + 1) * w, pair * LANES:(pair + 1) * LANES]
            zero = jnp.zeros_like(q2)
            kkj = kk[j][blk * w:(blk + 2) * w]
            dn = (((1,), (1,)), ((), ()))
            out.append((lax.dot_general(jnp.where(qlane_lo, q2, zero), kkj, dn, preferred_element_type=_F32),
                        lax.dot_general(jnp.where(qlane_lo, zero, q2), kkj, dn, preferred_element_type=_F32)))
        return out

    def finish(blk, s_blk):
        valid = in_band if blk > 0 else in_band & ((kj >= w) | (step > 0))
        es, sk = [], []
        for pair in range(n_pairs):
            e2, k2_ = [], []
            for half in range(2):
                sink = sink_ref[2 * pair + half]
                s = jnp.where(valid, s_blk[pair][half], NEG)
                m = jnp.maximum(jnp.max(s, axis=-1, keepdims=True), sink)
                e2.append(jnp.exp(s - m).astype(_BF16))
                k2_.append(jnp.exp(sink - m))
            es.append(jnp.concatenate(e2, axis=1))
            sk.append(jnp.where(qlane_lo, k2_[0], k2_[1]))
        outs = []
        for pair in range(n_pairs):
            j = pair // pairs_per_kv
            r = jnp.concatenate([top[j][blk * w:(blk + 2) * w], bot[j][blk * w:(blk + 2) * w]], axis=0)
            ox = jnp.dot(es[pair], r, preferred_element_type=_F32)
            outs.append(ox[:, :LANES] / (ox[:, LANES:] + sk[pair]))
        o = jnp.concatenate(outs, axis=1)
        o_ref[0, blk * w:(blk + 1) * w, :] = (_rms_scale(o) * g_ref[...]).astype(_BF16)

    s_next = scores(0)
    for blk in range(qb):
        s_cur = s_next
        if blk + 1 < qb:
            s_next = scores(blk + 1)
        finish(blk, s_cur)


def _attention(q, kv, sinks, g, qb):
    bsz, seq, attn_w = q.shape
    kv_w = kv.shape[-1]
    n_kv = kv_w // (2 * HEAD_DIM)
    assert n_kv * HEAD_DIM == LANES and attn_w == n_kv * KV_RATIO * HEAD_DIM
    tq = qb * WINDOW
    kern = functools.partial(_attention_kernel, n_kv=n_kv, qb=qb)
    return pl.pallas_call(
        kern,
        grid=(bsz, seq // tq),
        in_specs=[
            pl.BlockSpec(memory_space=pltpu.SMEM),
            pl.BlockSpec((1, tq, attn_w), lambda b, n: (b, n, 0)),
            pl.BlockSpec((1, tq, kv_w), lambda b, n: (b, n, 0)),
            pl.BlockSpec((1, WINDOW, kv_w), lambda b, n: (b, jnp.maximum(n * qb - 1, 0), 0)),
            pl.BlockSpec((1, attn_w), lambda b, n: (0, 0)),
        ],
        out_specs=pl.BlockSpec((1, tq, attn_w), lambda b, n: (b, n, 0)),
        out_shape=jax.ShapeDtypeStruct((bsz, seq, attn_w), _BF16),
        compiler_params=pltpu.CompilerParams(
            dimension_semantics=("arbitrary", "arbitrary"), vmem_limit_bytes=VMEM_LIMIT),
        name="swa",
    )(sinks, q, kv, kv, g.reshape(1, attn_w))


def _ssm_kernel(u_ref, wb_ref, cw_ref, ar_ref, ai_ref, d_ref, z_ref, xs_ref, st_ref, il_ref,
                *, bsz, tc, ns):
    @pl.when(pl.program_id(1) == 0)
    def _():
        st_ref[...] = jnp.zeros_like(st_ref)

    tiles = il_ref.shape[0]
    for b in range(bsz):
        ub = u_ref[b].astype(_F32)
        for c in range(tiles):
            il_ref[c, pl.ds(b, tc, stride=bsz), :] = ub[:, c * LANES:(c + 1) * LANES]
    ar = jnp.broadcast_to(ar_ref[0], (bsz, ns))
    ai = jnp.broadcast_to(ai_ref[0], (bsz, ns))
    xr = st_ref[:, 0:ns]
    xi = st_ref[:, ns:2 * ns]
    sr = S5_SUB * bsz
    n_sub = tc // S5_SUB
    u32 = []
    for k in range(n_sub):
        rows = slice(k * sr, (k + 1) * sr)
        uk = jnp.concatenate([il_ref[c, rows, :] for c in range(tiles)], axis=1)
        u32.append(uk)
        xs_ref[rows, :] = jnp.dot(uk.astype(_BF16), wb_ref[0], preferred_element_type=_F32)
    ys = []
    for k in range(n_sub):
        for t in range(k * S5_SUB, (k + 1) * S5_SUB):
            r = slice(t * bsz, (t + 1) * bsz)
            nxr = ar * xr - ai * xi + xs_ref[r, 0:ns]
            nxi = ar * xi + ai * xr + xs_ref[r, ns:2 * ns]
            xs_ref[r, 0:ns] = nxr
            xs_ref[r, ns:2 * ns] = nxi
            xr, xi = nxr, nxi
        rows = slice(k * sr, (k + 1) * sr)
        ys.append(jnp.dot(xs_ref[rows, 0:ns].astype(_BF16), cw_ref[0, 0:ns, :], preferred_element_type=_F32)
                  + jnp.dot(xs_ref[rows, ns:2 * ns].astype(_BF16), cw_ref[0, ns:2 * ns, :],
                            preferred_element_type=_F32)
                  + d_ref[...] * u32[k])
    st_ref[:, 0:ns] = xr
    st_ref[:, ns:2 * ns] = xi
    y = jnp.concatenate(ys, axis=0)
    z = _gelu_tanh(y)
    for c in range(tiles):
        il_ref[c] = z[:, c * LANES:(c + 1) * LANES]
    for b in range(bsz):
        zb = jnp.concatenate([il_ref[c, pl.ds(b, tc, stride=bsz), :] for c in range(tiles)], axis=1)
        z_ref[b] = zb.astype(_BF16)


def _ssm_params(lam_re, lam_im, log_step, b_re, b_im, c_re, c_im, slab):
    g, p = lam_re.shape
    h = SSM_GROUP
    gs = slab // h
    n_slabs = g // gs
    dt = jnp.exp(log_step)[:, None]
    mag = jnp.exp(lam_re * dt)
    ang = lam_im * dt
    ab_re = mag * jnp.cos(ang)
    ab_im = mag * jnp.sin(ang)
    den = lam_re * lam_re + lam_im * lam_im
    f_re = ((ab_re - 1.0) * lam_re + ab_im * lam_im) / den
    f_im = (ab_im * lam_re - (ab_re - 1.0) * lam_im) / den
    bb_re = f_re[..., None] * b_re - f_im[..., None] * b_im
    bb_im = f_re[..., None] * b_im + f_im[..., None] * b_re
    eye = jnp.eye(gs, dtype=_F32)

    def block_diag_in(bb):
        t = bb.reshape(n_slabs, gs, p, h)
        return jnp.einsum('sgph,gk->sghkp', t, eye).reshape(n_slabs, gs * h, gs * p)

    def block_diag_out(cc):
        t = cc.reshape(n_slabs, gs, h, p)
        return jnp.einsum('sghp,gk->sgpkh', t, eye).reshape(n_slabs, gs * p, gs * h)

    wb = jnp.concatenate([block_diag_in(bb_re), block_diag_in(bb_im)], axis=2).astype(_BF16)
    cw = jnp.concatenate([block_diag_out(c_re), block_diag_out(-c_im)], axis=1).astype(_BF16)
    a_re = ab_re.reshape(n_slabs, 1, gs * p)
    a_im = ab_im.reshape(n_slabs, 1, gs * p)
    return wb, cw, a_re, a_im


def _ssm(u, wb, cw, a_re, a_im, d_skip, tc):
    bsz, seq, width = u.shape
    n_slabs, slab, ns2 = wb.shape
    ns = ns2 // 2
    kern = functools.partial(_ssm_kernel, bsz=bsz, tc=tc, ns=ns)
    return pl.pallas_call(
        kern,
        grid=(n_slabs, seq // tc),
        in_specs=[
            pl.BlockSpec((bsz, tc, slab), lambda s, t: (0, t, s)),
            pl.BlockSpec((1, slab, ns2), lambda s, t: (s, 0, 0)),
            pl.BlockSpec((1, ns2, slab), lambda s, t: (s, 0, 0)),
            pl.BlockSpec((1, 1, ns), lambda s, t: (s, 0, 0)),
            pl.BlockSpec((1, 1, ns), lambda s, t: (s, 0, 0)),
            pl.BlockSpec((1, slab), lambda s, t: (0, s)),
        ],
        out_specs=pl.BlockSpec((bsz, tc, slab), lambda s, t: (0, t, s)),
        out_shape=jax.ShapeDtypeStruct((bsz, seq, width), _BF16),
        scratch_shapes=[
            pltpu.VMEM((tc * bsz, ns2), _F32),
            pltpu.VMEM((bsz, ns2), _F32),
            pltpu.VMEM((slab // LANES, tc * bsz, LANES), _F32),
        ],
        compiler_params=pltpu.CompilerParams(
            dimension_semantics=("arbitrary", "arbitrary"), vmem_limit_bytes=VMEM_LIMIT),
        name="s5",
    )(u, wb, cw, a_re, a_im, d_skip.reshape(1, width))


def _post_mix_kernel(x_ref, attn_ref, z_ref, ada_ref, wglu_ref, wout_ref, gssm_ref, gpost_ref,
                     o_ref, *, d, attn_w):
    z = z_ref[0]
    zf = z.astype(_F32)
    ssm = zf * _sigmoid(jnp.dot(z, wglu_ref[...], preferred_element_type=_F32))
    ssm_n = (_rms_scale(ssm) * gssm_ref[...]).astype(_BF16)
    o_ref[0] = (jnp.dot(attn_ref[0], wout_ref[0:attn_w, :], preferred_element_type=_F32)
                + jnp.dot(ssm_n, wout_ref[attn_w:, :], preferred_element_type=_F32))
    gg = (1.0 + ada_ref[0][:, 2 * d:3 * d]) * gpost_ref[...]
    for r in range(0, o_ref.shape[1], NORM_ROWS):
        rows = slice(r, r + NORM_ROWS)
        o_ref[0, rows, :] = x_ref[0, rows, :] + _rms_scale(o_ref[0, rows, :]) * gg


def _post_mix(x, attn_n, z, ada, wglu_bf16, wout_bf16, g_ssm, g_post, tm):
    bsz, seq, d = x.shape
    attn_w = attn_n.shape[-1]
    ssm_w = wglu_bf16.shape[0]
    kern = functools.partial(_post_mix_kernel, d=d, attn_w=attn_w)
    return pl.pallas_call(
        kern,
        grid=(bsz, seq // tm),
        in_specs=[
            pl.BlockSpec((1, tm, d), lambda b, i: (b, i, 0)),
            pl.BlockSpec((1, tm, attn_w), lambda b, i: (b, i, 0)),
            pl.BlockSpec((1, tm, ssm_w), lambda b, i: (b, i, 0)),
            pl.BlockSpec((1, 1, 6 * d), lambda b, i: (b, 0, 0)),
            _resident((ssm_w, ssm_w), lambda b, i: (0, 0)),
            _resident((attn_w + ssm_w, d), lambda b, i: (0, 0)),
            pl.BlockSpec((1, ssm_w), lambda b, i: (0, 0)),
            pl.BlockSpec((1, d), lambda b, i: (0, 0)),
        ],
        out_specs=pl.BlockSpec((1, tm, d), lambda b, i: (b, i, 0)),
        out_shape=jax.ShapeDtypeStruct((bsz, seq, d), _F32),
        compiler_params=pltpu.CompilerParams(
            dimension_semantics=("arbitrary", "arbitrary"), vmem_limit_bytes=VMEM_LIMIT),
        name="post_mix",
    )(x, attn_n, z, ada, wglu_bf16, wout_bf16,
      g_ssm.reshape(1, ssm_w), g_post.reshape(1, d))


HALO = BF16_ROWS
FFN_SUB = 2 * LANES
FFN_ROWS = 64


def _ffn_kernel(x_ref, xh_ref, ada_ref, gpre_ref, wv_ref, wg_ref, cwv_ref, cwg_ref, cbv_ref, cbg_ref,
                pwv_ref, pwg_ref, pbv_ref, pbg_ref, wd_ref, wdt_ref, gpost_ref,
                o_ref, h_ref, upv0_ref, upg0_ref, upv1_ref, upg1_ref, act_ref, tail_ref, *, d, tm):
    i = pl.program_id(1)
    j = pl.program_id(2)
    ada = ada_ref[0]
    sub = FFN_SUB
    tiles = sub // LANES
    upv = (upv0_ref, upv1_ref)
    upg = (upg0_ref, upg1_ref)

    @pl.when(j == 0)
    def _():
        g = gpre_ref[...]
        scale = ada[:, 4 * d:5 * d]
        shift = ada[:, 3 * d:4 * d]
        hh = _modulated_norm(xh_ref[0], g, scale, shift)
        h_ref[0:HALO, :] = jnp.where(i > 0, hh, 0.0).astype(_BF16)
        gs = g * (1.0 + scale)
        for r in range(0, tm, NORM_ROWS):
            xb = x_ref[0, r:r + NORM_ROWS, :]
            h_ref[HALO + r:HALO + r + NORM_ROWS, :] = (_rms_scale(xb) * gs + shift).astype(_BF16)
        o_ref[...] = jnp.zeros_like(o_ref)
        upv1_ref[...] = jnp.zeros_like(upv1_ref)
        upg1_ref[...] = jnp.zeros_like(upg1_ref)

    def conv(up_ref, cw_ref, cb_ref, s, c, r):
        cs = slice(s * sub + c * LANES, s * sub + (c + 1) * LANES)
        out = cb_ref[:, cs]
        for k in range(CONV_WIDTH):
            off = HALO - (CONV_WIDTH - 1) + k + r
            out = out + up_ref[c, pl.ds(off, FFN_ROWS, stride=1), :] * cw_ref[k:k + 1, cs]
        return out

    def activation(dst, s, col0, cwv, cwg, cbv, cbg):
        token = jnp.zeros((SUBLANES, LANES), jnp.uint32)
        for c in range(tiles):
            for r in range(0, tm, FFN_ROWS):
                a = _gelu_tanh(conv(upg[s], cwg, cbg, s, c, r)) * conv(upv[s], cwv, cbv, s, c, r)
                dst[r:r + FFN_ROWS, col0 + c * LANES:col0 + (c + 1) * LANES] = a.astype(_BF16)
                bits = lax.bitcast_convert_type(a, jnp.uint32)
                for q in range(0, FFN_ROWS, SUBLANES):
                    token = token | bits[q:q + SUBLANES, :]
        return lax.shift_right_logical(lax.shift_right_logical(token, jnp.uint32(16)), jnp.uint32(16))

    def up_proj(s, hv):
        cs = slice(s * sub, (s + 1) * sub)
        rv = jnp.dot(hv, wv_ref[:, cs], preferred_element_type=_F32)
        rg = jnp.dot(hv, wg_ref[:, cs], preferred_element_type=_F32)
        for c in range(tiles):
            upv[s][c] = rv[:, c * LANES:(c + 1) * LANES]
            upg[s][c] = rg[:, c * LANES:(c + 1) * LANES]

    token = activation(act_ref, 1, 0, pwv_ref, pwg_ref, pbv_ref, pbg_ref)
    up_proj(0, h_ref[...])
    zero = lax.bitcast_convert_type(token, _F32)
    anchor = h_ref[0:BF16_ROWS, 0:LANES].astype(_F32) + jnp.concatenate([zero, zero], axis=0)
    h_ref[0:BF16_ROWS, 0:LANES] = anchor.astype(_BF16)
    up_proj(1, h_ref[...])
    activation(act_ref, 0, sub, cwv_ref, cwg_ref, cbv_ref, cbg_ref)
    o_ref[0] += jnp.dot(act_ref[...], wd_ref[...], preferred_element_type=_F32)

    @pl.when(j == pl.num_programs(2) - 1)
    def _():
        activation(tail_ref, 1, 0, cwv_ref, cwg_ref, cbv_ref, cbg_ref)
        o_ref[0] += jnp.dot(tail_ref[...], wdt_ref[...], preferred_element_type=_F32)
        gg = (1.0 + ada[:, 5 * d:6 * d]) * gpost_ref[...]
        for r in range(0, tm, NORM_ROWS):
            rows = slice(r, r + NORM_ROWS)
            o_ref[0, rows, :] = x_ref[0, rows, :] + _rms_scale(o_ref[0, rows, :]) * gg


def _ffn(x, ada, g_pre, wup_bf16, conv_w, conv_b, wdown_bf16, g_post, tm):
    bsz, seq, d = x.shape
    d_ff = wdown_bf16.shape[0]
    tf = 2 * FFN_SUB
    nj = d_ff // tf
    assert nj * tf == d_ff
    halo_blocks = tm // HALO
    kern = functools.partial(_ffn_kernel, d=d, tm=tm)
    cb = conv_b.reshape(1, 2 * d_ff)
    wd = jnp.concatenate([jnp.zeros((FFN_SUB, d), _BF16), wdown_bf16], axis=0)

    def prev(j):
        return jnp.maximum(j - 1, 0)

    return pl.pallas_call(
        kern,
        grid=(bsz, seq // tm, nj),
        in_specs=[
            pl.BlockSpec((1, tm, d), lambda b, i, j: (b, i, 0)),
            pl.BlockSpec((1, HALO, d), lambda b, i, j: (b, jnp.maximum(i * halo_blocks - 1, 0), 0)),
            pl.BlockSpec((1, 1, 6 * d), lambda b, i, j: (b, 0, 0)),
            pl.BlockSpec((1, d), lambda b, i, j: (0, 0)),
            pl.BlockSpec((d, tf), lambda b, i, j: (0, j)),
            pl.BlockSpec((d, tf), lambda b, i, j: (0, nj + j)),
            pl.BlockSpec((CONV_WIDTH, tf), lambda b, i, j: (0, j)),
            pl.BlockSpec((CONV_WIDTH, tf), lambda b, i, j: (0, nj + j)),
            pl.BlockSpec((1, tf), lambda b, i, j: (0, j)),
            pl.BlockSpec((1, tf), lambda b, i, j: (0, nj + j)),
            pl.BlockSpec((CONV_WIDTH, tf), lambda b, i, j: (0, prev(j))),
            pl.BlockSpec((CONV_WIDTH, tf), lambda b, i, j: (0, nj + prev(j))),
            pl.BlockSpec((1, tf), lambda b, i, j: (0, prev(j))),
            pl.BlockSpec((1, tf), lambda b, i, j: (0, nj + prev(j))),
            pl.BlockSpec((tf, d), lambda b, i, j: (j, 0)),
            _resident((FFN_SUB, d), lambda b, i, j: (d_ff // FFN_SUB, 0)),
            pl.BlockSpec((1, d), lambda b, i, j: (0, 0)),
        ],
        out_specs=pl.BlockSpec((1, tm, d), lambda b, i, j: (b, i, 0)),
        out_shape=jax.ShapeDtypeStruct((bsz, seq, d), _F32),
        scratch_shapes=[
            pltpu.VMEM((tm + HALO, d), _BF16),
            pltpu.VMEM((FFN_SUB // LANES, tm + HALO, LANES), _F32),
            pltpu.VMEM((FFN_SUB // LANES, tm + HALO, LANES), _F32),
            pltpu.VMEM((FFN_SUB // LANES, tm + HALO, LANES), _F32),
            pltpu.VMEM((FFN_SUB // LANES, tm + HALO, LANES), _F32),
            pltpu.VMEM((tm, tf), _BF16),
            pltpu.VMEM((tm, FFN_SUB), _BF16),
        ],
        compiler_params=pltpu.CompilerParams(
            dimension_semantics=("arbitrary", "arbitrary", "arbitrary"),
            vmem_limit_bytes=VMEM_LIMIT),
        name="conv_ffn",
    )(x, x, ada, g_pre.reshape(1, d), wup_bf16, wup_bf16, conv_w, conv_w, cb, cb,
      conv_w, conv_w, cb, cb, wd, wd, g_post.reshape(1, d))


def _largest_tile(n, cap, quantum):
    best = quantum
    for t in range(quantum, min(n, cap) + 1, quantum):
        if n % t == 0:
            best = t
    return best


def kernel(x, c, w_ada, b_ada, g_pre_mix, g_post_mix, w_in, attn_sinks, lam_re, lam_im, log_step,
           ssm_b_re, ssm_b_im, ssm_c_re, ssm_c_im, ssm_d, w_glu, g_attn_out, g_ssm_out, w_out,
           g_pre_ffn, g_post_ffn, w_up, conv_w, conv_b, w_down):
    bsz, seq, d = x.shape
    depth = w_in.shape[0]
    ssm_w = w_glu.shape[1]
    attn_w = w_out.shape[1] - ssm_w
    kv_w = w_in.shape[2] - attn_w - ssm_w
    d_ff = w_down.shape[1]
    assert bsz == SUBLANES, "the scan keeps one batch row per f32 sublane"

    tm = _largest_tile(seq, 512, WINDOW)
    tc = _largest_tile(seq, 128, SUBLANES)
    slab = 2 * LANES
    qb = _largest_tile(seq, 4 * WINDOW, WINDOW) // WINDOW

    ada_all = _adaln(c, w_ada, b_ada)
    for l in range(depth):
        ada = ada_all[l][:, None, :]
        q, kv, u = _in_proj(x, ada, g_pre_mix[l], w_in[l].astype(_BF16), attn_w, kv_w, tm)
        attn_n = _attention(q, kv, attn_sinks[l], g_attn_out[l], qb)
        wb, cw, a_re, a_im = _ssm_params(lam_re[l], lam_im[l], log_step[l], ssm_b_re[l], ssm_b_im[l],
                                         ssm_c_re[l], ssm_c_im[l], slab)
        z = _ssm(u, wb, cw, a_re, a_im, ssm_d[l].reshape(ssm_w), tc)
        x = _post_mix(x, attn_n, z, ada, w_glu[l].astype(_BF16), w_out[l].astype(_BF16),
                      g_ssm_out[l], g_post_mix[l], tm)
        x = _ffn(x, ada, g_pre_ffn[l], w_up[l].astype(_BF16), conv_w[l], conv_b[l],
                 w_down[l].astype(_BF16), g_post_ffn[l], tm)
    return x
```

```python
import functools
import math

import jax
import jax.numpy as jnp
from jax import lax
from jax.experimental import pallas as pl
from jax.experimental.pallas import tpu as pltpu

HEAD_DIM = 64
KV_RATIO = 8
WINDOW = 128
SSM_GROUP = 16
S5_SUB = 64
STATE = 64
CONV_WIDTH = 3
EPS = 1e-6
NEG = -1e30

LANES = 128
SUBLANES = 8
BF16_ROWS = 16
NORM_ROWS = BF16_ROWS
VMEM_LIMIT = 56 * 1024 * 1024

_BF16 = jnp.bfloat16
_F32 = jnp.float32


def _gelu_tanh(x):
    c = math.sqrt(2.0 / math.pi)
    hx = 0.5 * x
    return hx + hx * jnp.tanh(x * (c + (0.044715 * c) * (x * x)))


def _sigmoid(x):
    return 1.0 / (1.0 + jnp.exp(-x))


def _rms_scale(x):
    return x * lax.rsqrt(jnp.mean(x * x, axis=-1, keepdims=True) + EPS)


def _resident(shape, index_map):
    return pl.BlockSpec(shape, index_map, pipeline_mode=pl.Buffered(1))


def _adaln_kernel(c_ref, w_ref, b_ref, o_ref):
    c = c_ref[...]
    ca = c * _sigmoid(c)
    o_ref[0] = jnp.dot(ca, w_ref[0], preferred_element_type=_F32,
                       precision=lax.Precision.HIGHEST) + b_ref[0]


def _adaln(c, w_ada, b_ada, tn=1024):
    depth, d, n = w_ada.shape
    bsz = c.shape[0]
    return pl.pallas_call(
        _adaln_kernel,
        grid=(depth, n // tn),
        in_specs=[
            pl.BlockSpec((bsz, d), lambda l, j: (0, 0)),
            pl.BlockSpec((1, d, tn), lambda l, j: (l, 0, j)),
            pl.BlockSpec((1, 1, tn), lambda l, j: (l, 0, j)),
        ],
        out_specs=pl.BlockSpec((1, bsz, tn), lambda l, j: (l, 0, j)),
        out_shape=jax.ShapeDtypeStruct((depth, bsz, n), _F32),
        compiler_params=pltpu.CompilerParams(
            dimension_semantics=("arbitrary", "arbitrary"), vmem_limit_bytes=VMEM_LIMIT),
        name="adaln",
    )(c, w_ada, b_ada.reshape(depth, 1, n))


def _modulated_norm(xv, g, scale, shift):
    return _rms_scale(xv) * (g * (1.0 + scale)) + shift


def _in_proj_kernel(x_ref, ada_ref, g_ref, w_ref, q_ref, kv_ref, u_ref, *, d, attn_w, kv_w):
    ada = ada_ref[0]
    h = _modulated_norm(x_ref[0], g_ref[...], ada[:, d:2 * d], ada[:, 0:d])
    p = jnp.dot(h.astype(_BF16), w_ref[...], preferred_element_type=_F32)
    q_ref[0] = (p[:, :attn_w] * (HEAD_DIM ** -0.5)).astype(_BF16)
    kv_ref[0] = p[:, attn_w:attn_w + kv_w].astype(_BF16)
    u_ref[0] = p[:, attn_w + kv_w:].astype(_BF16)


def _in_proj(x, ada, g, w_bf16, attn_w, kv_w, tm):
    bsz, seq, d = x.shape
    ncols = w_bf16.shape[1]
    ssm_w = ncols - attn_w - kv_w
    kern = functools.partial(_in_proj_kernel, d=d, attn_w=attn_w, kv_w=kv_w)
    return pl.pallas_call(
        kern,
        grid=(bsz, seq // tm),
        in_specs=[
            pl.BlockSpec((1, tm, d), lambda b, i: (b, i, 0)),
            pl.BlockSpec((1, 1, 6 * d), lambda b, i: (b, 0, 0)),
            pl.BlockSpec((1, d), lambda b, i: (0, 0)),
            _resident((d, ncols), lambda b, i: (0, 0)),
        ],
        out_specs=[
            pl.BlockSpec((1, tm, attn_w), lambda b, i: (b, i, 0)),
            pl.BlockSpec((1, tm, kv_w), lambda b, i: (b, i, 0)),
            pl.BlockSpec((1, tm, ssm_w), lambda b, i: (b, i, 0)),
        ],
        out_shape=[
            jax.ShapeDtypeStruct((bsz, seq, attn_w), _BF16),
            jax.ShapeDtypeStruct((bsz, seq, kv_w), _BF16),
            jax.ShapeDtypeStruct((bsz, seq, ssm_w), _BF16),
        ],
        compiler_params=pltpu.CompilerParams(
            dimension_semantics=("arbitrary", "arbitrary"), vmem_limit_bytes=VMEM_LIMIT),
        name="in_proj",
    )(x, ada, g.reshape(1, d), w_bf16)


def _attention_kernel(sink_ref, q_ref, kvc_ref, kvp_ref, g_ref, o_ref, *, n_kv, qb):
    step = pl.program_id(1)
    w = WINDOW
    rows = (qb + 1) * w
    kv = jnp.concatenate([kvp_ref[0], kvc_ref[0]], axis=0).astype(_F32)
    kw = n_kv * HEAD_DIM
    k2 = kv[:, :kw]
    v2 = kv[:, kw:]
    k2r = pltpu.roll(k2, HEAD_DIM, 1)
    v2r = pltpu.roll(v2, HEAD_DIM, 1)
    lo = lax.broadcasted_iota(jnp.int32, (rows, LANES), 1) < HEAD_DIM
    one_lo = jnp.where(lo, 1.0, 0.0).astype(_BF16)
    one_hi = jnp.where(lo, 0.0, 1.0).astype(_BF16)
    kk = (jnp.where(lo, k2, k2r).astype(_BF16), jnp.where(lo, k2r, k2).astype(_BF16))
    top = (jnp.concatenate([jnp.where(lo, v2, 0.0).astype(_BF16), one_lo], axis=1),
           jnp.concatenate([jnp.where(lo, v2r, 0.0).astype(_BF16), one_lo], axis=1))
    bot = (jnp.concatenate([jnp.where(lo, 0.0, v2r).astype(_BF16), one_hi], axis=1),
           jnp.concatenate([jnp.where(lo, 0.0, v2).astype(_BF16), one_hi], axis=1))

    qi = lax.broadcasted_iota(jnp.int32, (w, 2 * w), 0)
    kj = lax.broadcasted_iota(jnp.int32, (w, 2 * w), 1)
    in_band = (kj > qi) & (kj <= qi + w)
    qlane_lo = lax.broadcasted_iota(jnp.int32, (w, LANES), 1) < HEAD_DIM
    pairs_per_kv = KV_RATIO // 2
    n_pairs = n_kv * pairs_per_kv

    def scores(blk):
        out = []
        for pair in range(n_pairs):
            j = pair // pairs_per_kv
            q2 = q_ref[0, blk * w:(blk + 1) * w, pair * LANES:(pair + 1) * LANES]
            zero = jnp.zeros_like(q2)
            kkj = kk[j][blk * w:(blk + 2) * w]
            dn = (((1,), (1,)), ((), ()))
            out.append((lax.dot_general(jnp.where(qlane_lo, q2, zero), kkj, dn, preferred_element_type=_F32),
                        lax.dot_general(jnp.where(qlane_lo, zero, q2), kkj, dn, preferred_element_type=_F32)))
        return out

    def finish(blk, s_blk):
        valid = in_band if blk > 0 else in_band & ((kj >= w) | (step > 0))
        es, sk = [], []
        for pair in range(n_pairs):
            e2, k2_ = [], []
            for half in range(2):
                sink = sink_ref[2 * pair + half]
                s = jnp.where(valid, s_blk[pair][half], NEG)
                m = jnp.maximum(jnp.max(s, axis=-1, keepdims=True), sink)
                e2.append(jnp.exp(s - m).astype(_BF16))
                k2_.append(jnp.exp(sink - m))
            es.append(jnp.concatenate(e2, axis=1))
            sk.append(jnp.where(qlane_lo, k2_[0], k2_[1]))
        outs = []
        for pair in range(n_pairs):
            j = pair // pairs_per_kv
            r = jnp.concatenate([top[j][blk * w:(blk + 2) * w], bot[j][blk * w:(blk + 2) * w]], axis=0)
            ox = jnp.dot(es[pair], r, preferred_element_type=_F32)
            outs.append(ox[:, :LANES] / (ox[:, LANES:] + sk[pair]))
        o = jnp.concatenate(outs, axis=1)
        o_ref[0, blk * w:(blk + 1) * w, :] = (_rms_scale(o) * g_ref[...]).astype(_BF16)

    s_next = scores(0)
    for blk in range(qb):
        s_cur = s_next
        if blk + 1 < qb:
            s_next = scores(blk + 1)
        finish(blk, s_cur)


def _attention(q, kv, sinks, g, qb):
    bsz, seq, attn_w = q.shape
    kv_w = kv.shape[-1]
    n_kv = kv_w // (2 * HEAD_DIM)
    assert n_kv * HEAD_DIM == LANES and attn_w == n_kv * KV_RATIO * HEAD_DIM
    tq = qb * WINDOW
    kern = functools.partial(_attention_kernel, n_kv=n_kv, qb=qb)
    return pl.pallas_call(
        kern,
        grid=(bsz, seq // tq),
        in_specs=[
            pl.BlockSpec(memory_space=pltpu.SMEM),
            pl.BlockSpec((1, tq, attn_w), lambda b, n: (b, n, 0)),
            pl.BlockSpec((1, tq, kv_w), lambda b, n: (b, n, 0)),
            pl.BlockSpec((1, WINDOW, kv_w), lambda b, n: (b, jnp.maximum(n * qb - 1, 0), 0)),
            pl.BlockSpec((1, attn_w), lambda b, n: (0, 0)),
        ],
        out_specs=pl.BlockSpec((1, tq, attn_w), lambda b, n: (b, n, 0)),
        out_shape=jax.ShapeDtypeStruct((bsz, seq, attn_w), _BF16),
        compiler_params=pltpu.CompilerParams(
            dimension_semantics=("arbitrary", "arbitrary"), vmem_limit_bytes=VMEM_LIMIT),
        name="swa",
    )(sinks, q, kv, kv, g.reshape(1, attn_w))


def _ssm_kernel(u_ref, wb_ref, cw_ref, ar_ref, ai_ref, d_ref, z_ref, xs_ref, st_ref, il_ref,
                *, bsz, tc, ns):
    @pl.when(pl.program_id(1) == 0)
    def _():
        st_ref[...] = jnp.zeros_like(st_ref)

    tiles = il_ref.shape[0]
    for b in range(bsz):
        ub = u_ref[b].astype(_F32)
        for c in range(tiles):
            il_ref[c, pl.ds(b, tc, stride=bsz), :] = ub[:, c * LANES:(c + 1) * LANES]
    ar = jnp.broadcast_to(ar_ref[0], (bsz, ns))
    ai = jnp.broadcast_to(ai_ref[0], (bsz, ns))
    xr = st_ref[:, 0:ns]
    xi = st_ref[:, ns:2 * ns]
    sr = S5_SUB * bsz
    n_sub = tc // S5_SUB
    u32 = []
    for k in range(n_sub):
        rows = slice(k * sr, (k + 1) * sr)
        uk = jnp.concatenate([il_ref[c, rows, :] for c in range(tiles)], axis=1)
        u32.append(uk)
        xs_ref[rows, :] = jnp.dot(uk.astype(_BF16), wb_ref[0], preferred_element_type=_F32)
    ys = []
    for k in range(n_sub):
        for t in range(k * S5_SUB, (k + 1) * S5_SUB):
            r = slice(t * bsz, (t + 1) * bsz)
            nxr = ar * xr - ai * xi + xs_ref[r, 0:ns]
            nxi = ar * xi + ai * xr + xs_ref[r, ns:2 * ns]
            xs_ref[r, 0:ns] = nxr
            xs_ref[r, ns:2 * ns] = nxi
            xr, xi = nxr, nxi
        rows = slice(k * sr, (k + 1) * sr)
        ys.append(jnp.dot(xs_ref[rows, 0:ns].astype(_BF16), cw_ref[0, 0:ns, :], preferred_element_type=_F32)
                  + jnp.dot(xs_ref[rows, ns:2 * ns].astype(_BF16), cw_ref[0, ns:2 * ns, :],
                            preferred_element_type=_F32)
                  + d_ref[...] * u32[k])
    st_ref[:, 0:ns] = xr
    st_ref[:, ns:2 * ns] = xi
    y = jnp.concatenate(ys, axis=0)
    z = _gelu_tanh(y)
    for c in range(tiles):
        il_ref[c] = z[:, c * LANES:(c + 1) * LANES]
    for b in range(bsz):
        zb = jnp.concatenate([il_ref[c, pl.ds(b, tc, stride=bsz), :] for c in range(tiles)], axis=1)
        z_ref[b] = zb.astype(_BF16)


def _ssm_params(lam_re, lam_im, log_step, b_re, b_im, c_re, c_im, slab):
    g, p = lam_re.shape
    h = SSM_GROUP
    gs = slab // h
    n_slabs = g // gs
    dt = jnp.exp(log_step)[:, None]
    mag = jnp.exp(lam_re * dt)
    ang = lam_im * dt
    ab_re = mag * jnp.cos(ang)
    ab_im = mag * jnp.sin(ang)
    den = lam_re * lam_re + lam_im * lam_im
    f_re = ((ab_re - 1.0) * lam_re + ab_im * lam_im) / den
    f_im = (ab_im * lam_re - (ab_re - 1.0) * lam_im) / den
    bb_re = f_re[..., None] * b_re - f_im[..., None] * b_im
    bb_im = f_re[..., None] * b_im + f_im[..., None] * b_re
    eye = jnp.eye(gs, dtype=_F32)

    def block_diag_in(bb):
        t = bb.reshape(n_slabs, gs, p, h)
        return jnp.einsum('sgph,gk->sghkp', t, eye).reshape(n_slabs, gs * h, gs * p)

    def block_diag_out(cc):
        t = cc.reshape(n_slabs, gs, h, p)
        return jnp.einsum('sghp,gk->sgpkh', t, eye).reshape(n_slabs, gs * p, gs * h)

    wb = jnp.concatenate([block_diag_in(bb_re), block_diag_in(bb_im)], axis=2).astype(_BF16)
    cw = jnp.concatenate([block_diag_out(c_re), block_diag_out(-c_im)], axis=1).astype(_BF16)
    a_re = ab_re.reshape(n_slabs, 1, gs * p)
    a_im = ab_im.reshape(n_slabs, 1, gs * p)
    return wb, cw, a_re, a_im


def _ssm(u, wb, cw, a_re, a_im, d_skip, tc):
    bsz, seq, width = u.shape
    n_slabs, slab, ns2 = wb.shape
    ns = ns2 // 2
    kern = functools.partial(_ssm_kernel, bsz=bsz, tc=tc, ns=ns)
    return pl.pallas_call(
        kern,
        grid=(n_slabs, seq // tc),
        in_specs=[
            pl.BlockSpec((bsz, tc, slab), lambda s, t: (0, t, s)),
            pl.BlockSpec((1, slab, ns2), lambda s, t: (s, 0, 0)),
            pl.BlockSpec((1, ns2, slab), lambda s, t: (s, 0, 0)),
            pl.BlockSpec((1, 1, ns), lambda s, t: (s, 0, 0)),
            pl.BlockSpec((1, 1, ns), lambda s, t: (s, 0, 0)),
            pl.BlockSpec((1, slab), lambda s, t: (0, s)),
        ],
        out_specs=pl.BlockSpec((bsz, tc, slab), lambda s, t: (0, t, s)),
        out_shape=jax.ShapeDtypeStruct((bsz, seq, width), _BF16),
        scratch_shapes=[
            pltpu.VMEM((tc * bsz, ns2), _F32),
            pltpu.VMEM((bsz, ns2), _F32),
            pltpu.VMEM((slab // LANES, tc * bsz, LANES), _F32),
        ],
        compiler_params=pltpu.CompilerParams(
            dimension_semantics=("arbitrary", "arbitrary"), vmem_limit_bytes=VMEM_LIMIT),
        name="s5",
    )(u, wb, cw, a_re, a_im, d_skip.reshape(1, width))


def _post_mix_kernel(x_ref, attn_ref, z_ref, ada_ref, wglu_ref, wout_ref, gssm_ref, gpost_ref,
                     o_ref, *, d, attn_w):
    z = z_ref[0]
    zf = z.astype(_F32)
    ssm = zf * _sigmoid(jnp.dot(z, wglu_ref[...], preferred_element_type=_F32))
    ssm_n = (_rms_scale(ssm) * gssm_ref[...]).astype(_BF16)
    o_ref[0] = (jnp.dot(attn_ref[0], wout_ref[0:attn_w, :], preferred_element_type=_F32)
                + jnp.dot(ssm_n, wout_ref[attn_w:, :], preferred_element_type=_F32))
    gg = (1.0 + ada_ref[0][:, 2 * d:3 * d]) * gpost_ref[...]
    for r in range(0, o_ref.shape[1], NORM_ROWS):
        rows = slice(r, r + NORM_ROWS)
        o_ref[0, rows, :] = x_ref[0, rows, :] + _rms_scale(o_ref[0, rows, :]) * gg


def _post_mix(x, attn_n, z, ada, wglu_bf16, wout_bf16, g_ssm, g_post, tm):
    bsz, seq, d = x.shape
    attn_w = attn_n.shape[-1]
    ssm_w = wglu_bf16.shape[0]
    kern = functools.partial(_post_mix_kernel, d=d, attn_w=attn_w)
    return pl.pallas_call(
        kern,
        grid=(bsz, seq // tm),
        in_specs=[
            pl.BlockSpec((1, tm, d), lambda b, i: (b, i, 0)),
            pl.BlockSpec((1, tm, attn_w), lambda b, i: (b, i, 0)),
            pl.BlockSpec((1, tm, ssm_w), lambda b, i: (b, i, 0)),
            pl.BlockSpec((1, 1, 6 * d), lambda b, i: (b, 0, 0)),
            _resident((ssm_w, ssm_w), lambda b, i: (0, 0)),
            _resident((attn_w + ssm_w, d), lambda b, i: (0, 0)),
            pl.BlockSpec((1, ssm_w), lambda b, i: (0, 0)),
            pl.BlockSpec((1, d), lambda b, i: (0, 0)),
        ],
        out_specs=pl.BlockSpec((1, tm, d), lambda b, i: (b, i, 0)),
        out_shape=jax.ShapeDtypeStruct((bsz, seq, d), _F32),
        compiler_params=pltpu.CompilerParams(
            dimension_semantics=("arbitrary", "arbitrary"), vmem_limit_bytes=VMEM_LIMIT),
        name="post_mix",
    )(x, attn_n, z, ada, wglu_bf16, wout_bf16,
      g_ssm.reshape(1, ssm_w), g_post.reshape(1, d))


HALO = BF16_ROWS
FFN_SUB = 2 * LANES
FFN_ROWS = 64


def _ffn_kernel(x_ref, xh_ref, ada_ref, gpre_ref, wv_ref, wg_ref, cwv_ref, cwg_ref, cbv_ref, cbg_ref,
                pwv_ref, pwg_ref, pbv_ref, pbg_ref, wd_ref, wdt_ref, gpost_ref,
                o_ref, h_ref, upv0_ref, upg0_ref, upv1_ref, upg1_ref, act_ref, tail_ref, *, d, tm):
    i = pl.program_id(1)
    j = pl.program_id(2)
    ada = ada_ref[0]
    sub = FFN_SUB
    tiles = sub // LANES
    upv = (upv0_ref, upv1_ref)
    upg = (upg0_ref, upg1_ref)

    @pl.when(j == 0)
    def _():
        g = gpre_ref[...]
        scale = ada[:, 4 * d:5 * d]
        shift = ada[:, 3 * d:4 * d]
        hh = _modulated_norm(xh_ref[0], g, scale, shift)
        h_ref[0:HALO, :] = jnp.where(i > 0, hh, 0.0).astype(_BF16)
        gs = g * (1.0 + scale)
        for r in range(0, tm, NORM_ROWS):
            xb = x_ref[0, r:r + NORM_ROWS, :]
            h_ref[HALO + r:HALO + r + NORM_ROWS, :] = (_rms_scale(xb) * gs + shift).astype(_BF16)
        o_ref[...] = jnp.zeros_like(o_ref)
        upv1_ref[...] = jnp.zeros_like(upv1_ref)
        upg1_ref[...] = jnp.zeros_like(upg1_ref)

    def conv(up_ref, cw_ref, cb_ref, s, c, r):
        cs = slice(s * sub + c * LANES, s * sub + (c + 1) * LANES)
        lead = HALO - SUBLANES
        win = up_ref[c, r + lead:r + HALO + FFN_ROWS, :]
        out = cb_ref[:, cs] + win[SUBLANES:, :] * cw_ref[CONV_WIDTH - 1:CONV_WIDTH, cs]
        for back in range(1, CONV_WIDTH):
            k = CONV_WIDTH - 1 - back
            out = out + pltpu.roll(win, back, 0)[SUBLANES:, :] * cw_ref[k:k + 1, cs]
        return out

    def activation(dst, s, col0, cwv, cwg, cbv, cbg):
        token = jnp.zeros((SUBLANES, LANES), jnp.uint32)
        for c in range(tiles):
            for r in range(0, tm, FFN_ROWS):
                a = _gelu_tanh(conv(upg[s], cwg, cbg, s, c, r)) * conv(upv[s], cwv, cbv, s, c, r)
                dst[r:r + FFN_ROWS, col0 + c * LANES:col0 + (c + 1) * LANES] = a.astype(_BF16)
                bits = lax.bitcast_convert_type(a, jnp.uint32)
                for q in range(0, FFN_ROWS, SUBLANES):
                    token = token | bits[q:q + SUBLANES, :]
        return lax.shift_right_logical(lax.shift_right_logical(token, jnp.uint32(16)), jnp.uint32(16))

    def up_proj(s, hv):
        cs = slice(s * sub, (s + 1) * sub)
        rv = jnp.dot(hv, wv_ref[:, cs], preferred_element_type=_F32)
        rg = jnp.dot(hv, wg_ref[:, cs], preferred_element_type=_F32)
        for c in range(tiles):
            upv[s][c] = rv[:, c * LANES:(c + 1) * LANES]
            upg[s][c] = rg[:, c * LANES:(c + 1) * LANES]

    token = activation(act_ref, 1, 0, pwv_ref, pwg_ref, pbv_ref, pbg_ref)
    up_proj(0, h_ref[...])
    zero = lax.bitcast_convert_type(token, _F32)
    anchor = h_ref[0:BF16_ROWS, 0:LANES].astype(_F32) + jnp.concatenate([zero, zero], axis=0)
    h_ref[0:BF16_ROWS, 0:LANES] = anchor.astype(_BF16)
    up_proj(1, h_ref[...])
    activation(act_ref, 0, sub, cwv_ref, cwg_ref, cbv_ref, cbg_ref)
    o_ref[0] += jnp.dot(act_ref[...], wd_ref[...], preferred_element_type=_F32)

    @pl.when(j == pl.num_programs(2) - 1)
    def _():
        activation(tail_ref, 1, 0, cwv_ref, cwg_ref, cbv_ref, cbg_ref)
        o_ref[0] += jnp.dot(tail_ref[...], wdt_ref[...], preferred_element_type=_F32)
        gg = (1.0 + ada[:, 5 * d:6 * d]) * gpost_ref[...]
        for r in range(0, tm, NORM_ROWS):
            rows = slice(r, r + NORM_ROWS)
            o_ref[0, rows, :] = x_ref[0, rows, :] + _rms_scale(o_ref[0, rows, :]) * gg


def _ffn(x, ada, g_pre, wup_bf16, conv_w, conv_b, wdown_bf16, g_post, tm):
    bsz, seq, d = x.shape
    d_ff = wdown_bf16.shape[0]
    tf = 2 * FFN_SUB
    nj = d_ff // tf
    assert nj * tf == d_ff
    halo_blocks = tm // HALO
    kern = functools.partial(_ffn_kernel, d=d, tm=tm)
    cb = conv_b.reshape(1, 2 * d_ff)
    wd = jnp.concatenate([jnp.zeros((FFN_SUB, d), _BF16), wdown_bf16], axis=0)

    def prev(j):
        return jnp.maximum(j - 1, 0)

    return pl.pallas_call(
        kern,
        grid=(bsz, seq // tm, nj),
        in_specs=[
            pl.BlockSpec((1, tm, d), lambda b, i, j: (b, i, 0)),
            pl.BlockSpec((1, HALO, d), lambda b, i, j: (b, jnp.maximum(i * halo_blocks - 1, 0), 0)),
            pl.BlockSpec((1, 1, 6 * d), lambda b, i, j: (b, 0, 0)),
            pl.BlockSpec((1, d), lambda b, i, j: (0, 0)),
            pl.BlockSpec((d, tf), lambda b, i, j: (0, j)),
            pl.BlockSpec((d, tf), lambda b, i, j: (0, nj + j)),
            pl.BlockSpec((CONV_WIDTH, tf), lambda b, i, j: (0, j)),
            pl.BlockSpec((CONV_WIDTH, tf), lambda b, i, j: (0, nj + j)),
            pl.BlockSpec((1, tf), lambda b, i, j: (0, j)),
            pl.BlockSpec((1, tf), lambda b, i, j: (0, nj + j)),
            pl.BlockSpec((CONV_WIDTH, tf), lambda b, i, j: (0, prev(j))),
            pl.BlockSpec((CONV_WIDTH, tf), lambda b, i, j: (0, nj + prev(j))),
            pl.BlockSpec((1, tf), lambda b, i, j: (0, prev(j))),
            pl.BlockSpec((1, tf), lambda b, i, j: (0, nj + prev(j))),
            pl.BlockSpec((tf, d), lambda b, i, j: (j, 0)),
            _resident((FFN_SUB, d), lambda b, i, j: (d_ff // FFN_SUB, 0)),
            pl.BlockSpec((1, d), lambda b, i, j: (0, 0)),
        ],
        out_specs=pl.BlockSpec((1, tm, d), lambda b, i, j: (b, i, 0)),
        out_shape=jax.ShapeDtypeStruct((bsz, seq, d), _F32),
        scratch_shapes=[
            pltpu.VMEM((tm + HALO, d), _BF16),
            pltpu.VMEM((FFN_SUB // LANES, tm + HALO, LANES), _F32),
            pltpu.VMEM((FFN_SUB // LANES, tm + HALO, LANES), _F32),
            pltpu.VMEM((FFN_SUB // LANES, tm + HALO, LANES), _F32),
            pltpu.VMEM((FFN_SUB // LANES, tm + HALO, LANES), _F32),
            pltpu.VMEM((tm, tf), _BF16),
            pltpu.VMEM((tm, FFN_SUB), _BF16),
        ],
        compiler_params=pltpu.CompilerParams(
            dimension_semantics=("arbitrary", "arbitrary", "arbitrary"),
            vmem_limit_bytes=VMEM_LIMIT),
        name="conv_ffn",
    )(x, x, ada, g_pre.reshape(1, d), wup_bf16, wup_bf16, conv_w, conv_w, cb, cb,
      conv_w, conv_w, cb, cb, wd, wd, g_post.reshape(1, d))


def _largest_tile(n, cap, quantum):
    best = quantum
    for t in range(quantum, min(n, cap) + 1, quantum):
        if n % t == 0:
            best = t
    return best


def kernel(x, c, w_ada, b_ada, g_pre_mix, g_post_mix, w_in, attn_sinks, lam_re, lam_im, log_step,
           ssm_b_re, ssm_b_im, ssm_c_re, ssm_c_im, ssm_d, w_glu, g_attn_out, g_ssm_out, w_out,
           g_pre_ffn, g_post_ffn, w_up, conv_w, conv_b, w_down):
    bsz, seq, d = x.shape
    depth = w_in.shape[0]
    ssm_w = w_glu.shape[1]
    attn_w = w_out.shape[1] - ssm_w
    kv_w = w_in.shape[2] - attn_w - ssm_w
    d_ff = w_down.shape[1]
    assert bsz == SUBLANES, "the scan keeps one batch row per f32 sublane"

    tm = _largest_tile(seq, 512, WINDOW)
    tc = _largest_tile(seq, 128, SUBLANES)
    slab = 2 * LANES
    qb = _largest_tile(seq, 4 * WINDOW, WINDOW) // WINDOW

    ada_all = _adaln(c, w_ada, b_ada)
    for l in range(depth):
        ada = ada_all[l][:, None, :]
        q, kv, u = _in_proj(x, ada, g_pre_mix[l], w_in[l].astype(_BF16), attn_w, kv_w, tm)
        attn_n = _attention(q, kv, attn_sinks[l], g_attn_out[l], qb)
        wb, cw, a_re, a_im = _ssm_params(lam_re[l], lam_im[l], log_step[l], ssm_b_re[l], ssm_b_im[l],
                                         ssm_c_re[l], ssm_c_im[l], slab)
        z = _ssm(u, wb, cw, a_re, a_im, ssm_d[l].reshape(ssm_w), tc)
        x = _post_mix(x, attn_n, z, ada, w_glu[l].astype(_BF16), w_out[l].astype(_BF16),
                      g_ssm_out[l], g_post_mix[l], tm)
        x = _ffn(x, ada, g_pre_ffn[l], w_up[l].astype(_BF16), conv_w[l], conv_b[l],
                 w_down[l].astype(_BF16), g_post_ffn[l], tm)
    return x
```

```python
import functools
import math

import jax
import jax.numpy as jnp
from jax import lax
from jax.experimental import pallas as pl
from jax.experimental.pallas import tpu as pltpu

HEAD_DIM = 64
KV_RATIO = 8
WINDOW = 128
SSM_GROUP = 16
S5_SUB = 64
STATE = 64
CONV_WIDTH = 3
EPS = 1e-6
NEG = -1e30

LANES = 128
SUBLANES = 8
BF16_ROWS = 16
NORM_ROWS = BF16_ROWS
VMEM_LIMIT = 56 * 1024 * 1024

_BF16 = jnp.bfloat16
_F32 = jnp.float32


def _gelu_tanh(x):
    c = math.sqrt(2.0 / math.pi)
    hx = 0.5 * x
    return hx + hx * jnp.tanh(x * (c + (0.044715 * c) * (x * x)))


def _sigmoid(x):
    return 1.0 / (1.0 + jnp.exp(-x))


def _rms_scale(x):
    return x * lax.rsqrt(jnp.mean(x * x, axis=-1, keepdims=True) + EPS)


def _resident(shape, index_map):
    return pl.BlockSpec(shape, index_map, pipeline_mode=pl.Buffered(1))


def _adaln_kernel(c_ref, w_ref, b_ref, o_ref):
    c = c_ref[...]
    ca = c * _sigmoid(c)
    o_ref[0] = jnp.dot(ca, w_ref[0], preferred_element_type=_F32,
                       precision=lax.Precision.HIGHEST) + b_ref[0]


def _adaln(c, w_ada, b_ada, tn=1024):
    depth, d, n = w_ada.shape
    bsz = c.shape[0]
    return pl.pallas_call(
        _adaln_kernel,
        grid=(depth, n // tn),
        in_specs=[
            pl.BlockSpec((bsz, d), lambda l, j: (0, 0)),
            pl.BlockSpec((1, d, tn), lambda l, j: (l, 0, j)),
            pl.BlockSpec((1, 1, tn), lambda l, j: (l, 0, j)),
        ],
        out_specs=pl.BlockSpec((1, bsz, tn), lambda l, j: (l, 0, j)),
        out_shape=jax.ShapeDtypeStruct((depth, bsz, n), _F32),
        compiler_params=pltpu.CompilerParams(
            dimension_semantics=("arbitrary", "arbitrary"), vmem_limit_bytes=VMEM_LIMIT),
        name="adaln",
    )(c, w_ada, b_ada.reshape(depth, 1, n))


def _modulated_norm(xv, g, scale, shift):
    return _rms_scale(xv) * (g * (1.0 + scale)) + shift


def _in_proj_kernel(x_ref, ada_ref, g_ref, w_ref, q_ref, kv_ref, u_ref, *, d, attn_w, kv_w):
    ada = ada_ref[0]
    h = _modulated_norm(x_ref[0], g_ref[...], ada[:, d:2 * d], ada[:, 0:d])
    p = jnp.dot(h.astype(_BF16), w_ref[...], preferred_element_type=_F32)
    q_ref[0] = (p[:, :attn_w] * (HEAD_DIM ** -0.5)).astype(_BF16)
    kv_ref[0] = p[:, attn_w:attn_w + kv_w].astype(_BF16)
    u_ref[0] = p[:, attn_w + kv_w:].astype(_BF16)


def _in_proj(x, ada, g, w_bf16, attn_w, kv_w, tm):
    bsz, seq, d = x.shape
    ncols = w_bf16.shape[1]
    ssm_w = ncols - attn_w - kv_w
    kern = functools.partial(_in_proj_kernel, d=d, attn_w=attn_w, kv_w=kv_w)
    return pl.pallas_call(
        kern,
        grid=(bsz, seq // tm),
        in_specs=[
            pl.BlockSpec((1, tm, d), lambda b, i: (b, i, 0)),
            pl.BlockSpec((1, 1, 6 * d), lambda b, i: (b, 0, 0)),
            pl.BlockSpec((1, d), lambda b, i: (0, 0)),
            _resident((d, ncols), lambda b, i: (0, 0)),
        ],
        out_specs=[
            pl.BlockSpec((1, tm, attn_w), lambda b, i: (b, i, 0)),
            pl.BlockSpec((1, tm, kv_w), lambda b, i: (b, i, 0)),
            pl.BlockSpec((1, tm, ssm_w), lambda b, i: (b, i, 0)),
        ],
        out_shape=[
            jax.ShapeDtypeStruct((bsz, seq, attn_w), _BF16),
            jax.ShapeDtypeStruct((bsz, seq, kv_w), _BF16),
            jax.ShapeDtypeStruct((bsz, seq, ssm_w), _BF16),
        ],
        compiler_params=pltpu.CompilerParams(
            dimension_semantics=("arbitrary", "arbitrary"), vmem_limit_bytes=VMEM_LIMIT),
        name="in_proj",
    )(x, ada, g.reshape(1, d), w_bf16)


def _attention_kernel(sink_ref, q_ref, kvc_ref, kvp_ref, g_ref, o_ref, *, n_kv, qb):
    step = pl.program_id(1)
    w = WINDOW
    rows = (qb + 1) * w
    kv = jnp.concatenate([kvp_ref[0], kvc_ref[0]], axis=0).astype(_F32)
    kw = n_kv * HEAD_DIM
    k2 = kv[:, :kw]
    v2 = kv[:, kw:]
    k2r = pltpu.roll(k2, HEAD_DIM, 1)
    v2r = pltpu.roll(v2, HEAD_DIM, 1)
    lo = lax.broadcasted_iota(jnp.int32, (rows, LANES), 1) < HEAD_DIM
    one_lo = jnp.where(lo, 1.0, 0.0).astype(_BF16)
    one_hi = jnp.where(lo, 0.0, 1.0).astype(_BF16)
    kk = (jnp.where(lo, k2, k2r).astype(_BF16), jnp.where(lo, k2r, k2).astype(_BF16))
    top = (jnp.concatenate([jnp.where(lo, v2, 0.0).astype(_BF16), one_lo], axis=1),
           jnp.concatenate([jnp.where(lo, v2r, 0.0).astype(_BF16), one_lo], axis=1))
    bot = (jnp.concatenate([jnp.where(lo, 0.0, v2r).astype(_BF16), one_hi], axis=1),
           jnp.concatenate([jnp.where(lo, 0.0, v2).astype(_BF16), one_hi], axis=1))

    qi = lax.broadcasted_iota(jnp.int32, (w, 2 * w), 0)
    kj = lax.broadcasted_iota(jnp.int32, (w, 2 * w), 1)
    in_band = (kj > qi) & (kj <= qi + w)
    qlane_lo = lax.broadcasted_iota(jnp.int32, (w, LANES), 1) < HEAD_DIM
    pairs_per_kv = KV_RATIO // 2
    n_pairs = n_kv * pairs_per_kv

    def scores(blk):
        out = []
        for pair in range(n_pairs):
            j = pair // pairs_per_kv
            q2 = q_ref[0, blk * w:(blk + 1) * w, pair * LANES:(pair + 1) * LANES]
            zero = jnp.zeros_like(q2)
            kkj = kk[j][blk * w:(blk + 2) * w]
            dn = (((1,), (1,)), ((), ()))
            out.append((lax.dot_general(jnp.where(qlane_lo, q2, zero), kkj, dn, preferred_element_type=_F32),
                        lax.dot_general(jnp.where(qlane_lo, zero, q2), kkj, dn, preferred_element_type=_F32)))
        return out

    def finish(blk, s_blk):
        valid = in_band if blk > 0 else in_band & ((kj >= w) | (step > 0))
        es, sk = [], []
        for pair in range(n_pairs):
            e2, k2_ = [], []
            for half in range(2):
                sink = sink_ref[2 * pair + half]
                s = jnp.where(valid, s_blk[pair][half], NEG)
                m = jnp.maximum(jnp.max(s, axis=-1, keepdims=True), sink)
                e2.append(jnp.exp(s - m).astype(_BF16))
                k2_.append(jnp.exp(sink - m))
            es.append(jnp.concatenate(e2, axis=1))
            sk.append(jnp.where(qlane_lo, k2_[0], k2_[1]))
        outs = []
        for pair in range(n_pairs):
            j = pair // pairs_per_kv
            r = jnp.concatenate([top[j][blk * w:(blk + 2) * w], bot[j][blk * w:(blk + 2) * w]], axis=0)
            ox = jnp.dot(es[pair], r, preferred_element_type=_F32)
            outs.append(ox[:, :LANES] / (ox[:, LANES:] + sk[pair]))
        o = jnp.concatenate(outs, axis=1)
        o_ref[0, blk * w:(blk + 1) * w, :] = (_rms_scale(o) * g_ref[...]).astype(_BF16)

    s_next = scores(0)
    for blk in range(qb):
        s_cur = s_next
        if blk + 1 < qb:
            s_next = scores(blk + 1)
        finish(blk, s_cur)


def _attention(q, kv, sinks, g, qb):
    bsz, seq, attn_w = q.shape
    kv_w = kv.shape[-1]
    n_kv = kv_w // (2 * HEAD_DIM)
    assert n_kv * HEAD_DIM == LANES and attn_w == n_kv * KV_RATIO * HEAD_DIM
    tq = qb * WINDOW
    kern = functools.partial(_attention_kernel, n_kv=n_kv, qb=qb)
    return pl.pallas_call(
        kern,
        grid=(bsz, seq // tq),
        in_specs=[
            pl.BlockSpec(memory_space=pltpu.SMEM),
            pl.BlockSpec((1, tq, attn_w), lambda b, n: (b, n, 0)),
            pl.BlockSpec((1, tq, kv_w), lambda b, n: (b, n, 0)),
            pl.BlockSpec((1, WINDOW, kv_w), lambda b, n: (b, jnp.maximum(n * qb - 1, 0), 0)),
            pl.BlockSpec((1, attn_w), lambda b, n: (0, 0)),
        ],
        out_specs=pl.BlockSpec((1, tq, attn_w), lambda b, n: (b, n, 0)),
        out_shape=jax.ShapeDtypeStruct((bsz, seq, attn_w), _BF16),
        compiler_params=pltpu.CompilerParams(
            dimension_semantics=("arbitrary", "arbitrary"), vmem_limit_bytes=VMEM_LIMIT),
        name="swa",
    )(sinks, q, kv, kv, g.reshape(1, attn_w))


def _ssm_kernel(u_ref, wb_ref, cw_ref, ar_ref, ai_ref, d_ref, z_ref, xs_ref, st_ref, il_ref,
                *, bsz, tc, ns):
    @pl.when(pl.program_id(1) == 0)
    def _():
        st_ref[...] = jnp.zeros_like(st_ref)

    tiles = il_ref.shape[0]
    for b in range(bsz):
        ub = u_ref[b].astype(_F32)
        for c in range(tiles):
            il_ref[c, pl.ds(b, tc, stride=bsz), :] = ub[:, c * LANES:(c + 1) * LANES]
    ar = jnp.broadcast_to(ar_ref[0], (bsz, ns))
    ai = jnp.broadcast_to(ai_ref[0], (bsz, ns))
    xr = st_ref[:, 0:ns]
    xi = st_ref[:, ns:2 * ns]
    sr = S5_SUB * bsz
    n_sub = tc // S5_SUB
    u32 = []
    for k in range(n_sub):
        rows = slice(k * sr, (k + 1) * sr)
        uk = jnp.concatenate([il_ref[c, rows, :] for c in range(tiles)], axis=1)
        u32.append(uk)
        xs_ref[rows, :] = jnp.dot(uk.astype(_BF16), wb_ref[0], preferred_element_type=_F32)
    ys = []
    for k in range(n_sub):
        for t in range(k * S5_SUB, (k + 1) * S5_SUB):
            r = slice(t * bsz, (t + 1) * bsz)
            nxr = ar * xr - ai * xi + xs_ref[r, 0:ns]
            nxi = ar * xi + ai * xr + xs_ref[r, ns:2 * ns]
            xs_ref[r, 0:ns] = nxr
            xs_ref[r, ns:2 * ns] = nxi
            xr, xi = nxr, nxi
        rows = slice(k * sr, (k + 1) * sr)
        ys.append(jnp.dot(xs_ref[rows, 0:ns].astype(_BF16), cw_ref[0, 0:ns, :], preferred_element_type=_F32)
                  + jnp.dot(xs_ref[rows, ns:2 * ns].astype(_BF16), cw_ref[0, ns:2 * ns, :],
                            preferred_element_type=_F32)
                  + d_ref[...] * u32[k])
    st_ref[:, 0:ns] = xr
    st_ref[:, ns:2 * ns] = xi
    y = jnp.concatenate(ys, axis=0)
    z = _gelu_tanh(y)
    for c in range(tiles):
        il_ref[c] = z[:, c * LANES:(c + 1) * LANES]
    for b in range(bsz):
        zb = jnp.concatenate([il_ref[c, pl.ds(b, tc, stride=bsz), :] for c in range(tiles)], axis=1)
        z_ref[b] = zb.astype(_BF16)


def _ssm_params(lam_re, lam_im, log_step, b_re, b_im, c_re, c_im, slab):
    g, p = lam_re.shape
    h = SSM_GROUP
    gs = slab // h
    n_slabs = g // gs
    dt = jnp.exp(log_step)[:, None]
    mag = jnp.exp(lam_re * dt)
    ang = lam_im * dt
    ab_re = mag * jnp.cos(ang)
    ab_im = mag * jnp.sin(ang)
    den = lam_re * lam_re + lam_im * lam_im
    f_re = ((ab_re - 1.0) * lam_re + ab_im * lam_im) / den
    f_im = (ab_im * lam_re - (ab_re - 1.0) * lam_im) / den
    bb_re = f_re[..., None] * b_re - f_im[..., None] * b_im
    bb_im = f_re[..., None] * b_im + f_im[..., None] * b_re
    eye = jnp.eye(gs, dtype=_F32)

    def block_diag_in(bb):
        t = bb.reshape(n_slabs, gs, p, h)
        return jnp.einsum('sgph,gk->sghkp', t, eye).reshape(n_slabs, gs * h, gs * p)

    def block_diag_out(cc):
        t = cc.reshape(n_slabs, gs, h, p)
        return jnp.einsum('sghp,gk->sgpkh', t, eye).reshape(n_slabs, gs * p, gs * h)

    wb = jnp.concatenate([block_diag_in(bb_re), block_diag_in(bb_im)], axis=2).astype(_BF16)
    cw = jnp.concatenate([block_diag_out(c_re), block_diag_out(-c_im)], axis=1).astype(_BF16)
    a_re = ab_re.reshape(n_slabs, 1, gs * p)
    a_im = ab_im.reshape(n_slabs, 1, gs * p)
    return wb, cw, a_re, a_im


def _ssm(u, wb, cw, a_re, a_im, d_skip, tc):
    bsz, seq, width = u.shape
    n_slabs, slab, ns2 = wb.shape
    ns = ns2 // 2
    kern = functools.partial(_ssm_kernel, bsz=bsz, tc=tc, ns=ns)
    return pl.pallas_call(
        kern,
        grid=(n_slabs, seq // tc),
        in_specs=[
            pl.BlockSpec((bsz, tc, slab), lambda s, t: (0, t, s)),
            pl.BlockSpec((1, slab, ns2), lambda s, t: (s, 0, 0)),
            pl.BlockSpec((1, ns2, slab), lambda s, t: (s, 0, 0)),
            pl.BlockSpec((1, 1, ns), lambda s, t: (s, 0, 0)),
            pl.BlockSpec((1, 1, ns), lambda s, t: (s, 0, 0)),
            pl.BlockSpec((1, slab), lambda s, t: (0, s)),
        ],
        out_specs=pl.BlockSpec((bsz, tc, slab), lambda s, t: (0, t, s)),
        out_shape=jax.ShapeDtypeStruct((bsz, seq, width), _BF16),
        scratch_shapes=[
            pltpu.VMEM((tc * bsz, ns2), _F32),
            pltpu.VMEM((bsz, ns2), _F32),
            pltpu.VMEM((slab // LANES, tc * bsz, LANES), _F32),
        ],
        compiler_params=pltpu.CompilerParams(
            dimension_semantics=("arbitrary", "arbitrary"), vmem_limit_bytes=VMEM_LIMIT),
        name="s5",
    )(u, wb, cw, a_re, a_im, d_skip.reshape(1, width))


def _post_mix_kernel(x_ref, attn_ref, z_ref, ada_ref, wglu_ref, wout_ref, gssm_ref, gpost_ref,
                     o_ref, *, d, attn_w):
    z = z_ref[0]
    zf = z.astype(_F32)
    ssm = zf * _sigmoid(jnp.dot(z, wglu_ref[...], preferred_element_type=_F32))
    ssm_n = (_rms_scale(ssm) * gssm_ref[...]).astype(_BF16)
    o_ref[0] = (jnp.dot(attn_ref[0], wout_ref[0:attn_w, :], preferred_element_type=_F32)
                + jnp.dot(ssm_n, wout_ref[attn_w:, :], preferred_element_type=_F32))
    gg = (1.0 + ada_ref[0][:, 2 * d:3 * d]) * gpost_ref[...]
    for r in range(0, o_ref.shape[1], NORM_ROWS):
        rows = slice(r, r + NORM_ROWS)
        o_ref[0, rows, :] = x_ref[0, rows, :] + _rms_scale(o_ref[0, rows, :]) * gg


def _post_mix(x, attn_n, z, ada, wglu_bf16, wout_bf16, g_ssm, g_post, tm):
    bsz, seq, d = x.shape
    attn_w = attn_n.shape[-1]
    ssm_w = wglu_bf16.shape[0]
    kern = functools.partial(_post_mix_kernel, d=d, attn_w=attn_w)
    return pl.pallas_call(
        kern,
        grid=(bsz, seq // tm),
        in_specs=[
            pl.BlockSpec((1, tm, d), lambda b, i: (b, i, 0)),
            pl.BlockSpec((1, tm, attn_w), lambda b, i: (b, i, 0)),
            pl.BlockSpec((1, tm, ssm_w), lambda b, i: (b, i, 0)),
            pl.BlockSpec((1, 1, 6 * d), lambda b, i: (b, 0, 0)),
            _resident((ssm_w, ssm_w), lambda b, i: (0, 0)),
            _resident((attn_w + ssm_w, d), lambda b, i: (0, 0)),
            pl.BlockSpec((1, ssm_w), lambda b, i: (0, 0)),
            pl.BlockSpec((1, d), lambda b, i: (0, 0)),
        ],
        out_specs=pl.BlockSpec((1, tm, d), lambda b, i: (b, i, 0)),
        out_shape=jax.ShapeDtypeStruct((bsz, seq, d), _F32),
        compiler_params=pltpu.CompilerParams(
            dimension_semantics=("arbitrary", "arbitrary"), vmem_limit_bytes=VMEM_LIMIT),
        name="post_mix",
    )(x, attn_n, z, ada, wglu_bf16, wout_bf16,
      g_ssm.reshape(1, ssm_w), g_post.reshape(1, d))


HALO = BF16_ROWS
FFN_SUB = 2 * LANES
FFN_ROWS = 64


def _ffn_kernel(x_ref, xh_ref, ada_ref, gpre_ref, wv_ref, wg_ref, cwv_ref, cwg_ref, cbv_ref, cbg_ref,
                wd_ref, wdt_ref, gpost_ref, o_ref, h_ref, upv_ref, upg_ref, act_ref, carry_ref, *, d, tm):
    i = pl.program_id(1)
    j = pl.program_id(2)
    ada = ada_ref[0]
    sub = FFN_SUB

    @pl.when(j == 0)
    def _():
        g = gpre_ref[...]
        scale = ada[:, 4 * d:5 * d]
        shift = ada[:, 3 * d:4 * d]
        hh = _modulated_norm(xh_ref[0], g, scale, shift)
        h_ref[0:HALO, :] = jnp.where(i > 0, hh, 0.0).astype(_BF16)
        gs = g * (1.0 + scale)
        for r in range(0, tm, NORM_ROWS):
            xb = x_ref[0, r:r + NORM_ROWS, :]
            h_ref[HALO + r:HALO + r + NORM_ROWS, :] = (_rms_scale(xb) * gs + shift).astype(_BF16)
        o_ref[...] = jnp.zeros_like(o_ref)
        carry_ref[...] = jnp.zeros_like(carry_ref)

    hv = h_ref[...]
    tiles = sub // LANES
    for s in range(2):
        cs = slice(s * sub, (s + 1) * sub)
        rv = jnp.dot(hv, wv_ref[:, cs], preferred_element_type=_F32)
        rg = jnp.dot(hv, wg_ref[:, cs], preferred_element_type=_F32)
        for c in range(tiles):
            upv_ref[s * tiles + c] = rv[:, c * LANES:(c + 1) * LANES]
            upg_ref[s * tiles + c] = rg[:, c * LANES:(c + 1) * LANES]

    def conv(up_ref, cw_ref, cb_ref, t, r):
        cs = slice(t * LANES, (t + 1) * LANES)
        out = cb_ref[:, cs]
        for k in range(CONV_WIDTH):
            off = HALO - (CONV_WIDTH - 1) + k + r
            out = out + up_ref[t, pl.ds(off, FFN_ROWS, stride=1), :] * cw_ref[k:k + 1, cs]
        return out

    def activation(dst, s, col0):
        for c in range(tiles):
            t = s * tiles + c
            for r in range(0, tm, FFN_ROWS):
                a = (_gelu_tanh(conv(upg_ref, cwg_ref, cbg_ref, t, r))
                     * conv(upv_ref, cwv_ref, cbv_ref, t, r))
                dst[r:r + FFN_ROWS, col0 + c * LANES:col0 + (c + 1) * LANES] = a.astype(_BF16)

    act_ref[:, 0:sub] = carry_ref[...]
    activation(act_ref, 0, sub)
    o_ref[0] += jnp.dot(act_ref[...], wd_ref[...], preferred_element_type=_F32)
    activation(carry_ref, 1, 0)

    @pl.when(j == pl.num_programs(2) - 1)
    def _():
        o_ref[0] += jnp.dot(carry_ref[...], wdt_ref[...], preferred_element_type=_F32)
        gg = (1.0 + ada[:, 5 * d:6 * d]) * gpost_ref[...]
        for r in range(0, tm, NORM_ROWS):
            rows = slice(r, r + NORM_ROWS)
            o_ref[0, rows, :] = x_ref[0, rows, :] + _rms_scale(o_ref[0, rows, :]) * gg


def _ffn(x, ada, g_pre, wup_bf16, conv_w, conv_b, wdown_bf16, g_post, tm):
    bsz, seq, d = x.shape
    d_ff = wdown_bf16.shape[0]
    tf = 2 * FFN_SUB
    nj = d_ff // tf
    assert nj * tf == d_ff
    halo_blocks = tm // HALO
    kern = functools.partial(_ffn_kernel, d=d, tm=tm)
    cb = conv_b.reshape(1, 2 * d_ff)
    wd = jnp.concatenate([jnp.zeros((FFN_SUB, d), _BF16), wdown_bf16], axis=0)
    return pl.pallas_call(
        kern,
        grid=(bsz, seq // tm, nj),
        in_specs=[
            pl.BlockSpec((1, tm, d), lambda b, i, j: (b, i, 0)),
            pl.BlockSpec((1, HALO, d), lambda b, i, j: (b, jnp.maximum(i * halo_blocks - 1, 0), 0)),
            pl.BlockSpec((1, 1, 6 * d), lambda b, i, j: (b, 0, 0)),
            pl.BlockSpec((1, d), lambda b, i, j: (0, 0)),
            pl.BlockSpec((d, tf), lambda b, i, j: (0, j)),
            pl.BlockSpec((d, tf), lambda b, i, j: (0, nj + j)),
            pl.BlockSpec((CONV_WIDTH, tf), lambda b, i, j: (0, j)),
            pl.BlockSpec((CONV_WIDTH, tf), lambda b, i, j: (0, nj + j)),
            pl.BlockSpec((1, tf), lambda b, i, j: (0, j)),
            pl.BlockSpec((1, tf), lambda b, i, j: (0, nj + j)),
            pl.BlockSpec((tf, d), lambda b, i, j: (j, 0)),
            _resident((FFN_SUB, d), lambda b, i, j: (d_ff // FFN_SUB, 0)),
            pl.BlockSpec((1, d), lambda b, i, j: (0, 0)),
        ],
        out_specs=pl.BlockSpec((1, tm, d), lambda b, i, j: (b, i, 0)),
        out_shape=jax.ShapeDtypeStruct((bsz, seq, d), _F32),
        scratch_shapes=[
            pltpu.VMEM((tm + HALO, d), _BF16),
            pltpu.VMEM((tf // LANES, tm + HALO, LANES), _F32),
            pltpu.VMEM((tf // LANES, tm + HALO, LANES), _F32),
            pltpu.VMEM((tm, tf), _BF16),
            pltpu.VMEM((tm, FFN_SUB), _BF16),
        ],
        compiler_params=pltpu.CompilerParams(
            dimension_semantics=("arbitrary", "arbitrary", "arbitrary"),
            vmem_limit_bytes=VMEM_LIMIT),
        name="conv_ffn",
    )(x, x, ada, g_pre.reshape(1, d), wup_bf16, wup_bf16, conv_w, conv_w, cb, cb,
      wd, wd, g_post.reshape(1, d))


def _largest_tile(n, cap, quantum):
    best = quantum
    for t in range(quantum, min(n, cap) + 1, quantum):
        if n % t == 0:
            best = t
    return best


def kernel(x, c, w_ada, b_ada, g_pre_mix, g_post_mix, w_in, attn_sinks, lam_re, lam_im, log_step,
           ssm_b_re, ssm_b_im, ssm_c_re, ssm_c_im, ssm_d, w_glu, g_attn_out, g_ssm_out, w_out,
           g_pre_ffn, g_post_ffn, w_up, conv_w, conv_b, w_down):
    bsz, seq, d = x.shape
    depth = w_in.shape[0]
    ssm_w = w_glu.shape[1]
    attn_w = w_out.shape[1] - ssm_w
    kv_w = w_in.shape[2] - attn_w - ssm_w
    d_ff = w_down.shape[1]
    assert bsz == SUBLANES, "the scan keeps one batch row per f32 sublane"

    tm = _largest_tile(seq, 512, WINDOW)
    tc = _largest_tile(seq, 128, SUBLANES)
    slab = 2 * LANES
    qb = _largest_tile(seq, 4 * WINDOW, WINDOW) // WINDOW

    ada_all = _adaln(c, w_ada, b_ada)
    for l in range(depth):
        ada = ada_all[l][:, None, :]
        q, kv, u = _in_proj(x, ada, g_pre_mix[l], w_in[l].astype(_BF16), attn_w, kv_w, tm)
        attn_n = _attention(q, kv, attn_sinks[l], g_attn_out[l], qb)
        wb, cw, a_re, a_im = _ssm_params(lam_re[l], lam_im[l], log_step[l], ssm_b_re[l], ssm_b_im[l],
                                         ssm_c_re[l], ssm_c_im[l], slab)
        z = _ssm(u, wb, cw, a_re, a_im, ssm_d[l].reshape(ssm_w), tc)
        x = _post_mix(x, attn_n, z, ada, w_glu[l].astype(_BF16), w_out[l].astype(_BF16),
                      g_ssm_out[l], g_post_mix[l], tm)
        x = _ffn(x, ada, g_pre_ffn[l], w_up[l].astype(_BF16), conv_w[l], conv_b[l],
                 w_down[l].astype(_BF16), g_post_ffn[l], tm)
    return x
```

```python
import functools
import math

import jax
import jax.numpy as jnp
from jax import lax
from jax.experimental import pallas as pl
from jax.experimental.pallas import tpu as pltpu

HEAD_DIM = 64
KV_RATIO = 8
WINDOW = 128
SSM_GROUP = 16
S5_SUB = 64
STATE = 64
CONV_WIDTH = 3
EPS = 1e-6
NEG = -1e30

LANES = 128
SUBLANES = 8
BF16_ROWS = 16
NORM_ROWS = BF16_ROWS
VMEM_LIMIT = 56 * 1024 * 1024

_BF16 = jnp.bfloat16
_F32 = jnp.float32


def _gelu_tanh(x):
    c = math.sqrt(2.0 / math.pi)
    hx = 0.5 * x
    return hx + hx * jnp.tanh(x * (c + (0.044715 * c) * (x * x)))


def _sigmoid(x):
    return 1.0 / (1.0 + jnp.exp(-x))


def _rms_scale(x):
    return x * lax.rsqrt(jnp.mean(x * x, axis=-1, keepdims=True) + EPS)


def _resident(shape, index_map):
    return pl.BlockSpec(shape, index_map, pipeline_mode=pl.Buffered(1))


def _adaln_kernel(c_ref, w_ref, b_ref, o_ref):
    c = c_ref[...]
    ca = c * _sigmoid(c)
    o_ref[0] = jnp.dot(ca, w_ref[0], preferred_element_type=_F32,
                       precision=lax.Precision.HIGHEST) + b_ref[0]


def _adaln(c, w_ada, b_ada, tn=1024):
    depth, d, n = w_ada.shape
    bsz = c.shape[0]
    return pl.pallas_call(
        _adaln_kernel,
        grid=(depth, n // tn),
        in_specs=[
            pl.BlockSpec((bsz, d), lambda l, j: (0, 0)),
            pl.BlockSpec((1, d, tn), lambda l, j: (l, 0, j)),
            pl.BlockSpec((1, 1, tn), lambda l, j: (l, 0, j)),
        ],
        out_specs=pl.BlockSpec((1, bsz, tn), lambda l, j: (l, 0, j)),
        out_shape=jax.ShapeDtypeStruct((depth, bsz, n), _F32),
        compiler_params=pltpu.CompilerParams(
            dimension_semantics=("arbitrary", "arbitrary"), vmem_limit_bytes=VMEM_LIMIT),
        name="adaln",
    )(c, w_ada, b_ada.reshape(depth, 1, n))


def _modulated_norm(xv, g, scale, shift):
    return _rms_scale(xv) * (g * (1.0 + scale)) + shift


def _in_proj_kernel(x_ref, ada_ref, g_ref, w_ref, q_ref, kv_ref, u_ref, *, d, attn_w, kv_w):
    ada = ada_ref[0]
    h = _modulated_norm(x_ref[0], g_ref[...], ada[:, d:2 * d], ada[:, 0:d])
    p = jnp.dot(h.astype(_BF16), w_ref[...], preferred_element_type=_F32)
    q_ref[0] = (p[:, :attn_w] * (HEAD_DIM ** -0.5)).astype(_BF16)
    kv_ref[0] = p[:, attn_w:attn_w + kv_w].astype(_BF16)
    u_ref[0] = p[:, attn_w + kv_w:].astype(_BF16)


def _in_proj(x, ada, g, w_bf16, attn_w, kv_w, tm):
    bsz, seq, d = x.shape
    ncols = w_bf16.shape[1]
    ssm_w = ncols - attn_w - kv_w
    kern = functools.partial(_in_proj_kernel, d=d, attn_w=attn_w, kv_w=kv_w)
    return pl.pallas_call(
        kern,
        grid=(bsz, seq // tm),
        in_specs=[
            pl.BlockSpec((1, tm, d), lambda b, i: (b, i, 0)),
            pl.BlockSpec((1, 1, 6 * d), lambda b, i: (b, 0, 0)),
            pl.BlockSpec((1, d), lambda b, i: (0, 0)),
            _resident((d, ncols), lambda b, i: (0, 0)),
        ],
        out_specs=[
            pl.BlockSpec((1, tm, attn_w), lambda b, i: (b, i, 0)),
            pl.BlockSpec((1, tm, kv_w), lambda b, i: (b, i, 0)),
            pl.BlockSpec((1, tm, ssm_w), lambda b, i: (b, i, 0)),
        ],
        out_shape=[
            jax.ShapeDtypeStruct((bsz, seq, attn_w), _BF16),
            jax.ShapeDtypeStruct((bsz, seq, kv_w), _BF16),
            jax.ShapeDtypeStruct((bsz, seq, ssm_w), _BF16),
        ],
        compiler_params=pltpu.CompilerParams(
            dimension_semantics=("arbitrary", "arbitrary"), vmem_limit_bytes=VMEM_LIMIT),
        name="in_proj",
    )(x, ada, g.reshape(1, d), w_bf16)


def _attention_kernel(sink_ref, q_ref, kvc_ref, kvp_ref, g_ref, o_ref, *, n_kv, qb):
    step = pl.program_id(1)
    w = WINDOW
    rows = (qb + 1) * w
    kv = jnp.concatenate([kvp_ref[0], kvc_ref[0]], axis=0).astype(_F32)
    kw = n_kv * HEAD_DIM
    k2 = kv[:, :kw]
    v2 = kv[:, kw:]
    k2r = pltpu.roll(k2, HEAD_DIM, 1)
    v2r = pltpu.roll(v2, HEAD_DIM, 1)
    lo = lax.broadcasted_iota(jnp.int32, (rows, LANES), 1) < HEAD_DIM
    one_lo = jnp.where(lo, 1.0, 0.0).astype(_BF16)
    one_hi = jnp.where(lo, 0.0, 1.0).astype(_BF16)
    kk = (jnp.where(lo, k2, k2r).astype(_BF16), jnp.where(lo, k2r, k2).astype(_BF16))
    top = (jnp.concatenate([jnp.where(lo, v2, 0.0).astype(_BF16), one_lo], axis=1),
           jnp.concatenate([jnp.where(lo, v2r, 0.0).astype(_BF16), one_lo], axis=1))
    bot = (jnp.concatenate([jnp.where(lo, 0.0, v2r).astype(_BF16), one_hi], axis=1),
           jnp.concatenate([jnp.where(lo, 0.0, v2).astype(_BF16), one_hi], axis=1))

    qi = lax.broadcasted_iota(jnp.int32, (w, 2 * w), 0)
    kj = lax.broadcasted_iota(jnp.int32, (w, 2 * w), 1)
    in_band = (kj > qi) & (kj <= qi + w)
    qlane_lo = lax.broadcasted_iota(jnp.int32, (w, LANES), 1) < HEAD_DIM
    pairs_per_kv = KV_RATIO // 2
    n_pairs = n_kv * pairs_per_kv

    def scores(blk):
        out = []
        for pair in range(n_pairs):
            j = pair // pairs_per_kv
            q2 = q_ref[0, blk * w:(blk + 1) * w, pair * LANES:(pair + 1) * LANES]
            zero = jnp.zeros_like(q2)
            kkj = kk[j][blk * w:(blk + 2) * w]
            dn = (((1,), (1,)), ((), ()))
            out.append((lax.dot_general(jnp.where(qlane_lo, q2, zero), kkj, dn, preferred_element_type=_F32),
                        lax.dot_general(jnp.where(qlane_lo, zero, q2), kkj, dn, preferred_element_type=_F32)))
        return out

    def finish(blk, s_blk):
        valid = in_band if blk > 0 else in_band & ((kj >= w) | (step > 0))
        es, sk = [], []
        for pair in range(n_pairs):
            e2, k2_ = [], []
            for half in range(2):
                sink = sink_ref[2 * pair + half]
                s = jnp.where(valid, s_blk[pair][half], NEG)
                m = jnp.maximum(jnp.max(s, axis=-1, keepdims=True), sink)
                e2.append(jnp.exp(s - m).astype(_BF16))
                k2_.append(jnp.exp(sink - m))
            es.append(jnp.concatenate(e2, axis=1))
            sk.append(jnp.where(qlane_lo, k2_[0], k2_[1]))
        outs = []
        for pair in range(n_pairs):
            j = pair // pairs_per_kv
            r = jnp.concatenate([top[j][blk * w:(blk + 2) * w], bot[j][blk * w:(blk + 2) * w]], axis=0)
            ox = jnp.dot(es[pair], r, preferred_element_type=_F32)
            outs.append(ox[:, :LANES] / (ox[:, LANES:] + sk[pair]))
        o = jnp.concatenate(outs, axis=1)
        o_ref[0, blk * w:(blk + 1) * w, :] = (_rms_scale(o) * g_ref[...]).astype(_BF16)

    s_next = scores(0)
    for blk in range(qb):
        s_cur = s_next
        if blk + 1 < qb:
            s_next = scores(blk + 1)
        finish(blk, s_cur)


def _attention(q, kv, sinks, g, qb):
    bsz, seq, attn_w = q.shape
    kv_w = kv.shape[-1]
    n_kv = kv_w // (2 * HEAD_DIM)
    assert n_kv * HEAD_DIM == LANES and attn_w == n_kv * KV_RATIO * HEAD_DIM
    tq = qb * WINDOW
    kern = functools.partial(_attention_kernel, n_kv=n_kv, qb=qb)
    return pl.pallas_call(
        kern,
        grid=(bsz, seq // tq),
        in_specs=[
            pl.BlockSpec(memory_space=pltpu.SMEM),
            pl.BlockSpec((1, tq, attn_w), lambda b, n: (b, n, 0)),
            pl.BlockSpec((1, tq, kv_w), lambda b, n: (b, n, 0)),
            pl.BlockSpec((1, WINDOW, kv_w), lambda b, n: (b, jnp.maximum(n * qb - 1, 0), 0)),
            pl.BlockSpec((1, attn_w), lambda b, n: (0, 0)),
        ],
        out_specs=pl.BlockSpec((1, tq, attn_w), lambda b, n: (b, n, 0)),
        out_shape=jax.ShapeDtypeStruct((bsz, seq, attn_w), _BF16),
        compiler_params=pltpu.CompilerParams(
            dimension_semantics=("arbitrary", "arbitrary"), vmem_limit_bytes=VMEM_LIMIT),
        name="swa",
    )(sinks, q, kv, kv, g.reshape(1, attn_w))


def _ssm_kernel(u_ref, wb_ref, cw_ref, ar_ref, ai_ref, d_ref, z_ref, xs_ref, st_ref, il_ref,
                *, bsz, tc, ns):
    @pl.when(pl.program_id(1) == 0)
    def _():
        st_ref[...] = jnp.zeros_like(st_ref)

    tiles = il_ref.shape[0]
    for b in range(bsz):
        ub = u_ref[b].astype(_F32)
        for c in range(tiles):
            il_ref[c, pl.ds(b, tc, stride=bsz), :] = ub[:, c * LANES:(c + 1) * LANES]
    ar = jnp.broadcast_to(ar_ref[0], (bsz, ns))
    ai = jnp.broadcast_to(ai_ref[0], (bsz, ns))
    xr = st_ref[:, 0:ns]
    xi = st_ref[:, ns:2 * ns]
    sr = S5_SUB * bsz
    n_sub = tc // S5_SUB
    u32 = []
    for k in range(n_sub):
        rows = slice(k * sr, (k + 1) * sr)
        uk = jnp.concatenate([il_ref[c, rows, :] for c in range(tiles)], axis=1)
        u32.append(uk)
        xs_ref[rows, :] = jnp.dot(uk.astype(_BF16), wb_ref[0], preferred_element_type=_F32)
    ys = []
    for k in range(n_sub):
        for t in range(k * S5_SUB, (k + 1) * S5_SUB):
            r = slice(t * bsz, (t + 1) * bsz)
            nxr = ar * xr - ai * xi + xs_ref[r, 0:ns]
            nxi = ar * xi + ai * xr + xs_ref[r, ns:2 * ns]
            xs_ref[r, 0:ns] = nxr
            xs_ref[r, ns:2 * ns] = nxi
            xr, xi = nxr, nxi
        rows = slice(k * sr, (k + 1) * sr)
        ys.append(jnp.dot(xs_ref[rows, 0:ns].astype(_BF16), cw_ref[0, 0:ns, :], preferred_element_type=_F32)
                  + jnp.dot(xs_ref[rows, ns:2 * ns].astype(_BF16), cw_ref[0, ns:2 * ns, :],
                            preferred_element_type=_F32)
                  + d_ref[...] * u32[k])
    st_ref[:, 0:ns] = xr
    st_ref[:, ns:2 * ns] = xi
    y = jnp.concatenate(ys, axis=0)
    z = _gelu_tanh(y)
    for c in range(tiles):
        il_ref[c] = z[:, c * LANES:(c + 1) * LANES]
    for b in range(bsz):
        zb = jnp.concatenate([il_ref[c, pl.ds(b, tc, stride=bsz), :] for c in range(tiles)], axis=1)
        z_ref[b] = zb.astype(_BF16)


def _ssm_params(lam_re, lam_im, log_step, b_re, b_im, c_re, c_im, slab):
    g, p = lam_re.shape
    h = SSM_GROUP
    gs = slab // h
    n_slabs = g // gs
    dt = jnp.exp(log_step)[:, None]
    mag = jnp.exp(lam_re * dt)
    ang = lam_im * dt
    ab_re = mag * jnp.cos(ang)
    ab_im = mag * jnp.sin(ang)
    den = lam_re * lam_re + lam_im * lam_im
    f_re = ((ab_re - 1.0) * lam_re + ab_im * lam_im) / den
    f_im = (ab_im * lam_re - (ab_re - 1.0) * lam_im) / den
    bb_re = f_re[..., None] * b_re - f_im[..., None] * b_im
    bb_im = f_re[..., None] * b_im + f_im[..., None] * b_re
    eye = jnp.eye(gs, dtype=_F32)

    def block_diag_in(bb):
        t = bb.reshape(n_slabs, gs, p, h)
        return jnp.einsum('sgph,gk->sghkp', t, eye).reshape(n_slabs, gs * h, gs * p)

    def block_diag_out(cc):
        t = cc.reshape(n_slabs, gs, h, p)
        return jnp.einsum('sghp,gk->sgpkh', t, eye).reshape(n_slabs, gs * p, gs * h)

    wb = jnp.concatenate([block_diag_in(bb_re), block_diag_in(bb_im)], axis=2).astype(_BF16)
    cw = jnp.concatenate([block_diag_out(c_re), block_diag_out(-c_im)], axis=1).astype(_BF16)
    a_re = ab_re.reshape(n_slabs, 1, gs * p)
    a_im = ab_im.reshape(n_slabs, 1, gs * p)
    return wb, cw, a_re, a_im


def _ssm(u, wb, cw, a_re, a_im, d_skip, tc):
    bsz, seq, width = u.shape
    n_slabs, slab, ns2 = wb.shape
    ns = ns2 // 2
    kern = functools.partial(_ssm_kernel, bsz=bsz, tc=tc, ns=ns)
    return pl.pallas_call(
        kern,
        grid=(n_slabs, seq // tc),
        in_specs=[
            pl.BlockSpec((bsz, tc, slab), lambda s, t: (0, t, s)),
            pl.BlockSpec((1, slab, ns2), lambda s, t: (s, 0, 0)),
            pl.BlockSpec((1, ns2, slab), lambda s, t: (s, 0, 0)),
            pl.BlockSpec((1, 1, ns), lambda s, t: (s, 0, 0)),
            pl.BlockSpec((1, 1, ns), lambda s, t: (s, 0, 0)),
            pl.BlockSpec((1, slab), lambda s, t: (0, s)),
        ],
        out_specs=pl.BlockSpec((bsz, tc, slab), lambda s, t: (0, t, s)),
        out_shape=jax.ShapeDtypeStruct((bsz, seq, width), _BF16),
        scratch_shapes=[
            pltpu.VMEM((tc * bsz, ns2), _F32),
            pltpu.VMEM((bsz, ns2), _F32),
            pltpu.VMEM((slab // LANES, tc * bsz, LANES), _F32),
        ],
        compiler_params=pltpu.CompilerParams(
            dimension_semantics=("arbitrary", "arbitrary"), vmem_limit_bytes=VMEM_LIMIT),
        name="s5",
    )(u, wb, cw, a_re, a_im, d_skip.reshape(1, width))


def _post_mix_kernel(x_ref, attn_ref, z_ref, ada_ref, wglu_ref, wout_ref, gssm_ref, gpost_ref,
                     o_ref, *, d, attn_w):
    z = z_ref[0]
    zf = z.astype(_F32)
    ssm = zf * _sigmoid(jnp.dot(z, wglu_ref[...], preferred_element_type=_F32))
    ssm_n = (_rms_scale(ssm) * gssm_ref[...]).astype(_BF16)
    o_ref[0] = (jnp.dot(attn_ref[0], wout_ref[0:attn_w, :], preferred_element_type=_F32)
                + jnp.dot(ssm_n, wout_ref[attn_w:, :], preferred_element_type=_F32))
    gg = (1.0 + ada_ref[0][:, 2 * d:3 * d]) * gpost_ref[...]
    for r in range(0, o_ref.shape[1], NORM_ROWS):
        rows = slice(r, r + NORM_ROWS)
        o_ref[0, rows, :] = x_ref[0, rows, :] + _rms_scale(o_ref[0, rows, :]) * gg


def _post_mix(x, attn_n, z, ada, wglu_bf16, wout_bf16, g_ssm, g_post, tm):
    bsz, seq, d = x.shape
    attn_w = attn_n.shape[-1]
    ssm_w = wglu_bf16.shape[0]
    kern = functools.partial(_post_mix_kernel, d=d, attn_w=attn_w)
    return pl.pallas_call(
        kern,
        grid=(bsz, seq // tm),
        in_specs=[
            pl.BlockSpec((1, tm, d), lambda b, i: (b, i, 0)),
            pl.BlockSpec((1, tm, attn_w), lambda b, i: (b, i, 0)),
            pl.BlockSpec((1, tm, ssm_w), lambda b, i: (b, i, 0)),
            pl.BlockSpec((1, 1, 6 * d), lambda b, i: (b, 0, 0)),
            _resident((ssm_w, ssm_w), lambda b, i: (0, 0)),
            _resident((attn_w + ssm_w, d), lambda b, i: (0, 0)),
            pl.BlockSpec((1, ssm_w), lambda b, i: (0, 0)),
            pl.BlockSpec((1, d), lambda b, i: (0, 0)),
        ],
        out_specs=pl.BlockSpec((1, tm, d), lambda b, i: (b, i, 0)),
        out_shape=jax.ShapeDtypeStruct((bsz, seq, d), _F32),
        compiler_params=pltpu.CompilerParams(
            dimension_semantics=("arbitrary", "arbitrary"), vmem_limit_bytes=VMEM_LIMIT),
        name="post_mix",
    )(x, attn_n, z, ada, wglu_bf16, wout_bf16,
      g_ssm.reshape(1, ssm_w), g_post.reshape(1, d))


HALO = BF16_ROWS
FFN_SUB = 2 * LANES
FFN_ROWS = 64


def _ffn_kernel(x_ref, xh_ref, ada_ref, gpre_ref, wv_ref, wg_ref, cwv_ref, cwg_ref, cbv_ref, cbg_ref,
                wd_ref, wdt_ref, gpost_ref, o_ref, h_ref, upv_ref, upg_ref, act_ref, carry_ref, *, d, tm):
    i = pl.program_id(1)
    j = pl.program_id(2)
    ada = ada_ref[0]
    sub = FFN_SUB

    @pl.when(j == 0)
    def _():
        g = gpre_ref[...]
        scale = ada[:, 4 * d:5 * d]
        shift = ada[:, 3 * d:4 * d]
        hh = _modulated_norm(xh_ref[0], g, scale, shift)
        h_ref[0:HALO, :] = jnp.where(i > 0, hh, 0.0).astype(_BF16)
        gs = g * (1.0 + scale)
        for r in range(0, tm, NORM_ROWS):
            xb = x_ref[0, r:r + NORM_ROWS, :]
            h_ref[HALO + r:HALO + r + NORM_ROWS, :] = (_rms_scale(xb) * gs + shift).astype(_BF16)
        o_ref[...] = jnp.zeros_like(o_ref)
        carry_ref[...] = jnp.zeros_like(carry_ref)

    hv = h_ref[...]
    tiles = sub // LANES
    for s in range(2):
        cs = slice(s * sub, (s + 1) * sub)
        rv = jnp.dot(hv, wv_ref[:, cs], preferred_element_type=_F32)
        rg = jnp.dot(hv, wg_ref[:, cs], preferred_element_type=_F32)
        for c in range(tiles):
            upv_ref[s * tiles + c] = rv[:, c * LANES:(c + 1) * LANES]
            upg_ref[s * tiles + c] = rg[:, c * LANES:(c + 1) * LANES]

    def conv(up_ref, cw_ref, cb_ref, t, r):
        cs = slice(t * LANES, (t + 1) * LANES)
        out = cb_ref[:, cs].astype(_BF16)
        for k in range(CONV_WIDTH):
            off = HALO - (CONV_WIDTH - 1) + k + r
            out = out + (up_ref[t, pl.ds(off, FFN_ROWS, stride=1), :].astype(_BF16)
                         * cw_ref[k:k + 1, cs].astype(_BF16))
        return out

    def activation(dst, s, col0):
        for c in range(tiles):
            t = s * tiles + c
            for r in range(0, tm, FFN_ROWS):
                a = (_gelu_tanh(conv(upg_ref, cwg_ref, cbg_ref, t, r))
                     * conv(upv_ref, cwv_ref, cbv_ref, t, r))
                dst[r:r + FFN_ROWS, col0 + c * LANES:col0 + (c + 1) * LANES] = a.astype(_BF16)

    act_ref[:, 0:sub] = carry_ref[...]
    activation(act_ref, 0, sub)
    o_ref[0] += jnp.dot(act_ref[...], wd_ref[...], preferred_element_type=_F32)
    activation(carry_ref, 1, 0)

    @pl.when(j == pl.num_programs(2) - 1)
    def _():
        o_ref[0] += jnp.dot(carry_ref[...], wdt_ref[...], preferred_element_type=_F32)
        gg = (1.0 + ada[:, 5 * d:6 * d]) * gpost_ref[...]
        for r in range(0, tm, NORM_ROWS):
            rows = slice(r, r + NORM_ROWS)
            o_ref[0, rows, :] = x_ref[0, rows, :] + _rms_scale(o_ref[0, rows, :]) * gg


def _ffn(x, ada, g_pre, wup_bf16, conv_w, conv_b, wdown_bf16, g_post, tm):
    bsz, seq, d = x.shape
    d_ff = wdown_bf16.shape[0]
    tf = 2 * FFN_SUB
    nj = d_ff // tf
    assert nj * tf == d_ff
    halo_blocks = tm // HALO
    kern = functools.partial(_ffn_kernel, d=d, tm=tm)
    cb = conv_b.reshape(1, 2 * d_ff)
    wd = jnp.concatenate([jnp.zeros((FFN_SUB, d), _BF16), wdown_bf16], axis=0)
    return pl.pallas_call(
        kern,
        grid=(bsz, seq // tm, nj),
        in_specs=[
            pl.BlockSpec((1, tm, d), lambda b, i, j: (b, i, 0)),
            pl.BlockSpec((1, HALO, d), lambda b, i, j: (b, jnp.maximum(i * halo_blocks - 1, 0), 0)),
            pl.BlockSpec((1, 1, 6 * d), lambda b, i, j: (b, 0, 0)),
            pl.BlockSpec((1, d), lambda b, i, j: (0, 0)),
            pl.BlockSpec((d, tf), lambda b, i, j: (0, j)),
            pl.BlockSpec((d, tf), lambda b, i, j: (0, nj + j)),
            pl.BlockSpec((CONV_WIDTH, tf), lambda b, i, j: (0, j)),
            pl.BlockSpec((CONV_WIDTH, tf), lambda b, i, j: (0, nj + j)),
            pl.BlockSpec((1, tf), lambda b, i, j: (0, j)),
            pl.BlockSpec((1, tf), lambda b, i, j: (0, nj + j)),
            pl.BlockSpec((tf, d), lambda b, i, j: (j, 0)),
            _resident((FFN_SUB, d), lambda b, i, j: (d_ff // FFN_SUB, 0)),
            pl.BlockSpec((1, d), lambda b, i, j: (0, 0)),
        ],
        out_specs=pl.BlockSpec((1, tm, d), lambda b, i, j: (b, i, 0)),
        out_shape=jax.ShapeDtypeStruct((bsz, seq, d), _F32),
        scratch_shapes=[
            pltpu.VMEM((tm + HALO, d), _BF16),
            pltpu.VMEM((tf // LANES, tm + HALO, LANES), _F32),
            pltpu.VMEM((tf // LANES, tm + HALO, LANES), _F32),
            pltpu.VMEM((tm, tf), _BF16),
            pltpu.VMEM((tm, FFN_SUB), _BF16),
        ],
        compiler_params=pltpu.CompilerParams(
            dimension_semantics=("arbitrary", "arbitrary", "arbitrary"),
            vmem_limit_bytes=VMEM_LIMIT),
        name="conv_ffn",
    )(x, x, ada, g_pre.reshape(1, d), wup_bf16, wup_bf16, conv_w, conv_w, cb, cb,
      wd, wd, g_post.reshape(1, d))


def _largest_tile(n, cap, quantum):
    best = quantum
    for t in range(quantum, min(n, cap) + 1, quantum):
        if n % t == 0:
            best = t
    return best


def kernel(x, c, w_ada, b_ada, g_pre_mix, g_post_mix, w_in, attn_sinks, lam_re, lam_im, log_step,
           ssm_b_re, ssm_b_im, ssm_c_re, ssm_c_im, ssm_d, w_glu, g_attn_out, g_ssm_out, w_out,
           g_pre_ffn, g_post_ffn, w_up, conv_w, conv_b, w_down):
    bsz, seq, d = x.shape
    depth = w_in.shape[0]
    ssm_w = w_glu.shape[1]
    attn_w = w_out.shape[1] - ssm_w
    kv_w = w_in.shape[2] - attn_w - ssm_w
    d_ff = w_down.shape[1]
    assert bsz == SUBLANES, "the scan keeps one batch row per f32 sublane"

    tm = _largest_tile(seq, 512, WINDOW)
    tc = _largest_tile(seq, 128, SUBLANES)
    slab = 2 * LANES
    qb = _largest_tile(seq, 4 * WINDOW, WINDOW) // WINDOW

    ada_all = _adaln(c, w_ada, b_ada)
    for l in range(depth):
        ada = ada_all[l][:, None, :]
        q, kv, u = _in_proj(x, ada, g_pre_mix[l], w_in[l].astype(_BF16), attn_w, kv_w, tm)
        attn_n = _attention(q, kv, attn_sinks[l], g_attn_out[l], qb)
        wb, cw, a_re, a_im = _ssm_params(lam_re[l], lam_im[l], log_step[l], ssm_b_re[l], ssm_b_im[l],
                                         ssm_c_re[l], ssm_c_im[l], slab)
        z = _ssm(u, wb, cw, a_re, a_im, ssm_d[l].reshape(ssm_w), tc)
        x = _post_mix(x, attn_n, z, ada, w_glu[l].astype(_BF16), w_out[l].astype(_BF16),
                      g_ssm_out[l], g_post_mix[l], tm)
        x = _ffn(x, ada, g_pre_ffn[l], w_up[l].astype(_BF16), conv_w[l], conv_b[l],
                 w_down[l].astype(_BF16), g_post_ffn[l], tm)
    return x
```

```python
import functools
import math

import jax
import jax.numpy as jnp
from jax import lax
from jax.experimental import pallas as pl
from jax.experimental.pallas import tpu as pltpu

HEAD_DIM = 64
KV_RATIO = 8
WINDOW = 128
SSM_GROUP = 16
S5_SUB = 64
STATE = 64
CONV_WIDTH = 3
EPS = 1e-6
NEG = -1e30
LOG2E = math.log2(math.e)

LANES = 128
SUBLANES = 8
BF16_ROWS = 16
NORM_ROWS = BF16_ROWS
VMEM_LIMIT = 56 * 1024 * 1024

_BF16 = jnp.bfloat16
_F32 = jnp.float32


def _gelu_tanh_x2(x):
    c = math.sqrt(2.0 / math.pi)
    return x + x * jnp.tanh(x * (c + (0.044715 * c) * (x * x)))


def _gelu_tanh(x):
    return 0.5 * _gelu_tanh_x2(x)


def _sigmoid(x):
    return 1.0 / (1.0 + jnp.exp(-x))


def _rms_scale(x):
    return x * lax.rsqrt(jnp.mean(x * x, axis=-1, keepdims=True) + EPS)


def _resident(shape, index_map):
    return pl.BlockSpec(shape, index_map, pipeline_mode=pl.Buffered(1))


def _adaln_kernel(c_ref, w_ref, b_ref, o_ref):
    c = c_ref[...]
    ca = c * _sigmoid(c)
    o_ref[0] = jnp.dot(ca, w_ref[0], preferred_element_type=_F32,
                       precision=lax.Precision.HIGHEST) + b_ref[0]


def _adaln(c, w_ada, b_ada, tn=1024):
    depth, d, n = w_ada.shape
    bsz = c.shape[0]
    return pl.pallas_call(
        _adaln_kernel,
        grid=(depth, n // tn),
        in_specs=[
            pl.BlockSpec((bsz, d), lambda l, j: (0, 0)),
            pl.BlockSpec((1, d, tn), lambda l, j: (l, 0, j)),
            pl.BlockSpec((1, 1, tn), lambda l, j: (l, 0, j)),
        ],
        out_specs=pl.BlockSpec((1, bsz, tn), lambda l, j: (l, 0, j)),
        out_shape=jax.ShapeDtypeStruct((depth, bsz, n), _F32),
        compiler_params=pltpu.CompilerParams(
            dimension_semantics=("arbitrary", "arbitrary"), vmem_limit_bytes=VMEM_LIMIT),
        name="adaln",
    )(c, w_ada, b_ada.reshape(depth, 1, n))


def _modulated_norm(xv, g, scale, shift):
    return _rms_scale(xv) * (g * (1.0 + scale)) + shift


def _in_proj_kernel(x_ref, ada_ref, g_ref, w_ref, q_ref, kv_ref, u_ref, *, d, attn_w, kv_w):
    ada = ada_ref[0]
    h = _modulated_norm(x_ref[0], g_ref[...], ada[:, d:2 * d], ada[:, 0:d])
    p = jnp.dot(h.astype(_BF16), w_ref[...], preferred_element_type=_F32)
    q_ref[0] = (p[:, :attn_w] * (HEAD_DIM ** -0.5 * LOG2E)).astype(_BF16)
    kv_ref[0] = p[:, attn_w:attn_w + kv_w].astype(_BF16)
    u_ref[0] = p[:, attn_w + kv_w:].astype(_BF16)


def _in_proj(x, ada, g, w_bf16, attn_w, kv_w, tm):
    bsz, seq, d = x.shape
    ncols = w_bf16.shape[1]
    ssm_w = ncols - attn_w - kv_w
    kern = functools.partial(_in_proj_kernel, d=d, attn_w=attn_w, kv_w=kv_w)
    return pl.pallas_call(
        kern,
        grid=(bsz, seq // tm),
        in_specs=[
            pl.BlockSpec((1, tm, d), lambda b, i: (b, i, 0)),
            pl.BlockSpec((1, 1, 6 * d), lambda b, i: (b, 0, 0)),
            pl.BlockSpec((1, d), lambda b, i: (0, 0)),
            _resident((d, ncols), lambda b, i: (0, 0)),
        ],
        out_specs=[
            pl.BlockSpec((1, tm, attn_w), lambda b, i: (b, i, 0)),
            pl.BlockSpec((1, tm, kv_w), lambda b, i: (b, i, 0)),
            pl.BlockSpec((1, tm, ssm_w), lambda b, i: (b, i, 0)),
        ],
        out_shape=[
            jax.ShapeDtypeStruct((bsz, seq, attn_w), _BF16),
            jax.ShapeDtypeStruct((bsz, seq, kv_w), _BF16),
            jax.ShapeDtypeStruct((bsz, seq, ssm_w), _BF16),
        ],
        compiler_params=pltpu.CompilerParams(
            dimension_semantics=("arbitrary", "arbitrary"), vmem_limit_bytes=VMEM_LIMIT),
        name="in_proj",
    )(x, ada, g.reshape(1, d), w_bf16)


def _attention_kernel(sink_ref, q_ref, kvc_ref, kvp_ref, g_ref, o_ref, *, n_kv, qb):
    step = pl.program_id(1)
    w = WINDOW
    rows = (qb + 1) * w
    kv = jnp.concatenate([kvp_ref[0], kvc_ref[0]], axis=0).astype(_F32)
    kw = n_kv * HEAD_DIM
    k2 = kv[:, :kw]
    v2 = kv[:, kw:]
    k2r = pltpu.roll(k2, HEAD_DIM, 1)
    v2r = pltpu.roll(v2, HEAD_DIM, 1)
    lo = lax.broadcasted_iota(jnp.int32, (rows, LANES), 1) < HEAD_DIM
    one_lo = jnp.where(lo, 1.0, 0.0).astype(_BF16)
    one_hi = jnp.where(lo, 0.0, 1.0).astype(_BF16)
    kk = (jnp.where(lo, k2, k2r).astype(_BF16), jnp.where(lo, k2r, k2).astype(_BF16))
    top = (jnp.concatenate([jnp.where(lo, v2, 0.0).astype(_BF16), one_lo], axis=1),
           jnp.concatenate([jnp.where(lo, v2r, 0.0).astype(_BF16), one_lo], axis=1))
    bot = (jnp.concatenate([jnp.where(lo, 0.0, v2r).astype(_BF16), one_hi], axis=1),
           jnp.concatenate([jnp.where(lo, 0.0, v2).astype(_BF16), one_hi], axis=1))

    qi = lax.broadcasted_iota(jnp.int32, (w, 2 * w), 0)
    kj = lax.broadcasted_iota(jnp.int32, (w, 2 * w), 1)
    in_band = (kj > qi) & (kj <= qi + w)
    qlane_lo = lax.broadcasted_iota(jnp.int32, (w, LANES), 1) < HEAD_DIM
    pairs_per_kv = KV_RATIO // 2
    n_pairs = n_kv * pairs_per_kv

    def scores(blk):
        out = []
        for pair in range(n_pairs):
            j = pair // pairs_per_kv
            q2 = q_ref[0, blk * w:(blk + 1) * w, pair * LANES:(pair + 1) * LANES]
            zero = jnp.zeros_like(q2)
            kkj = kk[j][blk * w:(blk + 2) * w]
            dn = (((1,), (1,)), ((), ()))
            out.append((lax.dot_general(jnp.where(qlane_lo, q2, zero), kkj, dn, preferred_element_type=_F32),
                        lax.dot_general(jnp.where(qlane_lo, zero, q2), kkj, dn, preferred_element_type=_F32)))
        return out

    def finish(blk, s_blk):
        valid = in_band if blk > 0 else in_band & ((kj >= w) | (step > 0))
        es, sk = [], []
        for pair in range(n_pairs):
            e2, k2_ = [], []
            for half in range(2):
                sink = sink_ref[2 * pair + half]
                s = jnp.where(valid, s_blk[pair][half], NEG)
                m = jnp.maximum(jnp.max(s, axis=-1, keepdims=True), sink)
                e2.append(jnp.exp2(s - m).astype(_BF16))
                k2_.append(jnp.exp2(sink - m))
            es.append(jnp.concatenate(e2, axis=1))
            sk.append(jnp.where(qlane_lo, k2_[0], k2_[1]))
        outs = []
        for pair in range(n_pairs):
            j = pair // pairs_per_kv
            r = jnp.concatenate([top[j][blk * w:(blk + 2) * w], bot[j][blk * w:(blk + 2) * w]], axis=0)
            ox = jnp.dot(es[pair], r, preferred_element_type=_F32)
            outs.append(ox[:, :LANES] / (ox[:, LANES:] + sk[pair]))
        o = jnp.concatenate(outs, axis=1)
        o_ref[0, blk * w:(blk + 1) * w, :] = (_rms_scale(o) * g_ref[...]).astype(_BF16)

    s_next = scores(0)
    for blk in range(qb):
        s_cur = s_next
        if blk + 1 < qb:
            s_next = scores(blk + 1)
        finish(blk, s_cur)


def _attention(q, kv, sinks, g, qb):
    bsz, seq, attn_w = q.shape
    kv_w = kv.shape[-1]
    n_kv = kv_w // (2 * HEAD_DIM)
    assert n_kv * HEAD_DIM == LANES and attn_w == n_kv * KV_RATIO * HEAD_DIM
    tq = qb * WINDOW
    kern = functools.partial(_attention_kernel, n_kv=n_kv, qb=qb)
    return pl.pallas_call(
        kern,
        grid=(bsz, seq // tq),
        in_specs=[
            pl.BlockSpec(memory_space=pltpu.SMEM),
            pl.BlockSpec((1, tq, attn_w), lambda b, n: (b, n, 0)),
            pl.BlockSpec((1, tq, kv_w), lambda b, n: (b, n, 0)),
            pl.BlockSpec((1, WINDOW, kv_w), lambda b, n: (b, jnp.maximum(n * qb - 1, 0), 0)),
            pl.BlockSpec((1, attn_w), lambda b, n: (0, 0)),
        ],
        out_specs=pl.BlockSpec((1, tq, attn_w), lambda b, n: (b, n, 0)),
        out_shape=jax.ShapeDtypeStruct((bsz, seq, attn_w), _BF16),
        compiler_params=pltpu.CompilerParams(
            dimension_semantics=("arbitrary", "arbitrary"), vmem_limit_bytes=VMEM_LIMIT),
        name="swa",
    )(sinks * LOG2E, q, kv, kv, g.reshape(1, attn_w))


def _ssm_kernel(u_ref, wb_ref, cw_ref, ar_ref, ai_ref, d_ref, z_ref, xs_ref, st_ref, il_ref,
                *, bsz, tc, ns):
    @pl.when(pl.program_id(1) == 0)
    def _():
        st_ref[...] = jnp.zeros_like(st_ref)

    tiles = il_ref.shape[0]
    for b in range(bsz):
        ub = u_ref[b].astype(_F32)
        for c in range(tiles):
            il_ref[c, pl.ds(b, tc, stride=bsz), :] = ub[:, c * LANES:(c + 1) * LANES]
    ar = jnp.broadcast_to(ar_ref[0], (bsz, ns))
    ai = jnp.broadcast_to(ai_ref[0], (bsz, ns))
    xr = st_ref[:, 0:ns]
    xi = st_ref[:, ns:2 * ns]
    sr = S5_SUB * bsz
    n_sub = tc // S5_SUB
    u32 = []
    for k in range(n_sub):
        rows = slice(k * sr, (k + 1) * sr)
        uk = jnp.concatenate([il_ref[c, rows, :] for c in range(tiles)], axis=1)
        u32.append(uk)
        xs_ref[rows, :] = jnp.dot(uk.astype(_BF16), wb_ref[0], preferred_element_type=_F32)
    ys = []
    for k in range(n_sub):
        for t in range(k * S5_SUB, (k + 1) * S5_SUB):
            r = slice(t * bsz, (t + 1) * bsz)
            nxr = ar * xr - ai * xi + xs_ref[r, 0:ns]
            nxi = ar * xi + ai * xr + xs_ref[r, ns:2 * ns]
            xs_ref[r, 0:ns] = nxr
            xs_ref[r, ns:2 * ns] = nxi
            xr, xi = nxr, nxi
        rows = slice(k * sr, (k + 1) * sr)
        ys.append(jnp.dot(xs_ref[rows, 0:ns].astype(_BF16), cw_ref[0, 0:ns, :], preferred_element_type=_F32)
                  + jnp.dot(xs_ref[rows, ns:2 * ns].astype(_BF16), cw_ref[0, ns:2 * ns, :],
                            preferred_element_type=_F32)
                  + d_ref[...] * u32[k])
    st_ref[:, 0:ns] = xr
    st_ref[:, ns:2 * ns] = xi
    y = jnp.concatenate(ys, axis=0)
    z = _gelu_tanh(y)
    for c in range(tiles):
        il_ref[c] = z[:, c * LANES:(c + 1) * LANES]
    for b in range(bsz):
        zb = jnp.concatenate([il_ref[c, pl.ds(b, tc, stride=bsz), :] for c in range(tiles)], axis=1)
        z_ref[b] = zb.astype(_BF16)


def _ssm_params(lam_re, lam_im, log_step, b_re, b_im, c_re, c_im, slab):
    g, p = lam_re.shape
    h = SSM_GROUP
    gs = slab // h
    n_slabs = g // gs
    dt = jnp.exp(log_step)[:, None]
    mag = jnp.exp(lam_re * dt)
    ang = lam_im * dt
    ab_re = mag * jnp.cos(ang)
    ab_im = mag * jnp.sin(ang)
    den = lam_re * lam_re + lam_im * lam_im
    f_re = ((ab_re - 1.0) * lam_re + ab_im * lam_im) / den
    f_im = (ab_im * lam_re - (ab_re - 1.0) * lam_im) / den
    bb_re = f_re[..., None] * b_re - f_im[..., None] * b_im
    bb_im = f_re[..., None] * b_im + f_im[..., None] * b_re
    eye = jnp.eye(gs, dtype=_F32)

    def block_diag_in(bb):
        t = bb.reshape(n_slabs, gs, p, h)
        return jnp.einsum('sgph,gk->sghkp', t, eye).reshape(n_slabs, gs * h, gs * p)

    def block_diag_out(cc):
        t = cc.reshape(n_slabs, gs, h, p)
        return jnp.einsum('sghp,gk->sgpkh', t, eye).reshape(n_slabs, gs * p, gs * h)

    wb = jnp.concatenate([block_diag_in(bb_re), block_diag_in(bb_im)], axis=2).astype(_BF16)
    cw = jnp.concatenate([block_diag_out(c_re), block_diag_out(-c_im)], axis=1).astype(_BF16)
    a_re = ab_re.reshape(n_slabs, 1, gs * p)
    a_im = ab_im.reshape(n_slabs, 1, gs * p)
    return wb, cw, a_re, a_im


def _ssm(u, wb, cw, a_re, a_im, d_skip, tc):
    bsz, seq, width = u.shape
    n_slabs, slab, ns2 = wb.shape
    ns = ns2 // 2
    kern = functools.partial(_ssm_kernel, bsz=bsz, tc=tc, ns=ns)
    return pl.pallas_call(
        kern,
        grid=(n_slabs, seq // tc),
        in_specs=[
            pl.BlockSpec((bsz, tc, slab), lambda s, t: (0, t, s)),
            pl.BlockSpec((1, slab, ns2), lambda s, t: (s, 0, 0)),
            pl.BlockSpec((1, ns2, slab), lambda s, t: (s, 0, 0)),
            pl.BlockSpec((1, 1, ns), lambda s, t: (s, 0, 0)),
            pl.BlockSpec((1, 1, ns), lambda s, t: (s, 0, 0)),
            pl.BlockSpec((1, slab), lambda s, t: (0, s)),
        ],
        out_specs=pl.BlockSpec((bsz, tc, slab), lambda s, t: (0, t, s)),
        out_shape=jax.ShapeDtypeStruct((bsz, seq, width), _BF16),
        scratch_shapes=[
            pltpu.VMEM((tc * bsz, ns2), _F32),
            pltpu.VMEM((bsz, ns2), _F32),
            pltpu.VMEM((slab // LANES, tc * bsz, LANES), _F32),
        ],
        compiler_params=pltpu.CompilerParams(
            dimension_semantics=("arbitrary", "arbitrary"), vmem_limit_bytes=VMEM_LIMIT),
        name="s5",
    )(u, wb, cw, a_re, a_im, d_skip.reshape(1, width))


def _post_mix_kernel(x_ref, attn_ref, z_ref, ada_ref, wglu_ref, wout_ref, gssm_ref, gpost_ref,
                     o_ref, *, d, attn_w):
    z = z_ref[0]
    zf = z.astype(_F32)
    ssm = zf * _sigmoid(jnp.dot(z, wglu_ref[...], preferred_element_type=_F32))
    ssm_n = (_rms_scale(ssm) * gssm_ref[...]).astype(_BF16)
    o_ref[0] = (jnp.dot(attn_ref[0], wout_ref[0:attn_w, :], preferred_element_type=_F32)
                + jnp.dot(ssm_n, wout_ref[attn_w:, :], preferred_element_type=_F32))
    gg = (1.0 + ada_ref[0][:, 2 * d:3 * d]) * gpost_ref[...]
    for r in range(0, o_ref.shape[1], NORM_ROWS):
        rows = slice(r, r + NORM_ROWS)
        o_ref[0, rows, :] = x_ref[0, rows, :] + _rms_scale(o_ref[0, rows, :]) * gg


def _post_mix(x, attn_n, z, ada, wglu_bf16, wout_bf16, g_ssm, g_post, tm):
    bsz, seq, d = x.shape
    attn_w = attn_n.shape[-1]
    ssm_w = wglu_bf16.shape[0]
    kern = functools.partial(_post_mix_kernel, d=d, attn_w=attn_w)
    return pl.pallas_call(
        kern,
        grid=(bsz, seq // tm),
        in_specs=[
            pl.BlockSpec((1, tm, d), lambda b, i: (b, i, 0)),
            pl.BlockSpec((1, tm, attn_w), lambda b, i: (b, i, 0)),
            pl.BlockSpec((1, tm, ssm_w), lambda b, i: (b, i, 0)),
            pl.BlockSpec((1, 1, 6 * d), lambda b, i: (b, 0, 0)),
            _resident((ssm_w, ssm_w), lambda b, i: (0, 0)),
            _resident((attn_w + ssm_w, d), lambda b, i: (0, 0)),
            pl.BlockSpec((1, ssm_w), lambda b, i: (0, 0)),
            pl.BlockSpec((1, d), lambda b, i: (0, 0)),
        ],
        out_specs=pl.BlockSpec((1, tm, d), lambda b, i: (b, i, 0)),
        out_shape=jax.ShapeDtypeStruct((bsz, seq, d), _F32),
        compiler_params=pltpu.CompilerParams(
            dimension_semantics=("arbitrary", "arbitrary"), vmem_limit_bytes=VMEM_LIMIT),
        name="post_mix",
    )(x, attn_n, z, ada, wglu_bf16, wout_bf16,
      g_ssm.reshape(1, ssm_w), g_post.reshape(1, d))


HALO = BF16_ROWS
FFN_SUB = 2 * LANES
FFN_ROWS = 64


def _ffn_kernel(x_ref, xh_ref, ada_ref, gpre_ref, wv_ref, wg_ref, cwv_ref, cwg_ref, cbv_ref, cbg_ref,
                wd_ref, wdt_ref, gpost_ref, o_ref, h_ref, upv_ref, upg_ref, act_ref, carry_ref, *, d, tm):
    i = pl.program_id(1)
    j = pl.program_id(2)
    ada = ada_ref[0]
    sub = FFN_SUB

    @pl.when(j == 0)
    def _():
        g = gpre_ref[...]
        scale = ada[:, 4 * d:5 * d]
        shift = ada[:, 3 * d:4 * d]
        hh = _modulated_norm(xh_ref[0], g, scale, shift)
        h_ref[0:HALO, :] = jnp.where(i > 0, hh, 0.0).astype(_BF16)
        gs = g * (1.0 + scale)
        for r in range(0, tm, NORM_ROWS):
            xb = x_ref[0, r:r + NORM_ROWS, :]
            h_ref[HALO + r:HALO + r + NORM_ROWS, :] = (_rms_scale(xb) * gs + shift).astype(_BF16)
        o_ref[...] = jnp.zeros_like(o_ref)
        carry_ref[...] = jnp.zeros_like(carry_ref)

    hv = h_ref[...]
    tiles = sub // LANES
    for s in range(2):
        cs = slice(s * sub, (s + 1) * sub)
        rv = jnp.dot(hv, wv_ref[:, cs], preferred_element_type=_F32)
        rg = jnp.dot(hv, wg_ref[:, cs], preferred_element_type=_F32)
        for c in range(tiles):
            upv_ref[s * tiles + c] = rv[:, c * LANES:(c + 1) * LANES]
            upg_ref[s * tiles + c] = rg[:, c * LANES:(c + 1) * LANES]

    def conv(up_ref, cw_ref, cb_ref, t, r):
        cs = slice(t * LANES, (t + 1) * LANES)
        out = cb_ref[:, cs].astype(_BF16)
        for k in range(CONV_WIDTH):
            off = HALO - (CONV_WIDTH - 1) + k + r
            out = out + (up_ref[t, pl.ds(off, FFN_ROWS, stride=1), :].astype(_BF16)
                         * cw_ref[k:k + 1, cs].astype(_BF16))
        return out

    def activation(dst, s, col0):
        for c in range(tiles):
            t = s * tiles + c
            for r in range(0, tm, FFN_ROWS):
                a = (_gelu_tanh_x2(conv(upg_ref, cwg_ref, cbg_ref, t, r))
                     * conv(upv_ref, cwv_ref, cbv_ref, t, r))
                dst[r:r + FFN_ROWS, col0 + c * LANES:col0 + (c + 1) * LANES] = a.astype(_BF16)

    act_ref[:, 0:sub] = carry_ref[...]
    activation(act_ref, 0, sub)
    o_ref[0] += jnp.dot(act_ref[...], wd_ref[...], preferred_element_type=_F32)
    activation(carry_ref, 1, 0)

    @pl.when(j == pl.num_programs(2) - 1)
    def _():
        o_ref[0] += jnp.dot(carry_ref[...], wdt_ref[...], preferred_element_type=_F32)
        gg = (1.0 + ada[:, 5 * d:6 * d]) * gpost_ref[...]
        for r in range(0, tm, NORM_ROWS):
            rows = slice(r, r + NORM_ROWS)
            o_ref[0, rows, :] = x_ref[0, rows, :] + _rms_scale(o_ref[0, rows, :]) * gg


def _ffn(x, ada, g_pre, wup_bf16, conv_w, conv_b, wdown_bf16, g_post, tm):
    bsz, seq, d = x.shape
    d_ff = wdown_bf16.shape[0]
    tf = 2 * FFN_SUB
    nj = d_ff // tf
    assert nj * tf == d_ff
    halo_blocks = tm // HALO
    kern = functools.partial(_ffn_kernel, d=d, tm=tm)
    half = jnp.concatenate([jnp.full((d_ff,), 0.5, _F32), jnp.ones((d_ff,), _F32)])
    cw = conv_w * half
    cb = (conv_b * half).reshape(1, 2 * d_ff)
    wd = jnp.concatenate([jnp.zeros((FFN_SUB, d), _BF16), wdown_bf16], axis=0)
    return pl.pallas_call(
        kern,
        grid=(bsz, seq // tm, nj),
        in_specs=[
            pl.BlockSpec((1, tm, d), lambda b, i, j: (b, i, 0)),
            pl.BlockSpec((1, HALO, d), lambda b, i, j: (b, jnp.maximum(i * halo_blocks - 1, 0), 0)),
            pl.BlockSpec((1, 1, 6 * d), lambda b, i, j: (b, 0, 0)),
            pl.BlockSpec((1, d), lambda b, i, j: (0, 0)),
            pl.BlockSpec((d, tf), lambda b, i, j: (0, j)),
            pl.BlockSpec((d, tf), lambda b, i, j: (0, nj + j)),
            pl.BlockSpec((CONV_WIDTH, tf), lambda b, i, j: (0, j)),
            pl.BlockSpec((CONV_WIDTH, tf), lambda b, i, j: (0, nj + j)),
            pl.BlockSpec((1, tf), lambda b, i, j: (0, j)),
            pl.BlockSpec((1, tf), lambda b, i, j: (0, nj + j)),
            pl.BlockSpec((tf, d), lambda b, i, j: (j, 0)),
            _resident((FFN_SUB, d), lambda b, i, j: (d_ff // FFN_SUB, 0)),
            pl.BlockSpec((1, d), lambda b, i, j: (0, 0)),
        ],
        out_specs=pl.BlockSpec((1, tm, d), lambda b, i, j: (b, i, 0)),
        out_shape=jax.ShapeDtypeStruct((bsz, seq, d), _F32),
        scratch_shapes=[
            pltpu.VMEM((tm + HALO, d), _BF16),
            pltpu.VMEM((tf // LANES, tm + HALO, LANES), _F32),
            pltpu.VMEM((tf // LANES, tm + HALO, LANES), _F32),
            pltpu.VMEM((tm, tf), _BF16),
            pltpu.VMEM((tm, FFN_SUB), _BF16),
        ],
        compiler_params=pltpu.CompilerParams(
            dimension_semantics=("arbitrary", "arbitrary", "arbitrary"),
            vmem_limit_bytes=VMEM_LIMIT),
        name="conv_ffn",
    )(x, x, ada, g_pre.reshape(1, d), wup_bf16, wup_bf16, cw, cw, cb, cb,
      wd, wd, g_post.reshape(1, d))


def _largest_tile(n, cap, quantum):
    best = quantum
    for t in range(quantum, min(n, cap) + 1, quantum):
        if n % t == 0:
            best = t
    return best


def kernel(x, c, w_ada, b_ada, g_pre_mix, g_post_mix, w_in, attn_sinks, lam_re, lam_im, log_step,
           ssm_b_re, ssm_b_im, ssm_c_re, ssm_c_im, ssm_d, w_glu, g_attn_out, g_ssm_out, w_out,
           g_pre_ffn, g_post_ffn, w_up, conv_w, conv_b, w_down):
    bsz, seq, d = x.shape
    depth = w_in.shape[0]
    ssm_w = w_glu.shape[1]
    attn_w = w_out.shape[1] - ssm_w
    kv_w = w_in.shape[2] - attn_w - ssm_w
    d_ff = w_down.shape[1]
    assert bsz == SUBLANES, "the scan keeps one batch row per f32 sublane"

    tm = _largest_tile(seq, 512, WINDOW)
    tc = _largest_tile(seq, 128, SUBLANES)
    slab = 2 * LANES
    qb = _largest_tile(seq, 4 * WINDOW, WINDOW) // WINDOW

    ada_all = _adaln(c, w_ada, b_ada)
    for l in range(depth):
        ada = ada_all[l][:, None, :]
        q, kv, u = _in_proj(x, ada, g_pre_mix[l], w_in[l].astype(_BF16), attn_w, kv_w, tm)
        attn_n = _attention(q, kv, attn_sinks[l], g_attn_out[l], qb)
        wb, cw, a_re, a_im = _ssm_params(lam_re[l], lam_im[l], log_step[l], ssm_b_re[l], ssm_b_im[l],
                                         ssm_c_re[l], ssm_c_im[l], slab)
        z = _ssm(u, wb, cw, a_re, a_im, ssm_d[l].reshape(ssm_w), tc)
        x = _post_mix(x, attn_n, z, ada, w_glu[l].astype(_BF16), w_out[l].astype(_BF16),
                      g_ssm_out[l], g_post_mix[l], tm)
        x = _ffn(x, ada, g_pre_ffn[l], w_up[l].astype(_BF16), conv_w[l], conv_b[l],
                 w_down[l].astype(_BF16), g_post_ffn[l], tm)
    return x
```

```python
import functools
import math

import jax
import jax.numpy as jnp
from jax import lax
from jax.experimental import pallas as pl
from jax.experimental.pallas import tpu as pltpu

HEAD_DIM = 64
KV_RATIO = 8
WINDOW = 128
SSM_GROUP = 16
S5_SUB = 64
STATE = 64
CONV_WIDTH = 3
EPS = 1e-6
NEG = -1e30
LOG2E = math.log2(math.e)

LANES = 128
SUBLANES = 8
BF16_ROWS = 16
NORM_ROWS = BF16_ROWS
VMEM_LIMIT = 56 * 1024 * 1024

_BF16 = jnp.bfloat16
_F32 = jnp.float32


def _gelu_tanh_x2(x):
    c = math.sqrt(2.0 / math.pi)
    return x + x * jnp.tanh(x * (c + (0.044715 * c) * (x * x)))


def _gelu_tanh(x):
    return 0.5 * _gelu_tanh_x2(x)


def _sigmoid(x):
    return 1.0 / (1.0 + jnp.exp(-x))


def _rms_scale(x):
    return x * lax.rsqrt(jnp.mean(x * x, axis=-1, keepdims=True) + EPS)


def _resident(shape, index_map):
    return pl.BlockSpec(shape, index_map, pipeline_mode=pl.Buffered(1))


def _adaln_kernel(c_ref, w_ref, b_ref, o_ref):
    c = c_ref[...]
    ca = c * _sigmoid(c)
    o_ref[0] = jnp.dot(ca.astype(_BF16), w_ref[0].astype(_BF16), preferred_element_type=_F32) + b_ref[0]


def _adaln(c, w_ada, b_ada, tn=1024):
    depth, d, n = w_ada.shape
    bsz = c.shape[0]
    return pl.pallas_call(
        _adaln_kernel,
        grid=(depth, n // tn),
        in_specs=[
            pl.BlockSpec((bsz, d), lambda l, j: (0, 0)),
            pl.BlockSpec((1, d, tn), lambda l, j: (l, 0, j)),
            pl.BlockSpec((1, 1, tn), lambda l, j: (l, 0, j)),
        ],
        out_specs=pl.BlockSpec((1, bsz, tn), lambda l, j: (l, 0, j)),
        out_shape=jax.ShapeDtypeStruct((depth, bsz, n), _F32),
        compiler_params=pltpu.CompilerParams(
            dimension_semantics=("arbitrary", "arbitrary"), vmem_limit_bytes=VMEM_LIMIT),
        name="adaln",
    )(c, w_ada, b_ada.reshape(depth, 1, n))


def _modulated_norm(xv, g, scale, shift):
    return _rms_scale(xv) * (g * (1.0 + scale)) + shift


def _in_proj_kernel(x_ref, ada_ref, g_ref, w_ref, q_ref, kv_ref, u_ref, *, d, attn_w, kv_w):
    ada = ada_ref[0]
    h = _modulated_norm(x_ref[0], g_ref[...], ada[:, d:2 * d], ada[:, 0:d])
    p = jnp.dot(h.astype(_BF16), w_ref[...], preferred_element_type=_F32)
    q_ref[0] = (p[:, :attn_w] * (HEAD_DIM ** -0.5 * LOG2E)).astype(_BF16)
    kv_ref[0] = p[:, attn_w:attn_w + kv_w].astype(_BF16)
    u_ref[0] = p[:, attn_w + kv_w:].astype(_BF16)


def _in_proj(x, ada, g, w_bf16, attn_w, kv_w, tm):
    bsz, seq, d = x.shape
    ncols = w_bf16.shape[1]
    ssm_w = ncols - attn_w - kv_w
    kern = functools.partial(_in_proj_kernel, d=d, attn_w=attn_w, kv_w=kv_w)
    return pl.pallas_call(
        kern,
        grid=(bsz, seq // tm),
        in_specs=[
            pl.BlockSpec((1, tm, d), lambda b, i: (b, i, 0)),
            pl.BlockSpec((1, 1, 6 * d), lambda b, i: (b, 0, 0)),
            pl.BlockSpec((1, d), lambda b, i: (0, 0)),
            _resident((d, ncols), lambda b, i: (0, 0)),
        ],
        out_specs=[
            pl.BlockSpec((1, tm, attn_w), lambda b, i: (b, i, 0)),
            pl.BlockSpec((1, tm, kv_w), lambda b, i: (b, i, 0)),
            pl.BlockSpec((1, tm, ssm_w), lambda b, i: (b, i, 0)),
        ],
        out_shape=[
            jax.ShapeDtypeStruct((bsz, seq, attn_w), _BF16),
            jax.ShapeDtypeStruct((bsz, seq, kv_w), _BF16),
            jax.ShapeDtypeStruct((bsz, seq, ssm_w), _BF16),
        ],
        compiler_params=pltpu.CompilerParams(
            dimension_semantics=("arbitrary", "arbitrary"), vmem_limit_bytes=VMEM_LIMIT),
        name="in_proj",
    )(x, ada, g.reshape(1, d), w_bf16)


def _attention_kernel(sink_ref, q_ref, kvc_ref, kvp_ref, g_ref, o_ref, *, n_kv, qb):
    step = pl.program_id(1)
    w = WINDOW
    rows = (qb + 1) * w
    kv = jnp.concatenate([kvp_ref[0], kvc_ref[0]], axis=0).astype(_F32)
    kw = n_kv * HEAD_DIM
    k2 = kv[:, :kw]
    v2 = kv[:, kw:]
    k2r = pltpu.roll(k2, HEAD_DIM, 1)
    v2r = pltpu.roll(v2, HEAD_DIM, 1)
    lo = lax.broadcasted_iota(jnp.int32, (rows, LANES), 1) < HEAD_DIM
    one_lo = jnp.where(lo, 1.0, 0.0).astype(_BF16)
    one_hi = jnp.where(lo, 0.0, 1.0).astype(_BF16)
    kk = (jnp.where(lo, k2, k2r).astype(_BF16), jnp.where(lo, k2r, k2).astype(_BF16))
    top = (jnp.concatenate([jnp.where(lo, v2, 0.0).astype(_BF16), one_lo], axis=1),
           jnp.concatenate([jnp.where(lo, v2r, 0.0).astype(_BF16), one_lo], axis=1))
    bot = (jnp.concatenate([jnp.where(lo, 0.0, v2r).astype(_BF16), one_hi], axis=1),
           jnp.concatenate([jnp.where(lo, 0.0, v2).astype(_BF16), one_hi], axis=1))

    qi = lax.broadcasted_iota(jnp.int32, (w, 2 * w), 0)
    kj = lax.broadcasted_iota(jnp.int32, (w, 2 * w), 1)
    in_band = (kj > qi) & (kj <= qi + w)
    qlane_lo = lax.broadcasted_iota(jnp.int32, (w, LANES), 1) < HEAD_DIM
    pairs_per_kv = KV_RATIO // 2
    n_pairs = n_kv * pairs_per_kv

    def scores(blk):
        out = []
        for pair in range(n_pairs):
            j = pair // pairs_per_kv
            q2 = q_ref[0, blk * w:(blk + 1) * w, pair * LANES:(pair + 1) * LANES]
            zero = jnp.zeros_like(q2)
            kkj = kk[j][blk * w:(blk + 2) * w]
            dn = (((1,), (1,)), ((), ()))
            out.append((lax.dot_general(jnp.where(qlane_lo, q2, zero), kkj, dn, preferred_element_type=_F32),
                        lax.dot_general(jnp.where(qlane_lo, zero, q2), kkj, dn, preferred_element_type=_F32)))
        return out

    def finish(blk, s_blk):
        valid = in_band if blk > 0 else in_band & ((kj >= w) | (step > 0))
        es, sk = [], []
        for pair in range(n_pairs):
            e2, k2_ = [], []
            for half in range(2):
                sink = sink_ref[2 * pair + half]
                s = jnp.where(valid, s_blk[pair][half], NEG)
                m = jnp.maximum(jnp.max(s, axis=-1, keepdims=True), sink)
                e2.append(jnp.exp2(s - m).astype(_BF16))
                k2_.append(jnp.exp2(sink - m))
            es.append(jnp.concatenate(e2, axis=1))
            sk.append(jnp.where(qlane_lo, k2_[0], k2_[1]))
        outs = []
        for pair in range(n_pairs):
            j = pair // pairs_per_kv
            r = jnp.concatenate([top[j][blk * w:(blk + 2) * w], bot[j][blk * w:(blk + 2) * w]], axis=0)
            ox = jnp.dot(es[pair], r, preferred_element_type=_F32)
            outs.append(ox[:, :LANES] / (ox[:, LANES:] + sk[pair]))
        o = jnp.concatenate(outs, axis=1)
        o_ref[0, blk * w:(blk + 1) * w, :] = (_rms_scale(o) * g_ref[...]).astype(_BF16)

    s_next = scores(0)
    for blk in range(qb):
        s_cur = s_next
        if blk + 1 < qb:
            s_next = scores(blk + 1)
        finish(blk, s_cur)


def _attention(q, kv, sinks, g, qb):
    bsz, seq, attn_w = q.shape
    kv_w = kv.shape[-1]
    n_kv = kv_w // (2 * HEAD_DIM)
    assert n_kv * HEAD_DIM == LANES and attn_w == n_kv * KV_RATIO * HEAD_DIM
    tq = qb * WINDOW
    kern = functools.partial(_attention_kernel, n_kv=n_kv, qb=qb)
    return pl.pallas_call(
        kern,
        grid=(bsz, seq // tq),
        in_specs=[
            pl.BlockSpec(memory_space=pltpu.SMEM),
            pl.BlockSpec((1, tq, attn_w), lambda b, n: (b, n, 0)),
            pl.BlockSpec((1, tq, kv_w), lambda b, n: (b, n, 0)),
            pl.BlockSpec((1, WINDOW, kv_w), lambda b, n: (b, jnp.maximum(n * qb - 1, 0), 0)),
            pl.BlockSpec((1, attn_w), lambda b, n: (0, 0)),
        ],
        out_specs=pl.BlockSpec((1, tq, attn_w), lambda b, n: (b, n, 0)),
        out_shape=jax.ShapeDtypeStruct((bsz, seq, attn_w), _BF16),
        compiler_params=pltpu.CompilerParams(
            dimension_semantics=("arbitrary", "arbitrary"), vmem_limit_bytes=VMEM_LIMIT),
        name="swa",
    )(sinks * LOG2E, q, kv, kv, g.reshape(1, attn_w))


def _ssm_kernel(u_ref, wb_ref, cw_ref, ar_ref, ai_ref, d_ref, z_ref, xs_ref, st_ref, il_ref,
                *, bsz, tc, ns):
    @pl.when(pl.program_id(1) == 0)
    def _():
        st_ref[...] = jnp.zeros_like(st_ref)

    tiles = il_ref.shape[0]
    for b in range(bsz):
        ub = u_ref[b].astype(_F32)
        for c in range(tiles):
            il_ref[c, pl.ds(b, tc, stride=bsz), :] = ub[:, c * LANES:(c + 1) * LANES]
    ar = jnp.broadcast_to(ar_ref[0], (bsz, ns))
    ai = jnp.broadcast_to(ai_ref[0], (bsz, ns))
    xr = st_ref[:, 0:ns]
    xi = st_ref[:, ns:2 * ns]
    sr = S5_SUB * bsz
    n_sub = tc // S5_SUB
    u32 = []
    for k in range(n_sub):
        rows = slice(k * sr, (k + 1) * sr)
        uk = jnp.concatenate([il_ref[c, rows, :] for c in range(tiles)], axis=1)
        u32.append(uk)
        xs_ref[rows, :] = jnp.dot(uk.astype(_BF16), wb_ref[0], preferred_element_type=_F32)
    ys = []
    for k in range(n_sub):
        for t in range(k * S5_SUB, (k + 1) * S5_SUB):
            r = slice(t * bsz, (t + 1) * bsz)
            nxr = ar * xr - ai * xi + xs_ref[r, 0:ns]
            nxi = ar * xi + ai * xr + xs_ref[r, ns:2 * ns]
            xs_ref[r, 0:ns] = nxr
            xs_ref[r, ns:2 * ns] = nxi
            xr, xi = nxr, nxi
        rows = slice(k * sr, (k + 1) * sr)
        ys.append(jnp.dot(xs_ref[rows, 0:ns].astype(_BF16), cw_ref[0, 0:ns, :], preferred_element_type=_F32)
                  + jnp.dot(xs_ref[rows, ns:2 * ns].astype(_BF16), cw_ref[0, ns:2 * ns, :],
                            preferred_element_type=_F32)
                  + d_ref[...] * u32[k])
    st_ref[:, 0:ns] = xr
    st_ref[:, ns:2 * ns] = xi
    y = jnp.concatenate(ys, axis=0)
    z = _gelu_tanh(y)
    for c in range(tiles):
        il_ref[c] = z[:, c * LANES:(c + 1) * LANES]
    for b in range(bsz):
        zb = jnp.concatenate([il_ref[c, pl.ds(b, tc, stride=bsz), :] for c in range(tiles)], axis=1)
        z_ref[b] = zb.astype(_BF16)


def _ssm_params(lam_re, lam_im, log_step, b_re, b_im, c_re, c_im, slab):
    g, p = lam_re.shape
    h = SSM_GROUP
    gs = slab // h
    n_slabs = g // gs
    dt = jnp.exp(log_step)[:, None]
    mag = jnp.exp(lam_re * dt)
    ang = lam_im * dt
    ab_re = mag * jnp.cos(ang)
    ab_im = mag * jnp.sin(ang)
    den = lam_re * lam_re + lam_im * lam_im
    f_re = ((ab_re - 1.0) * lam_re + ab_im * lam_im) / den
    f_im = (ab_im * lam_re - (ab_re - 1.0) * lam_im) / den
    bb_re = f_re[..., None] * b_re - f_im[..., None] * b_im
    bb_im = f_re[..., None] * b_im + f_im[..., None] * b_re
    eye = jnp.eye(gs, dtype=_F32)

    def block_diag_in(bb):
        t = bb.reshape(n_slabs, gs, p, h)
        return jnp.einsum('sgph,gk->sghkp', t, eye).reshape(n_slabs, gs * h, gs * p)

    def block_diag_out(cc):
        t = cc.reshape(n_slabs, gs, h, p)
        return jnp.einsum('sghp,gk->sgpkh', t, eye).reshape(n_slabs, gs * p, gs * h)

    wb = jnp.concatenate([block_diag_in(bb_re), block_diag_in(bb_im)], axis=2).astype(_BF16)
    cw = jnp.concatenate([block_diag_out(c_re), block_diag_out(-c_im)], axis=1).astype(_BF16)
    a_re = ab_re.reshape(n_slabs, 1, gs * p)
    a_im = ab_im.reshape(n_slabs, 1, gs * p)
    return wb, cw, a_re, a_im


def _ssm(u, wb, cw, a_re, a_im, d_skip, tc):
    bsz, seq, width = u.shape
    n_slabs, slab, ns2 = wb.shape
    ns = ns2 // 2
    kern = functools.partial(_ssm_kernel, bsz=bsz, tc=tc, ns=ns)
    return pl.pallas_call(
        kern,
        grid=(n_slabs, seq // tc),
        in_specs=[
            pl.BlockSpec((bsz, tc, slab), lambda s, t: (0, t, s)),
            pl.BlockSpec((1, slab, ns2), lambda s, t: (s, 0, 0)),
            pl.BlockSpec((1, ns2, slab), lambda s, t: (s, 0, 0)),
            pl.BlockSpec((1, 1, ns), lambda s, t: (s, 0, 0)),
            pl.BlockSpec((1, 1, ns), lambda s, t: (s, 0, 0)),
            pl.BlockSpec((1, slab), lambda s, t: (0, s)),
        ],
        out_specs=pl.BlockSpec((bsz, tc, slab), lambda s, t: (0, t, s)),
        out_shape=jax.ShapeDtypeStruct((bsz, seq, width), _BF16),
        scratch_shapes=[
            pltpu.VMEM((tc * bsz, ns2), _F32),
            pltpu.VMEM((bsz, ns2), _F32),
            pltpu.VMEM((slab // LANES, tc * bsz, LANES), _F32),
        ],
        compiler_params=pltpu.CompilerParams(
            dimension_semantics=("arbitrary", "arbitrary"), vmem_limit_bytes=VMEM_LIMIT),
        name="s5",
    )(u, wb, cw, a_re, a_im, d_skip.reshape(1, width))


def _post_mix_kernel(x_ref, attn_ref, z_ref, ada_ref, wglu_ref, wout_ref, gssm_ref, gpost_ref,
                     o_ref, *, d, attn_w):
    z = z_ref[0]
    zf = z.astype(_F32)
    ssm = zf * _sigmoid(jnp.dot(z, wglu_ref[...], preferred_element_type=_F32))
    ssm_n = (_rms_scale(ssm) * gssm_ref[...]).astype(_BF16)
    o_ref[0] = (jnp.dot(attn_ref[0], wout_ref[0:attn_w, :], preferred_element_type=_F32)
                + jnp.dot(ssm_n, wout_ref[attn_w:, :], preferred_element_type=_F32))
    gg = (1.0 + ada_ref[0][:, 2 * d:3 * d]) * gpost_ref[...]
    for r in range(0, o_ref.shape[1], NORM_ROWS):
        rows = slice(r, r + NORM_ROWS)
        o_ref[0, rows, :] = x_ref[0, rows, :] + _rms_scale(o_ref[0, rows, :]) * gg


def _post_mix(x, attn_n, z, ada, wglu_bf16, wout_bf16, g_ssm, g_post, tm):
    bsz, seq, d = x.shape
    attn_w = attn_n.shape[-1]
    ssm_w = wglu_bf16.shape[0]
    kern = functools.partial(_post_mix_kernel, d=d, attn_w=attn_w)
    return pl.pallas_call(
        kern,
        grid=(bsz, seq // tm),
        in_specs=[
            pl.BlockSpec((1, tm, d), lambda b, i: (b, i, 0)),
            pl.BlockSpec((1, tm, attn_w), lambda b, i: (b, i, 0)),
            pl.BlockSpec((1, tm, ssm_w), lambda b, i: (b, i, 0)),
            pl.BlockSpec((1, 1, 6 * d), lambda b, i: (b, 0, 0)),
            _resident((ssm_w, ssm_w), lambda b, i: (0, 0)),
            _resident((attn_w + ssm_w, d), lambda b, i: (0, 0)),
            pl.BlockSpec((1, ssm_w), lambda b, i: (0, 0)),
            pl.BlockSpec((1, d), lambda b, i: (0, 0)),
        ],
        out_specs=pl.BlockSpec((1, tm, d), lambda b, i: (b, i, 0)),
        out_shape=jax.ShapeDtypeStruct((bsz, seq, d), _F32),
        compiler_params=pltpu.CompilerParams(
            dimension_semantics=("arbitrary", "arbitrary"), vmem_limit_bytes=VMEM_LIMIT),
        name="post_mix",
    )(x, attn_n, z, ada, wglu_bf16, wout_bf16,
      g_ssm.reshape(1, ssm_w), g_post.reshape(1, d))


HALO = BF16_ROWS
FFN_SUB = 2 * LANES
FFN_ROWS = 64


def _ffn_kernel(x_ref, xh_ref, ada_ref, gpre_ref, wv_ref, wg_ref, cp_ref,
                wd_ref, wdt_ref, gpost_ref, o_ref, h_ref, upv_ref, upg_ref, act_ref, carry_ref, *, d, tm):
    i = pl.program_id(1)
    j = pl.program_id(2)
    ada = ada_ref[0]
    sub = FFN_SUB

    @pl.when(j == 0)
    def _():
        g = gpre_ref[...]
        scale = ada[:, 4 * d:5 * d]
        shift = ada[:, 3 * d:4 * d]
        hh = _modulated_norm(xh_ref[0], g, scale, shift)
        h_ref[0:HALO, :] = jnp.where(i > 0, hh, 0.0).astype(_BF16)
        gs = g * (1.0 + scale)
        for r in range(0, tm, NORM_ROWS):
            xb = x_ref[0, r:r + NORM_ROWS, :]
            h_ref[HALO + r:HALO + r + NORM_ROWS, :] = (_rms_scale(xb) * gs + shift).astype(_BF16)
        o_ref[...] = jnp.zeros_like(o_ref)
        carry_ref[...] = jnp.zeros_like(carry_ref)

    hv = h_ref[...]
    tiles = sub // LANES
    for s in range(2):
        cs = slice(s * sub, (s + 1) * sub)
        rv = jnp.dot(hv, wv_ref[:, cs], preferred_element_type=_F32)
        rg = jnp.dot(hv, wg_ref[:, cs], preferred_element_type=_F32)
        for c in range(tiles):
            upv_ref[s * tiles + c] = rv[:, c * LANES:(c + 1) * LANES]
            upg_ref[s * tiles + c] = rg[:, c * LANES:(c + 1) * LANES]

    def conv(up_ref, col0, t, r):
        cs = slice(col0 + t * LANES, col0 + (t + 1) * LANES)
        out = cp_ref[0, CONV_WIDTH:CONV_WIDTH + 1, cs].astype(_BF16)
        for k in range(CONV_WIDTH):
            off = HALO - (CONV_WIDTH - 1) + k + r
            out = out + (up_ref[t, pl.ds(off, FFN_ROWS, stride=1), :].astype(_BF16)
                         * cp_ref[0, k:k + 1, cs].astype(_BF16))
        return out

    def activation(dst, s, col0):
        for c in range(tiles):
            t = s * tiles + c
            for r in range(0, tm, FFN_ROWS):
                a = _gelu_tanh_x2(conv(upg_ref, 2 * sub, t, r)) * conv(upv_ref, 0, t, r)
                dst[r:r + FFN_ROWS, col0 + c * LANES:col0 + (c + 1) * LANES] = a.astype(_BF16)

    act_ref[:, 0:sub] = carry_ref[...]
    activation(act_ref, 0, sub)
    o_ref[0] += jnp.dot(act_ref[...], wd_ref[...], preferred_element_type=_F32)
    activation(carry_ref, 1, 0)

    @pl.when(j == pl.num_programs(2) - 1)
    def _():
        o_ref[0] += jnp.dot(carry_ref[...], wdt_ref[...], preferred_element_type=_F32)
        gg = (1.0 + ada[:, 5 * d:6 * d]) * gpost_ref[...]
        for r in range(0, tm, NORM_ROWS):
            rows = slice(r, r + NORM_ROWS)
            o_ref[0, rows, :] = x_ref[0, rows, :] + _rms_scale(o_ref[0, rows, :]) * gg


def _ffn(x, ada, g_pre, wup_bf16, conv_w, conv_b, wdown_bf16, g_post, tm):
    bsz, seq, d = x.shape
    d_ff = wdown_bf16.shape[0]
    tf = 2 * FFN_SUB
    nj = d_ff // tf
    assert nj * tf == d_ff
    halo_blocks = tm // HALO
    kern = functools.partial(_ffn_kernel, d=d, tm=tm)
    half = jnp.concatenate([jnp.full((d_ff,), 0.5, _F32), jnp.ones((d_ff,), _F32)])
    cp = jnp.concatenate([conv_w * half, (conv_b * half)[None, :],
                          jnp.zeros((SUBLANES - CONV_WIDTH - 1, 2 * d_ff), _F32)], axis=0)
    cp = jnp.concatenate([cp[:, :d_ff].reshape(SUBLANES, nj, tf), cp[:, d_ff:].reshape(SUBLANES, nj, tf)],
                         axis=2).transpose(1, 0, 2)
    wd = jnp.concatenate([jnp.zeros((FFN_SUB, d), _BF16), wdown_bf16], axis=0)
    return pl.pallas_call(
        kern,
        grid=(bsz, seq // tm, nj),
        in_specs=[
            pl.BlockSpec((1, tm, d), lambda b, i, j: (b, i, 0)),
            pl.BlockSpec((1, HALO, d), lambda b, i, j: (b, jnp.maximum(i * halo_blocks - 1, 0), 0)),
            pl.BlockSpec((1, 1, 6 * d), lambda b, i, j: (b, 0, 0)),
            pl.BlockSpec((1, d), lambda b, i, j: (0, 0)),
            pl.BlockSpec((d, tf), lambda b, i, j: (0, j)),
            pl.BlockSpec((d, tf), lambda b, i, j: (0, nj + j)),
            pl.BlockSpec((1, SUBLANES, 2 * tf), lambda b, i, j: (j, 0, 0)),
            pl.BlockSpec((tf, d), lambda b, i, j: (j, 0)),
            _resident((FFN_SUB, d), lambda b, i, j: (d_ff // FFN_SUB, 0)),
            pl.BlockSpec((1, d), lambda b, i, j: (0, 0)),
        ],
        out_specs=pl.BlockSpec((1, tm, d), lambda b, i, j: (b, i, 0)),
        out_shape=jax.ShapeDtypeStruct((bsz, seq, d), _F32),
        scratch_shapes=[
            pltpu.VMEM((tm + HALO, d), _BF16),
            pltpu.VMEM((tf // LANES, tm + HALO, LANES), _F32),
            pltpu.VMEM((tf // LANES, tm + HALO, LANES), _F32),
            pltpu.VMEM((tm, tf), _BF16),
            pltpu.VMEM((tm, FFN_SUB), _BF16),
        ],
        compiler_params=pltpu.CompilerParams(
            dimension_semantics=("arbitrary", "arbitrary", "arbitrary"),
            vmem_limit_bytes=VMEM_LIMIT),
        name="conv_ffn",
    )(x, x, ada, g_pre.reshape(1, d), wup_bf16, wup_bf16, cp,
      wd, wd, g_post.reshape(1, d))


def _largest_tile(n, cap, quantum):
    best = quantum
    for t in range(quantum, min(n, cap) + 1, quantum):
        if n % t == 0:
            best = t
    return best


def kernel(x, c, w_ada, b_ada, g_pre_mix, g_post_mix, w_in, attn_sinks, lam_re, lam_im, log_step,
           ssm_b_re, ssm_b_im, ssm_c_re, ssm_c_im, ssm_d, w_glu, g_attn_out, g_ssm_out, w_out,
           g_pre_ffn, g_post_ffn, w_up, conv_w, conv_b, w_down):
    bsz, seq, d = x.shape
    depth = w_in.shape[0]
    ssm_w = w_glu.shape[1]
    attn_w = w_out.shape[1] - ssm_w
    kv_w = w_in.shape[2] - attn_w - ssm_w
    d_ff = w_down.shape[1]
    assert bsz == SUBLANES, "the scan keeps one batch row per f32 sublane"

    tm = _largest_tile(seq, 512, WINDOW)
    tc = _largest_tile(seq, 128, SUBLANES)
    slab = 2 * LANES
    qb = _largest_tile(seq, 4 * WINDOW, WINDOW) // WINDOW

    ada_all = _adaln(c, w_ada, b_ada)
    for l in range(depth):
        ada = ada_all[l][:, None, :]
        q, kv, u = _in_proj(x, ada, g_pre_mix[l], w_in[l].astype(_BF16), attn_w, kv_w, tm)
        attn_n = _attention(q, kv, attn_sinks[l], g_attn_out[l], qb)
        wb, cw, a_re, a_im = _ssm_params(lam_re[l], lam_im[l], log_step[l], ssm_b_re[l], ssm_b_im[l],
                                         ssm_c_re[l], ssm_c_im[l], slab)
        z = _ssm(u, wb, cw, a_re, a_im, ssm_d[l].reshape(ssm_w), tc)
        x = _post_mix(x, attn_n, z, ada, w_glu[l].astype(_BF16), w_out[l].astype(_BF16),
                      g_ssm_out[l], g_post_mix[l], tm)
        x = _ffn(x, ada, g_pre_ffn[l], w_up[l].astype(_BF16), conv_w[l], conv_b[l],
                 w_down[l].astype(_BF16), g_post_ffn[l], tm)
    return x
```

```python
import functools
import math

import jax
import jax.numpy as jnp
from jax import lax
from jax.experimental import pallas as pl
from jax.experimental.pallas import tpu as pltpu

HEAD_DIM = 64
KV_RATIO = 8
WINDOW = 128
SSM_GROUP = 16
S5_SUB = 64
STATE = 64
CONV_WIDTH = 3
EPS = 1e-6
NEG = -1e30
LOG2E = math.log2(math.e)

LANES = 128
SUBLANES = 8
BF16_ROWS = 16
NORM_ROWS = BF16_ROWS
VMEM_LIMIT = 56 * 1024 * 1024

_BF16 = jnp.bfloat16
_F32 = jnp.float32


def _gelu_tanh_x2(x):
    c = math.sqrt(2.0 / math.pi)
    return x + x * jnp.tanh(x * (c + (0.044715 * c) * (x * x)))


def _gelu_tanh(x):
    return 0.5 * _gelu_tanh_x2(x)


def _sigmoid(x):
    return 1.0 / (1.0 + jnp.exp(-x))


def _rms_scale(x):
    return x * lax.rsqrt(jnp.mean(x * x, axis=-1, keepdims=True) + EPS)


def _resident(shape, index_map):
    return pl.BlockSpec(shape, index_map, pipeline_mode=pl.Buffered(1))


def _adaln_kernel(c_ref, w_ref, b_ref, o_ref):
    c = c_ref[...]
    ca = c * _sigmoid(c)
    o_ref[0] = jnp.dot(ca.astype(_BF16), w_ref[0].astype(_BF16), preferred_element_type=_F32) + b_ref[0]


def _adaln(c, w_ada, b_ada, tn=1024):
    depth, d, n = w_ada.shape
    bsz = c.shape[0]
    return pl.pallas_call(
        _adaln_kernel,
        grid=(depth, n // tn),
        in_specs=[
            pl.BlockSpec((bsz, d), lambda l, j: (0, 0)),
            pl.BlockSpec((1, d, tn), lambda l, j: (l, 0, j)),
            pl.BlockSpec((1, 1, tn), lambda l, j: (l, 0, j)),
        ],
        out_specs=pl.BlockSpec((1, bsz, tn), lambda l, j: (l, 0, j)),
        out_shape=jax.ShapeDtypeStruct((depth, bsz, n), _F32),
        compiler_params=pltpu.CompilerParams(
            dimension_semantics=("arbitrary", "arbitrary"), vmem_limit_bytes=VMEM_LIMIT),
        name="adaln",
    )(c, w_ada, b_ada.reshape(depth, 1, n))


def _modulated_norm(xv, g, scale, shift):
    return _rms_scale(xv) * (g * (1.0 + scale)) + shift


def _in_proj_kernel(x_ref, ada_ref, g_ref, w_ref, q_ref, kv_ref, u_ref, *, d, attn_w, kv_w):
    ada = ada_ref[0]
    h = _modulated_norm(x_ref[0], g_ref[...], ada[:, d:2 * d], ada[:, 0:d])
    p = jnp.dot(h.astype(_BF16), w_ref[...], preferred_element_type=_F32)
    q_ref[0] = (p[:, :attn_w] * (HEAD_DIM ** -0.5 * LOG2E)).astype(_BF16)
    kv_ref[0] = p[:, attn_w:attn_w + kv_w].astype(_BF16)
    u_ref[0] = p[:, attn_w + kv_w:].astype(_BF16)


def _in_proj(x, ada, g, w_bf16, attn_w, kv_w, tm):
    bsz, seq, d = x.shape
    ncols = w_bf16.shape[1]
    ssm_w = ncols - attn_w - kv_w
    kern = functools.partial(_in_proj_kernel, d=d, attn_w=attn_w, kv_w=kv_w)
    return pl.pallas_call(
        kern,
        grid=(bsz, seq // tm),
        in_specs=[
            pl.BlockSpec((1, tm, d), lambda b, i: (b, i, 0)),
            pl.BlockSpec((1, 1, 6 * d), lambda b, i: (b, 0, 0)),
            pl.BlockSpec((1, d), lambda b, i: (0, 0)),
            _resident((d, ncols), lambda b, i: (0, 0)),
        ],
        out_specs=[
            pl.BlockSpec((1, tm, attn_w), lambda b, i: (b, i, 0)),
            pl.BlockSpec((1, tm, kv_w), lambda b, i: (b, i, 0)),
            pl.BlockSpec((1, tm, ssm_w), lambda b, i: (b, i, 0)),
        ],
        out_shape=[
            jax.ShapeDtypeStruct((bsz, seq, attn_w), _BF16),
            jax.ShapeDtypeStruct((bsz, seq, kv_w), _BF16),
            jax.ShapeDtypeStruct((bsz, seq, ssm_w), _BF16),
        ],
        compiler_params=pltpu.CompilerParams(
            dimension_semantics=("arbitrary", "arbitrary"), vmem_limit_bytes=VMEM_LIMIT),
        name="in_proj",
    )(x, ada, g.reshape(1, d), w_bf16)


def _attention_kernel(sink_ref, q_ref, kvc_ref, kvp_ref, g_ref, o_ref, *, n_kv, qb):
    step = pl.program_id(1)
    w = WINDOW
    rows = (qb + 1) * w
    kv = jnp.concatenate([kvp_ref[0], kvc_ref[0]], axis=0).astype(_F32)
    kw = n_kv * HEAD_DIM
    k2 = kv[:, :kw]
    v2 = kv[:, kw:]
    k2r = pltpu.roll(k2, HEAD_DIM, 1)
    v2r = pltpu.roll(v2, HEAD_DIM, 1)
    lo = lax.broadcasted_iota(jnp.int32, (rows, LANES), 1) < HEAD_DIM
    one_lo = jnp.where(lo, 1.0, 0.0).astype(_BF16)
    one_hi = jnp.where(lo, 0.0, 1.0).astype(_BF16)
    kk = (jnp.where(lo, k2, k2r).astype(_BF16), jnp.where(lo, k2r, k2).astype(_BF16))
    top = (jnp.concatenate([jnp.where(lo, v2, 0.0).astype(_BF16), one_lo], axis=1),
           jnp.concatenate([jnp.where(lo, v2r, 0.0).astype(_BF16), one_lo], axis=1))
    bot = (jnp.concatenate([jnp.where(lo, 0.0, v2r).astype(_BF16), one_hi], axis=1),
           jnp.concatenate([jnp.where(lo, 0.0, v2).astype(_BF16), one_hi], axis=1))

    qi = lax.broadcasted_iota(jnp.int32, (w, 2 * w), 0)
    kj = lax.broadcasted_iota(jnp.int32, (w, 2 * w), 1)
    in_band = (kj > qi) & (kj <= qi + w)
    qlane_lo = lax.broadcasted_iota(jnp.int32, (w, LANES), 1) < HEAD_DIM
    pairs_per_kv = KV_RATIO // 2
    n_pairs = n_kv * pairs_per_kv

    def scores(blk):
        out = []
        for pair in range(n_pairs):
            j = pair // pairs_per_kv
            q2 = q_ref[0, blk * w:(blk + 1) * w, pair * LANES:(pair + 1) * LANES]
            zero = jnp.zeros_like(q2)
            kkj = kk[j][blk * w:(blk + 2) * w]
            dn = (((1,), (1,)), ((), ()))
            out.append((lax.dot_general(jnp.where(qlane_lo, q2, zero), kkj, dn, preferred_element_type=_F32),
                        lax.dot_general(jnp.where(qlane_lo, zero, q2), kkj, dn, preferred_element_type=_F32)))
        return out

    def finish(blk, s_blk):
        valid = in_band if blk > 0 else in_band & ((kj >= w) | (step > 0))
        es, sk = [], []
        for pair in range(n_pairs):
            e2, k2_ = [], []
            for half in range(2):
                sink = sink_ref[2 * pair + half]
                s = jnp.where(valid, s_blk[pair][half], NEG)
                m = jnp.maximum(jnp.max(s, axis=-1, keepdims=True), sink)
                e2.append(jnp.exp2(s - m).astype(_BF16))
                k2_.append(jnp.exp2(sink - m))
            es.append(jnp.concatenate(e2, axis=1))
            sk.append(jnp.where(qlane_lo, k2_[0], k2_[1]))
        outs = []
        for pair in range(n_pairs):
            j = pair // pairs_per_kv
            r = jnp.concatenate([top[j][blk * w:(blk + 2) * w], bot[j][blk * w:(blk + 2) * w]], axis=0)
            ox = jnp.dot(es[pair], r, preferred_element_type=_F32)
            outs.append(ox[:, :LANES] / (ox[:, LANES:] + sk[pair]))
        o = jnp.concatenate(outs, axis=1)
        o_ref[0, blk * w:(blk + 1) * w, :] = (_rms_scale(o) * g_ref[...]).astype(_BF16)

    s_next = scores(0)
    for blk in range(qb):
        s_cur = s_next
        if blk + 1 < qb:
            s_next = scores(blk + 1)
        finish(blk, s_cur)


def _attention(q, kv, sinks, g, qb):
    bsz, seq, attn_w = q.shape
    kv_w = kv.shape[-1]
    n_kv = kv_w // (2 * HEAD_DIM)
    assert n_kv * HEAD_DIM == LANES and attn_w == n_kv * KV_RATIO * HEAD_DIM
    tq = qb * WINDOW
    kern = functools.partial(_attention_kernel, n_kv=n_kv, qb=qb)
    return pl.pallas_call(
        kern,
        grid=(bsz, seq // tq),
        in_specs=[
            pl.BlockSpec(memory_space=pltpu.SMEM),
            pl.BlockSpec((1, tq, attn_w), lambda b, n: (b, n, 0)),
            pl.BlockSpec((1, tq, kv_w), lambda b, n: (b, n, 0)),
            pl.BlockSpec((1, WINDOW, kv_w), lambda b, n: (b, jnp.maximum(n * qb - 1, 0), 0)),
            pl.BlockSpec((1, attn_w), lambda b, n: (0, 0)),
        ],
        out_specs=pl.BlockSpec((1, tq, attn_w), lambda b, n: (b, n, 0)),
        out_shape=jax.ShapeDtypeStruct((bsz, seq, attn_w), _BF16),
        compiler_params=pltpu.CompilerParams(
            dimension_semantics=("arbitrary", "arbitrary"), vmem_limit_bytes=VMEM_LIMIT),
        name="swa",
    )(sinks * LOG2E, q, kv, kv, g.reshape(1, attn_w))


def _ssm_kernel(u_ref, wb_ref, cw_ref, ar_ref, ai_ref, d_ref, z_ref, xs_ref, st_ref, il_ref,
                *, bsz, tc, ns):
    @pl.when(pl.program_id(1) == 0)
    def _():
        st_ref[...] = jnp.zeros_like(st_ref)

    tiles = il_ref.shape[0]
    for b in range(bsz):
        ub = u_ref[b].astype(_F32)
        for c in range(tiles):
            il_ref[c, pl.ds(b, tc, stride=bsz), :] = ub[:, c * LANES:(c + 1) * LANES]
    ar = jnp.broadcast_to(ar_ref[0], (bsz, ns))
    ai = jnp.broadcast_to(ai_ref[0], (bsz, ns))
    xr = st_ref[:, 0:ns]
    xi = st_ref[:, ns:2 * ns]
    sr = S5_SUB * bsz
    n_sub = tc // S5_SUB
    u32 = []
    for k in range(n_sub):
        rows = slice(k * sr, (k + 1) * sr)
        uk = jnp.concatenate([il_ref[c, rows, :] for c in range(tiles)], axis=1)
        u32.append(uk)
        xs_ref[rows, :] = jnp.dot(uk.astype(_BF16), wb_ref[0], preferred_element_type=_F32)
    ys = []
    for k in range(n_sub):
        for t in range(k * S5_SUB, (k + 1) * S5_SUB):
            r = slice(t * bsz, (t + 1) * bsz)
            nxr = ar * xr - ai * xi + xs_ref[r, 0:ns]
            nxi = ar * xi + ai * xr + xs_ref[r, ns:2 * ns]
            xs_ref[r, 0:ns] = nxr
            xs_ref[r, ns:2 * ns] = nxi
            xr, xi = nxr, nxi
        rows = slice(k * sr, (k + 1) * sr)
        ys.append(jnp.dot(xs_ref[rows, 0:ns].astype(_BF16), cw_ref[0, 0:ns, :], preferred_element_type=_F32)
                  + jnp.dot(xs_ref[rows, ns:2 * ns].astype(_BF16), cw_ref[0, ns:2 * ns, :],
                            preferred_element_type=_F32)
                  + d_ref[...] * u32[k])
    st_ref[:, 0:ns] = xr
    st_ref[:, ns:2 * ns] = xi
    y = jnp.concatenate(ys, axis=0)
    z = _gelu_tanh(y)
    for c in range(tiles):
        il_ref[c] = z[:, c * LANES:(c + 1) * LANES]
    for b in range(bsz):
        zb = jnp.concatenate([il_ref[c, pl.ds(b, tc, stride=bsz), :] for c in range(tiles)], axis=1)
        z_ref[b] = zb.astype(_BF16)


def _ssm_params(lam_re, lam_im, log_step, b_re, b_im, c_re, c_im, slab):
    g, p = lam_re.shape
    h = SSM_GROUP
    gs = slab // h
    n_slabs = g // gs
    dt = jnp.exp(log_step)[:, None]
    mag = jnp.exp(lam_re * dt)
    ang = lam_im * dt
    ab_re = mag * jnp.cos(ang)
    ab_im = mag * jnp.sin(ang)
    den = lam_re * lam_re + lam_im * lam_im
    f_re = ((ab_re - 1.0) * lam_re + ab_im * lam_im) / den
    f_im = (ab_im * lam_re - (ab_re - 1.0) * lam_im) / den
    bb_re = f_re[..., None] * b_re - f_im[..., None] * b_im
    bb_im = f_re[..., None] * b_im + f_im[..., None] * b_re
    eye = jnp.eye(gs, dtype=_F32)

    def block_diag_in(bb):
        t = bb.reshape(n_slabs, gs, p, h)
        return jnp.einsum('sgph,gk->sghkp', t, eye).reshape(n_slabs, gs * h, gs * p)

    def block_diag_out(cc):
        t = cc.reshape(n_slabs, gs, h, p)
        return jnp.einsum('sghp,gk->sgpkh', t, eye).reshape(n_slabs, gs * p, gs * h)

    wb = jnp.concatenate([block_diag_in(bb_re), block_diag_in(bb_im)], axis=2).astype(_BF16)
    cw = jnp.concatenate([block_diag_out(c_re), block_diag_out(-c_im)], axis=1).astype(_BF16)
    a_re = ab_re.reshape(n_slabs, 1, gs * p)
    a_im = ab_im.reshape(n_slabs, 1, gs * p)
    return wb, cw, a_re, a_im


def _ssm(u, wb, cw, a_re, a_im, d_skip, tc):
    bsz, seq, width = u.shape
    n_slabs, slab, ns2 = wb.shape
    ns = ns2 // 2
    kern = functools.partial(_ssm_kernel, bsz=bsz, tc=tc, ns=ns)
    return pl.pallas_call(
        kern,
        grid=(n_slabs, seq // tc),
        in_specs=[
            pl.BlockSpec((bsz, tc, slab), lambda s, t: (0, t, s)),
            pl.BlockSpec((1, slab, ns2), lambda s, t: (s, 0, 0)),
            pl.BlockSpec((1, ns2, slab), lambda s, t: (s, 0, 0)),
            pl.BlockSpec((1, 1, ns), lambda s, t: (s, 0, 0)),
            pl.BlockSpec((1, 1, ns), lambda s, t: (s, 0, 0)),
            pl.BlockSpec((1, slab), lambda s, t: (0, s)),
        ],
        out_specs=pl.BlockSpec((bsz, tc, slab), lambda s, t: (0, t, s)),
        out_shape=jax.ShapeDtypeStruct((bsz, seq, width), _BF16),
        scratch_shapes=[
            pltpu.VMEM((tc * bsz, ns2), _F32),
            pltpu.VMEM((bsz, ns2), _F32),
            pltpu.VMEM((slab // LANES, tc * bsz, LANES), _F32),
        ],
        compiler_params=pltpu.CompilerParams(
            dimension_semantics=("arbitrary", "arbitrary"), vmem_limit_bytes=VMEM_LIMIT),
        name="s5",
    )(u, wb, cw, a_re, a_im, d_skip.reshape(1, width))


def _post_mix_kernel(x_ref, attn_ref, z_ref, ada_ref, wglu_ref, wout_ref, gssm_ref, gpost_ref,
                     o_ref, *, d, attn_w):
    z = z_ref[0]
    zf = z.astype(_F32)
    ssm = zf * _sigmoid(jnp.dot(z, wglu_ref[...], preferred_element_type=_F32))
    ssm_n = (_rms_scale(ssm) * gssm_ref[...]).astype(_BF16)
    o_ref[0] = (jnp.dot(attn_ref[0], wout_ref[0:attn_w, :], preferred_element_type=_F32)
                + jnp.dot(ssm_n, wout_ref[attn_w:, :], preferred_element_type=_F32))
    gg = (1.0 + ada_ref[0][:, 2 * d:3 * d]) * gpost_ref[...]
    for r in range(0, o_ref.shape[1], NORM_ROWS):
        rows = slice(r, r + NORM_ROWS)
        o_ref[0, rows, :] = x_ref[0, rows, :] + _rms_scale(o_ref[0, rows, :]) * gg


def _post_mix(x, attn_n, z, ada, wglu_bf16, wout_bf16, g_ssm, g_post, tm):
    bsz, seq, d = x.shape
    attn_w = attn_n.shape[-1]
    ssm_w = wglu_bf16.shape[0]
    kern = functools.partial(_post_mix_kernel, d=d, attn_w=attn_w)
    return pl.pallas_call(
        kern,
        grid=(bsz, seq // tm),
        in_specs=[
            pl.BlockSpec((1, tm, d), lambda b, i: (b, i, 0)),
            pl.BlockSpec((1, tm, attn_w), lambda b, i: (b, i, 0)),
            pl.BlockSpec((1, tm, ssm_w), lambda b, i: (b, i, 0)),
            pl.BlockSpec((1, 1, 6 * d), lambda b, i: (b, 0, 0)),
            _resident((ssm_w, ssm_w), lambda b, i: (0, 0)),
            _resident((attn_w + ssm_w, d), lambda b, i: (0, 0)),
            pl.BlockSpec((1, ssm_w), lambda b, i: (0, 0)),
            pl.BlockSpec((1, d), lambda b, i: (0, 0)),
        ],
        out_specs=pl.BlockSpec((1, tm, d), lambda b, i: (b, i, 0)),
        out_shape=jax.ShapeDtypeStruct((bsz, seq, d), _F32),
        compiler_params=pltpu.CompilerParams(
            dimension_semantics=("arbitrary", "arbitrary"), vmem_limit_bytes=VMEM_LIMIT),
        name="post_mix",
    )(x, attn_n, z, ada, wglu_bf16, wout_bf16,
      g_ssm.reshape(1, ssm_w), g_post.reshape(1, d))


HALO = SUBLANES
FFN_SUB = 2 * LANES
FFN_ROWS = 64


def _ffn_kernel(x_ref, ada_ref, gpre_ref, wv_ref, wg_ref, cp_ref,
                wd_ref, wdt_ref, gpost_ref, o_ref, h_ref, upv_ref, upg_ref, act_ref, carry_ref,
                lastv_ref, lastg_ref, *, d, tm):
    i = pl.program_id(1)
    j = pl.program_id(2)
    ada = ada_ref[0]
    sub = FFN_SUB

    @pl.when(j == 0)
    def _():
        g = gpre_ref[...]
        scale = ada[:, 4 * d:5 * d]
        shift = ada[:, 3 * d:4 * d]
        gs = g * (1.0 + scale)
        for r in range(0, tm, NORM_ROWS):
            xb = x_ref[0, r:r + NORM_ROWS, :]
            h_ref[r:r + NORM_ROWS, :] = (_rms_scale(xb) * gs + shift).astype(_BF16)
        o_ref[...] = jnp.zeros_like(o_ref)
        carry_ref[...] = jnp.zeros_like(carry_ref)

    hv = h_ref[...]
    tiles = sub // LANES

    @pl.when(i == 0)
    def _():
        for t in range(2 * tiles):
            upv_ref[t, 0:HALO, :] = jnp.zeros((HALO, LANES), _F32)
            upg_ref[t, 0:HALO, :] = jnp.zeros((HALO, LANES), _F32)

    @pl.when(i > 0)
    def _():
        for t in range(2 * tiles):
            upv_ref[t, 0:HALO, :] = lastv_ref[j, t]
            upg_ref[t, 0:HALO, :] = lastg_ref[j, t]

    for s in range(2):
        cs = slice(s * sub, (s + 1) * sub)
        rv = jnp.dot(hv, wv_ref[:, cs], preferred_element_type=_F32)
        rg = jnp.dot(hv, wg_ref[:, cs], preferred_element_type=_F32)
        for c in range(tiles):
            upv_ref[s * tiles + c, HALO:, :] = rv[:, c * LANES:(c + 1) * LANES]
            upg_ref[s * tiles + c, HALO:, :] = rg[:, c * LANES:(c + 1) * LANES]
    for t in range(2 * tiles):
        lastv_ref[j, t] = upv_ref[t, tm:tm + HALO, :]
        lastg_ref[j, t] = upg_ref[t, tm:tm + HALO, :]

    def conv(up_ref, col0, t, r):
        cs = slice(col0 + t * LANES, col0 + (t + 1) * LANES)
        out = cp_ref[0, CONV_WIDTH:CONV_WIDTH + 1, cs].astype(_BF16)
        for k in range(CONV_WIDTH):
            off = HALO - (CONV_WIDTH - 1) + k + r
            out = out + (up_ref[t, pl.ds(off, FFN_ROWS, stride=1), :].astype(_BF16)
                         * cp_ref[0, k:k + 1, cs].astype(_BF16))
        return out

    def activation(dst, s, col0):
        for c in range(tiles):
            t = s * tiles + c
            for r in range(0, tm, FFN_ROWS):
                a = _gelu_tanh_x2(conv(upg_ref, 2 * sub, t, r)) * conv(upv_ref, 0, t, r)
                dst[r:r + FFN_ROWS, col0 + c * LANES:col0 + (c + 1) * LANES] = a.astype(_BF16)

    act_ref[:, 0:sub] = carry_ref[...]
    activation(act_ref, 0, sub)
    o_ref[0] += jnp.dot(act_ref[...], wd_ref[...], preferred_element_type=_F32)
    activation(carry_ref, 1, 0)

    @pl.when(j == pl.num_programs(2) - 1)
    def _():
        o_ref[0] += jnp.dot(carry_ref[...], wdt_ref[...], preferred_element_type=_F32)
        gg = (1.0 + ada[:, 5 * d:6 * d]) * gpost_ref[...]
        for r in range(0, tm, NORM_ROWS):
            rows = slice(r, r + NORM_ROWS)
            o_ref[0, rows, :] = x_ref[0, rows, :] + _rms_scale(o_ref[0, rows, :]) * gg


def _ffn(x, ada, g_pre, wup_bf16, conv_w, conv_b, wdown_bf16, g_post, tm):
    bsz, seq, d = x.shape
    d_ff = wdown_bf16.shape[0]
    tf = 2 * FFN_SUB
    nj = d_ff // tf
    assert nj * tf == d_ff
    kern = functools.partial(_ffn_kernel, d=d, tm=tm)
    half = jnp.concatenate([jnp.full((d_ff,), 0.5, _F32), jnp.ones((d_ff,), _F32)])
    cp = jnp.concatenate([conv_w * half, (conv_b * half)[None, :],
                          jnp.zeros((SUBLANES - CONV_WIDTH - 1, 2 * d_ff), _F32)], axis=0)
    cp = jnp.concatenate([cp[:, :d_ff].reshape(SUBLANES, nj, tf), cp[:, d_ff:].reshape(SUBLANES, nj, tf)],
                         axis=2).transpose(1, 0, 2)
    wd = jnp.concatenate([jnp.zeros((FFN_SUB, d), _BF16), wdown_bf16], axis=0)
    return pl.pallas_call(
        kern,
        grid=(bsz, seq // tm, nj),
        in_specs=[
            pl.BlockSpec((1, tm, d), lambda b, i, j: (b, i, 0)),
            pl.BlockSpec((1, 1, 6 * d), lambda b, i, j: (b, 0, 0)),
            pl.BlockSpec((1, d), lambda b, i, j: (0, 0)),
            pl.BlockSpec((d, tf), lambda b, i, j: (0, j)),
            pl.BlockSpec((d, tf), lambda b, i, j: (0, nj + j)),
            pl.BlockSpec((1, SUBLANES, 2 * tf), lambda b, i, j: (j, 0, 0)),
            pl.BlockSpec((tf, d), lambda b, i, j: (j, 0)),
            _resident((FFN_SUB, d), lambda b, i, j: (d_ff // FFN_SUB, 0)),
            pl.BlockSpec((1, d), lambda b, i, j: (0, 0)),
        ],
        out_specs=pl.BlockSpec((1, tm, d), lambda b, i, j: (b, i, 0)),
        out_shape=jax.ShapeDtypeStruct((bsz, seq, d), _F32),
        scratch_shapes=[
            pltpu.VMEM((tm, d), _BF16),
            pltpu.VMEM((tf // LANES, tm + HALO, LANES), _F32),
            pltpu.VMEM((tf // LANES, tm + HALO, LANES), _F32),
            pltpu.VMEM((tm, tf), _BF16),
            pltpu.VMEM((tm, FFN_SUB), _BF16),
            pltpu.VMEM((nj, tf // LANES, HALO, LANES), _F32),
            pltpu.VMEM((nj, tf // LANES, HALO, LANES), _F32),
        ],
        compiler_params=pltpu.CompilerParams(
            dimension_semantics=("arbitrary", "arbitrary", "arbitrary"),
            vmem_limit_bytes=VMEM_LIMIT),
        name="conv_ffn",
    )(x, ada, g_pre.reshape(1, d), wup_bf16, wup_bf16, cp,
      wd, wd, g_post.reshape(1, d))


def _largest_tile(n, cap, quantum):
    best = quantum
    for t in range(quantum, min(n, cap) + 1, quantum):
        if n % t == 0:
            best = t
    return best


def kernel(x, c, w_ada, b_ada, g_pre_mix, g_post_mix, w_in, attn_sinks, lam_re, lam_im, log_step,
           ssm_b_re, ssm_b_im, ssm_c_re, ssm_c_im, ssm_d, w_glu, g_attn_out, g_ssm_out, w_out,
           g_pre_ffn, g_post_ffn, w_up, conv_w, conv_b, w_down):
    bsz, seq, d = x.shape
    depth = w_in.shape[0]
    ssm_w = w_glu.shape[1]
    attn_w = w_out.shape[1] - ssm_w
    kv_w = w_in.shape[2] - attn_w - ssm_w
    d_ff = w_down.shape[1]
    assert bsz == SUBLANES, "the scan keeps one batch row per f32 sublane"

    tm = _largest_tile(seq, 512, WINDOW)
    tc = _largest_tile(seq, 128, SUBLANES)
    slab = 2 * LANES
    qb = _largest_tile(seq, 4 * WINDOW, WINDOW) // WINDOW

    ada_all = _adaln(c, w_ada, b_ada)
    for l in range(depth):
        ada = ada_all[l][:, None, :]
        q, kv, u = _in_proj(x, ada, g_pre_mix[l], w_in[l].astype(_BF16), attn_w, kv_w, tm)
        attn_n = _attention(q, kv, attn_sinks[l], g_attn_out[l], qb)
        wb, cw, a_re, a_im = _ssm_params(lam_re[l], lam_im[l], log_step[l], ssm_b_re[l], ssm_b_im[l],
                                         ssm_c_re[l], ssm_c_im[l], slab)
        z = _ssm(u, wb, cw, a_re, a_im, ssm_d[l].reshape(ssm_w), tc)
        x = _post_mix(x, attn_n, z, ada, w_glu[l].astype(_BF16), w_out[l].astype(_BF16),
                      g_ssm_out[l], g_post_mix[l], tm)
        x = _ffn(x, ada, g_pre_ffn[l], w_up[l].astype(_BF16), conv_w[l], conv_b[l],
                 w_down[l].astype(_BF16), g_post_ffn[l], tm)
    return x
```

```python
import functools
import math

import jax
import jax.numpy as jnp
from jax import lax
from jax.experimental import pallas as pl
from jax.experimental.pallas import tpu as pltpu

HEAD_DIM = 64
KV_RATIO = 8
WINDOW = 128
SSM_GROUP = 16
S5_SUB = 64
STATE = 64
CONV_WIDTH = 3
EPS = 1e-6
NEG = -1e30
LOG2E = math.log2(math.e)

LANES = 128
SUBLANES = 8
BF16_ROWS = 16
NORM_ROWS = BF16_ROWS
VMEM_LIMIT = 56 * 1024 * 1024

_BF16 = jnp.bfloat16
_F32 = jnp.float32


def _gelu_tanh_x2(x):
    c = math.sqrt(2.0 / math.pi)
    return x + x * jnp.tanh(x * (c + (0.044715 * c) * (x * x)))


def _gelu_tanh(x):
    return 0.5 * _gelu_tanh_x2(x)


def _sigmoid(x):
    return 1.0 / (1.0 + jnp.exp(-x))


def _rms_scale(x):
    return x * lax.rsqrt(jnp.mean(x * x, axis=-1, keepdims=True) + EPS)


def _resident(shape, index_map):
    return pl.BlockSpec(shape, index_map, pipeline_mode=pl.Buffered(1))


def _adaln_kernel(c_ref, w_ref, b_ref, o_ref):
    c = c_ref[...]
    ca = c * _sigmoid(c)
    o_ref[0] = jnp.dot(ca.astype(_BF16), w_ref[0].astype(_BF16), preferred_element_type=_F32) + b_ref[0]


def _adaln(c, w_ada, b_ada, tn=1024):
    depth, d, n = w_ada.shape
    bsz = c.shape[0]
    return pl.pallas_call(
        _adaln_kernel,
        grid=(depth, n // tn),
        in_specs=[
            pl.BlockSpec((bsz, d), lambda l, j: (0, 0)),
            pl.BlockSpec((1, d, tn), lambda l, j: (l, 0, j)),
            pl.BlockSpec((1, 1, tn), lambda l, j: (l, 0, j)),
        ],
        out_specs=pl.BlockSpec((1, bsz, tn), lambda l, j: (l, 0, j)),
        out_shape=jax.ShapeDtypeStruct((depth, bsz, n), _F32),
        compiler_params=pltpu.CompilerParams(
            dimension_semantics=("arbitrary", "arbitrary"), vmem_limit_bytes=VMEM_LIMIT),
        name="adaln",
    )(c, w_ada, b_ada.reshape(depth, 1, n))


def _modulated_norm(xv, g, scale, shift):
    return _rms_scale(xv) * (g * (1.0 + scale)) + shift


def _in_proj_kernel(x_ref, ada_ref, g_ref, w_ref, q_ref, kv_ref, u_ref, *, d, attn_w, kv_w):
    ada = ada_ref[0]
    h = _modulated_norm(x_ref[0], g_ref[...], ada[:, d:2 * d], ada[:, 0:d])
    p = jnp.dot(h.astype(_BF16), w_ref[...], preferred_element_type=_F32)
    q_ref[0] = (p[:, :attn_w] * (HEAD_DIM ** -0.5 * LOG2E)).astype(_BF16)
    kv_ref[0] = p[:, attn_w:attn_w + kv_w].astype(_BF16)
    u_ref[0] = p[:, attn_w + kv_w:].astype(_BF16)


def _in_proj(x, ada, g, w_bf16, attn_w, kv_w, tm):
    bsz, seq, d = x.shape
    ncols = w_bf16.shape[1]
    ssm_w = ncols - attn_w - kv_w
    kern = functools.partial(_in_proj_kernel, d=d, attn_w=attn_w, kv_w=kv_w)
    return pl.pallas_call(
        kern,
        grid=(bsz, seq // tm),
        in_specs=[
            pl.BlockSpec((1, tm, d), lambda b, i: (b, i, 0)),
            pl.BlockSpec((1, 1, 6 * d), lambda b, i: (b, 0, 0)),
            pl.BlockSpec((1, d), lambda b, i: (0, 0)),
            _resident((d, ncols), lambda b, i: (0, 0)),
        ],
        out_specs=[
            pl.BlockSpec((1, tm, attn_w), lambda b, i: (b, i, 0)),
            pl.BlockSpec((1, tm, kv_w), lambda b, i: (b, i, 0)),
            pl.BlockSpec((1, tm, ssm_w), lambda b, i: (b, i, 0)),
        ],
        out_shape=[
            jax.ShapeDtypeStruct((bsz, seq, attn_w), _BF16),
            jax.ShapeDtypeStruct((bsz, seq, kv_w), _BF16),
            jax.ShapeDtypeStruct((bsz, seq, ssm_w), _BF16),
        ],
        compiler_params=pltpu.CompilerParams(
            dimension_semantics=("arbitrary", "arbitrary"), vmem_limit_bytes=VMEM_LIMIT),
        name="in_proj",
    )(x, ada, g.reshape(1, d), w_bf16)


def _attention_kernel(sink_ref, q_ref, kvc_ref, kvp_ref, g_ref, o_ref, *, n_kv, qb):
    step = pl.program_id(1)
    w = WINDOW
    rows = (qb + 1) * w
    kv = jnp.concatenate([kvp_ref[0], kvc_ref[0]], axis=0).astype(_F32)
    kw = n_kv * HEAD_DIM
    k2 = kv[:, :kw]
    v2 = kv[:, kw:]
    k2r = pltpu.roll(k2, HEAD_DIM, 1)
    v2r = pltpu.roll(v2, HEAD_DIM, 1)
    lo = lax.broadcasted_iota(jnp.int32, (rows, LANES), 1) < HEAD_DIM
    one_lo = jnp.where(lo, 1.0, 0.0).astype(_BF16)
    one_hi = jnp.where(lo, 0.0, 1.0).astype(_BF16)
    kk = (jnp.where(lo, k2, k2r).astype(_BF16), jnp.where(lo, k2r, k2).astype(_BF16))
    top = (jnp.concatenate([jnp.where(lo, v2, 0.0).astype(_BF16), one_lo], axis=1),
           jnp.concatenate([jnp.where(lo, v2r, 0.0).astype(_BF16), one_lo], axis=1))
    bot = (jnp.concatenate([jnp.where(lo, 0.0, v2r).astype(_BF16), one_hi], axis=1),
           jnp.concatenate([jnp.where(lo, 0.0, v2).astype(_BF16), one_hi], axis=1))

    qi = lax.broadcasted_iota(jnp.int32, (w, 2 * w), 0)
    kj = lax.broadcasted_iota(jnp.int32, (w, 2 * w), 1)
    in_band = (kj > qi) & (kj <= qi + w)
    qlane_lo = lax.broadcasted_iota(jnp.int32, (w, LANES), 1) < HEAD_DIM
    pairs_per_kv = KV_RATIO // 2
    n_pairs = n_kv * pairs_per_kv

    def scores(blk):
        out = []
        for pair in range(n_pairs):
            j = pair // pairs_per_kv
            q2 = q_ref[0, blk * w:(blk + 1) * w, pair * LANES:(pair + 1) * LANES]
            zero = jnp.zeros_like(q2)
            kkj = kk[j][blk * w:(blk + 2) * w]
            dn = (((1,), (1,)), ((), ()))
            out.append((lax.dot_general(jnp.where(qlane_lo, q2, zero), kkj, dn, preferred_element_type=_F32),
                        lax.dot_general(jnp.where(qlane_lo, zero, q2), kkj, dn, preferred_element_type=_F32)))
        return out

    def finish(blk, s_blk):
        valid = in_band if blk > 0 else in_band & ((kj >= w) | (step > 0))
        es, sk = [], []
        for pair in range(n_pairs):
            e2, k2_ = [], []
            for half in range(2):
                sink = sink_ref[2 * pair + half]
                s = jnp.where(valid, s_blk[pair][half], NEG)
                m = jnp.maximum(jnp.max(s, axis=-1, keepdims=True), sink)
                e2.append(jnp.exp2(s - m).astype(_BF16))
                k2_.append(jnp.exp2(sink - m))
            es.append(jnp.concatenate(e2, axis=1))
            sk.append(jnp.where(qlane_lo, k2_[0], k2_[1]))
        outs = []
        for pair in range(n_pairs):
            j = pair // pairs_per_kv
            r = jnp.concatenate([top[j][blk * w:(blk + 2) * w], bot[j][blk * w:(blk + 2) * w]], axis=0)
            ox = jnp.dot(es[pair], r, preferred_element_type=_F32)
            outs.append(ox[:, :LANES] / (ox[:, LANES:] + sk[pair]))
        o = jnp.concatenate(outs, axis=1)
        o_ref[0, blk * w:(blk + 1) * w, :] = (_rms_scale(o) * g_ref[...]).astype(_BF16)

    s_next = scores(0)
    for blk in range(qb):
        s_cur = s_next
        if blk + 1 < qb:
            s_next = scores(blk + 1)
        finish(blk, s_cur)


def _attention(q, kv, sinks, g, qb):
    bsz, seq, attn_w = q.shape
    kv_w = kv.shape[-1]
    n_kv = kv_w // (2 * HEAD_DIM)
    assert n_kv * HEAD_DIM == LANES and attn_w == n_kv * KV_RATIO * HEAD_DIM
    tq = qb * WINDOW
    kern = functools.partial(_attention_kernel, n_kv=n_kv, qb=qb)
    return pl.pallas_call(
        kern,
        grid=(bsz, seq // tq),
        in_specs=[
            pl.BlockSpec(memory_space=pltpu.SMEM),
            pl.BlockSpec((1, tq, attn_w), lambda b, n: (b, n, 0)),
            pl.BlockSpec((1, tq, kv_w), lambda b, n: (b, n, 0)),
            pl.BlockSpec((1, WINDOW, kv_w), lambda b, n: (b, jnp.maximum(n * qb - 1, 0), 0)),
            pl.BlockSpec((1, attn_w), lambda b, n: (0, 0)),
        ],
        out_specs=pl.BlockSpec((1, tq, attn_w), lambda b, n: (b, n, 0)),
        out_shape=jax.ShapeDtypeStruct((bsz, seq, attn_w), _BF16),
        compiler_params=pltpu.CompilerParams(
            dimension_semantics=("arbitrary", "arbitrary"), vmem_limit_bytes=VMEM_LIMIT),
        name="swa",
    )(sinks * LOG2E, q, kv, kv, g.reshape(1, attn_w))


def _ssm_kernel(u_ref, wb_ref, cw_ref, ar_ref, ai_ref, d_ref, z_ref, xs_ref, st_ref, il_ref,
                *, bsz, tc, ns):
    @pl.when(pl.program_id(1) == 0)
    def _():
        st_ref[...] = jnp.zeros_like(st_ref)

    tiles = il_ref.shape[0]
    for b in range(bsz):
        ub = u_ref[b].astype(_F32)
        for c in range(tiles):
            il_ref[c, pl.ds(b, tc, stride=bsz), :] = ub[:, c * LANES:(c + 1) * LANES]
    ar = jnp.broadcast_to(ar_ref[0], (bsz, ns))
    ai = jnp.broadcast_to(ai_ref[0], (bsz, ns))
    xr = st_ref[:, 0:ns]
    xi = st_ref[:, ns:2 * ns]
    sr = S5_SUB * bsz
    n_sub = tc // S5_SUB
    u32 = []
    for k in range(n_sub):
        rows = slice(k * sr, (k + 1) * sr)
        uk = jnp.concatenate([il_ref[c, rows, :] for c in range(tiles)], axis=1)
        u32.append(uk)
        xs_ref[rows, :] = jnp.dot(uk.astype(_BF16), wb_ref[0], preferred_element_type=_F32)
    ys = []
    for k in range(n_sub):
        for t in range(k * S5_SUB, (k + 1) * S5_SUB):
            r = slice(t * bsz, (t + 1) * bsz)
            nxr = ar * xr - ai * xi + xs_ref[r, 0:ns]
            nxi = ar * xi + ai * xr + xs_ref[r, ns:2 * ns]
            xs_ref[r, 0:ns] = nxr
            xs_ref[r, ns:2 * ns] = nxi
            xr, xi = nxr, nxi
        rows = slice(k * sr, (k + 1) * sr)
        ys.append(jnp.dot(xs_ref[rows, 0:ns].astype(_BF16), cw_ref[0, 0:ns, :], preferred_element_type=_F32)
                  + jnp.dot(xs_ref[rows, ns:2 * ns].astype(_BF16), cw_ref[0, ns:2 * ns, :],
                            preferred_element_type=_F32)
                  + d_ref[...] * u32[k])
    st_ref[:, 0:ns] = xr
    st_ref[:, ns:2 * ns] = xi
    y = jnp.concatenate(ys, axis=0)
    z = _gelu_tanh(y)
    for c in range(tiles):
        il_ref[c] = z[:, c * LANES:(c + 1) * LANES]
    for b in range(bsz):
        zb = jnp.concatenate([il_ref[c, pl.ds(b, tc, stride=bsz), :] for c in range(tiles)], axis=1)
        z_ref[b] = zb.astype(_BF16)


def _ssm_params(lam_re, lam_im, log_step, b_re, b_im, c_re, c_im, slab):
    g, p = lam_re.shape
    h = SSM_GROUP
    gs = slab // h
    n_slabs = g // gs
    dt = jnp.exp(log_step)[:, None]
    mag = jnp.exp(lam_re * dt)
    ang = lam_im * dt
    ab_re = mag * jnp.cos(ang)
    ab_im = mag * jnp.sin(ang)
    den = lam_re * lam_re + lam_im * lam_im
    f_re = ((ab_re - 1.0) * lam_re + ab_im * lam_im) / den
    f_im = (ab_im * lam_re - (ab_re - 1.0) * lam_im) / den
    bb_re = f_re[..., None] * b_re - f_im[..., None] * b_im
    bb_im = f_re[..., None] * b_im + f_im[..., None] * b_re
    eye = jnp.eye(gs, dtype=_F32)

    def block_diag_in(bb):
        t = bb.reshape(n_slabs, gs, p, h)
        return jnp.einsum('sgph,gk->sghkp', t, eye).reshape(n_slabs, gs * h, gs * p)

    def block_diag_out(cc):
        t = cc.reshape(n_slabs, gs, h, p)
        return jnp.einsum('sghp,gk->sgpkh', t, eye).reshape(n_slabs, gs * p, gs * h)

    wb = jnp.concatenate([block_diag_in(bb_re), block_diag_in(bb_im)], axis=2).astype(_BF16)
    cw = jnp.concatenate([block_diag_out(c_re), block_diag_out(-c_im)], axis=1).astype(_BF16)
    a_re = ab_re.reshape(n_slabs, 1, gs * p)
    a_im = ab_im.reshape(n_slabs, 1, gs * p)
    return wb, cw, a_re, a_im


def _ssm(u, wb, cw, a_re, a_im, d_skip, tc):
    bsz, seq, width = u.shape
    n_slabs, slab, ns2 = wb.shape
    ns = ns2 // 2
    kern = functools.partial(_ssm_kernel, bsz=bsz, tc=tc, ns=ns)
    return pl.pallas_call(
        kern,
        grid=(n_slabs, seq // tc),
        in_specs=[
            pl.BlockSpec((bsz, tc, slab), lambda s, t: (0, t, s)),
            pl.BlockSpec((1, slab, ns2), lambda s, t: (s, 0, 0)),
            pl.BlockSpec((1, ns2, slab), lambda s, t: (s, 0, 0)),
            pl.BlockSpec((1, 1, ns), lambda s, t: (s, 0, 0)),
            pl.BlockSpec((1, 1, ns), lambda s, t: (s, 0, 0)),
            pl.BlockSpec((1, slab), lambda s, t: (0, s)),
        ],
        out_specs=pl.BlockSpec((bsz, tc, slab), lambda s, t: (0, t, s)),
        out_shape=jax.ShapeDtypeStruct((bsz, seq, width), _BF16),
        scratch_shapes=[
            pltpu.VMEM((tc * bsz, ns2), _F32),
            pltpu.VMEM((bsz, ns2), _F32),
            pltpu.VMEM((slab // LANES, tc * bsz, LANES), _F32),
        ],
        compiler_params=pltpu.CompilerParams(
            dimension_semantics=("arbitrary", "arbitrary"), vmem_limit_bytes=VMEM_LIMIT),
        name="s5",
    )(u, wb, cw, a_re, a_im, d_skip.reshape(1, width))


def _post_mix_kernel(x_ref, attn_ref, z_ref, ada_ref, wglu_ref, wout_ref, gssm_ref, gpost_ref,
                     o_ref, *, d, attn_w):
    z = z_ref[0]
    zf = z.astype(_F32)
    ssm = zf * _sigmoid(jnp.dot(z, wglu_ref[...], preferred_element_type=_F32))
    ssm_n = (_rms_scale(ssm) * gssm_ref[...]).astype(_BF16)
    o_ref[0] = (jnp.dot(attn_ref[0], wout_ref[0:attn_w, :], preferred_element_type=_F32)
                + jnp.dot(ssm_n, wout_ref[attn_w:, :], preferred_element_type=_F32))
    gg = (1.0 + ada_ref[0][:, 2 * d:3 * d]) * gpost_ref[...]
    for r in range(0, o_ref.shape[1], NORM_ROWS):
        rows = slice(r, r + NORM_ROWS)
        o_ref[0, rows, :] = x_ref[0, rows, :] + _rms_scale(o_ref[0, rows, :]) * gg


def _post_mix(x, attn_n, z, ada, wglu_bf16, wout_bf16, g_ssm, g_post, tm):
    bsz, seq, d = x.shape
    attn_w = attn_n.shape[-1]
    ssm_w = wglu_bf16.shape[0]
    kern = functools.partial(_post_mix_kernel, d=d, attn_w=attn_w)
    return pl.pallas_call(
        kern,
        grid=(bsz, seq // tm),
        in_specs=[
            pl.BlockSpec((1, tm, d), lambda b, i: (b, i, 0)),
            pl.BlockSpec((1, tm, attn_w), lambda b, i: (b, i, 0)),
            pl.BlockSpec((1, tm, ssm_w), lambda b, i: (b, i, 0)),
            pl.BlockSpec((1, 1, 6 * d), lambda b, i: (b, 0, 0)),
            _resident((ssm_w, ssm_w), lambda b, i: (0, 0)),
            _resident((attn_w + ssm_w, d), lambda b, i: (0, 0)),
            pl.BlockSpec((1, ssm_w), lambda b, i: (0, 0)),
            pl.BlockSpec((1, d), lambda b, i: (0, 0)),
        ],
        out_specs=pl.BlockSpec((1, tm, d), lambda b, i: (b, i, 0)),
        out_shape=jax.ShapeDtypeStruct((bsz, seq, d), _F32),
        compiler_params=pltpu.CompilerParams(
            dimension_semantics=("arbitrary", "arbitrary"), vmem_limit_bytes=VMEM_LIMIT),
        name="post_mix",
    )(x, attn_n, z, ada, wglu_bf16, wout_bf16,
      g_ssm.reshape(1, ssm_w), g_post.reshape(1, d))


HALO = BF16_ROWS
FFN_SUB = 2 * LANES
FFN_ROWS = 64


def _ffn_kernel(x_ref, xh_ref, ada_ref, gpre_ref, wv_ref, wg_ref, cp_ref,
                wd_ref, wdt_ref, gpost_ref, o_ref, h_ref, upv_ref, upg_ref, act_ref, carry_ref, *, d, tm):
    i = pl.program_id(1)
    j = pl.program_id(2)
    ada = ada_ref[0]
    sub = FFN_SUB

    @pl.when(j == 0)
    def _():
        g = gpre_ref[...]
        scale = ada[:, 4 * d:5 * d]
        shift = ada[:, 3 * d:4 * d]
        hh = _modulated_norm(xh_ref[0], g, scale, shift)
        h_ref[0:HALO, :] = jnp.where(i > 0, hh, 0.0).astype(_BF16)
        gs = g * (1.0 + scale)
        for r in range(0, tm, NORM_ROWS):
            xb = x_ref[0, r:r + NORM_ROWS, :]
            h_ref[HALO + r:HALO + r + NORM_ROWS, :] = (_rms_scale(xb) * gs + shift).astype(_BF16)
        o_ref[...] = jnp.zeros_like(o_ref)
        carry_ref[...] = jnp.zeros_like(carry_ref)

    hv = h_ref[...]
    tiles = sub // LANES
    for s in range(2):
        cs = slice(s * sub, (s + 1) * sub)
        rv = jnp.dot(hv, wv_ref[:, cs], preferred_element_type=_F32)
        rg = jnp.dot(hv, wg_ref[:, cs], preferred_element_type=_F32)
        for c in range(tiles):
            upv_ref[s * tiles + c] = rv[:, c * LANES:(c + 1) * LANES]
            upg_ref[s * tiles + c] = rg[:, c * LANES:(c + 1) * LANES]

    def conv(up_ref, col0, t, r):
        cs = slice(col0 + t * LANES, col0 + (t + 1) * LANES)
        out = cp_ref[0, CONV_WIDTH:CONV_WIDTH + 1, cs].astype(_BF16)
        for k in range(CONV_WIDTH):
            off = HALO - (CONV_WIDTH - 1) + k + r
            out = out + (up_ref[t, pl.ds(off, FFN_ROWS, stride=1), :].astype(_BF16)
                         * cp_ref[0, k:k + 1, cs].astype(_BF16))
        return out

    def activation(dst, s, col0):
        for c in range(tiles):
            t = s * tiles + c
            for r in range(0, tm, FFN_ROWS):
                a = _gelu_tanh_x2(conv(upg_ref, 2 * sub, t, r)) * conv(upv_ref, 0, t, r)
                dst[r:r + FFN_ROWS, col0 + c * LANES:col0 + (c + 1) * LANES] = a.astype(_BF16)

    act_ref[:, 0:sub] = carry_ref[...]
    activation(act_ref, 0, sub)
    o_ref[0] += jnp.dot(act_ref[...], wd_ref[...], preferred_element_type=_F32)
    activation(carry_ref, 1, 0)

    @pl.when(j == pl.num_programs(2) - 1)
    def _():
        o_ref[0] += jnp.dot(carry_ref[...], wdt_ref[...], preferred_element_type=_F32)
        gg = (1.0 + ada[:, 5 * d:6 * d]) * gpost_ref[...]
        for r in range(0, tm, NORM_ROWS):
            rows = slice(r, r + NORM_ROWS)
            o_ref[0, rows, :] = x_ref[0, rows, :] + _rms_scale(o_ref[0, rows, :]) * gg


def _ffn(x, ada, g_pre, wup_bf16, conv_w, conv_b, wdown_bf16, g_post, tm):
    bsz, seq, d = x.shape
    d_ff = wdown_bf16.shape[0]
    tf = 2 * FFN_SUB
    nj = d_ff // tf
    assert nj * tf == d_ff
    halo_blocks = tm // HALO
    kern = functools.partial(_ffn_kernel, d=d, tm=tm)
    half = jnp.concatenate([jnp.full((d_ff,), 0.5, _F32), jnp.ones((d_ff,), _F32)])
    cp = jnp.concatenate([conv_w * half, (conv_b * half)[None, :],
                          jnp.zeros((SUBLANES - CONV_WIDTH - 1, 2 * d_ff), _F32)], axis=0)
    cp = jnp.concatenate([cp[:, :d_ff].reshape(SUBLANES, nj, tf), cp[:, d_ff:].reshape(SUBLANES, nj, tf)],
                         axis=2).transpose(1, 0, 2)
    wd = jnp.concatenate([jnp.zeros((FFN_SUB, d), _BF16), wdown_bf16], axis=0)
    return pl.pallas_call(
        kern,
        grid=(bsz, seq // tm, nj),
        in_specs=[
            pl.BlockSpec((1, tm, d), lambda b, i, j: (b, i, 0)),
            pl.BlockSpec((1, HALO, d), lambda b, i, j: (b, jnp.maximum(i * halo_blocks - 1, 0), 0)),
            pl.BlockSpec((1, 1, 6 * d), lambda b, i, j: (b, 0, 0)),
            pl.BlockSpec((1, d), lambda b, i, j: (0, 0)),
            pl.BlockSpec((d, tf), lambda b, i, j: (0, j)),
            pl.BlockSpec((d, tf), lambda b, i, j: (0, nj + j)),
            pl.BlockSpec((1, SUBLANES, 2 * tf), lambda b, i, j: (j, 0, 0)),
            pl.BlockSpec((tf, d), lambda b, i, j: (j, 0)),
            _resident((FFN_SUB, d), lambda b, i, j: (d_ff // FFN_SUB, 0)),
            pl.BlockSpec((1, d), lambda b, i, j: (0, 0)),
        ],
        out_specs=pl.BlockSpec((1, tm, d), lambda b, i, j: (b, i, 0)),
        out_shape=jax.ShapeDtypeStruct((bsz, seq, d), _F32),
        scratch_shapes=[
            pltpu.VMEM((tm + HALO, d), _BF16),
            pltpu.VMEM((tf // LANES, tm + HALO, LANES), _F32),
            pltpu.VMEM((tf // LANES, tm + HALO, LANES), _F32),
            pltpu.VMEM((tm, tf), _BF16),
            pltpu.VMEM((tm, FFN_SUB), _BF16),
        ],
        compiler_params=pltpu.CompilerParams(
            dimension_semantics=("arbitrary", "arbitrary", "arbitrary"),
            vmem_limit_bytes=VMEM_LIMIT),
        name="conv_ffn",
    )(x, x, ada, g_pre.reshape(1, d), wup_bf16, wup_bf16, cp,
      wd, wd, g_post.reshape(1, d))


def _largest_tile(n, cap, quantum):
    best = quantum
    for t in range(quantum, min(n, cap) + 1, quantum):
        if n % t == 0:
            best = t
    return best


def kernel(x, c, w_ada, b_ada, g_pre_mix, g_post_mix, w_in, attn_sinks, lam_re, lam_im, log_step,
           ssm_b_re, ssm_b_im, ssm_c_re, ssm_c_im, ssm_d, w_glu, g_attn_out, g_ssm_out, w_out,
           g_pre_ffn, g_post_ffn, w_up, conv_w, conv_b, w_down):
    bsz, seq, d = x.shape
    depth = w_in.shape[0]
    ssm_w = w_glu.shape[1]
    attn_w = w_out.shape[1] - ssm_w
    kv_w = w_in.shape[2] - attn_w - ssm_w
    d_ff = w_down.shape[1]
    assert bsz == SUBLANES, "the scan keeps one batch row per f32 sublane"

    tm = _largest_tile(seq, 512, WINDOW)
    tc = _largest_tile(seq, 256, S5_SUB)
    slab = 2 * LANES
    qb = _largest_tile(seq, 4 * WINDOW, WINDOW) // WINDOW

    ada_all = _adaln(c, w_ada, b_ada)
    for l in range(depth):
        ada = ada_all[l][:, None, :]
        q, kv, u = _in_proj(x, ada, g_pre_mix[l], w_in[l].astype(_BF16), attn_w, kv_w, tm)
        attn_n = _attention(q, kv, attn_sinks[l], g_attn_out[l], qb)
        wb, cw, a_re, a_im = _ssm_params(lam_re[l], lam_im[l], log_step[l], ssm_b_re[l], ssm_b_im[l],
                                         ssm_c_re[l], ssm_c_im[l], slab)
        z = _ssm(u, wb, cw, a_re, a_im, ssm_d[l].reshape(ssm_w), tc)
        x = _post_mix(x, attn_n, z, ada, w_glu[l].astype(_BF16), w_out[l].astype(_BF16),
                      g_ssm_out[l], g_post_mix[l], tm)
        x = _ffn(x, ada, g_pre_ffn[l], w_up[l].astype(_BF16), conv_w[l], conv_b[l],
                 w_down[l].astype(_BF16), g_post_ffn[l], tm)
    return x
```

```python
import functools
import math

import jax
import jax.numpy as jnp
from jax import lax
from jax.experimental import pallas as pl
from jax.experimental.pallas import tpu as pltpu

HEAD_DIM = 64
KV_RATIO = 8
WINDOW = 128
SSM_GROUP = 16
S5_SUB = 128
STATE = 64
CONV_WIDTH = 3
EPS = 1e-6
NEG = -1e30
LOG2E = math.log2(math.e)

LANES = 128
SUBLANES = 8
BF16_ROWS = 16
NORM_ROWS = BF16_ROWS
VMEM_LIMIT = 56 * 1024 * 1024

_BF16 = jnp.bfloat16
_F32 = jnp.float32


def _gelu_tanh_x2(x):
    c = math.sqrt(2.0 / math.pi)
    return x + x * jnp.tanh(x * (c + (0.044715 * c) * (x * x)))


def _gelu_tanh(x):
    return 0.5 * _gelu_tanh_x2(x)


def _sigmoid(x):
    return 1.0 / (1.0 + jnp.exp(-x))


def _rms_scale(x):
    return x * lax.rsqrt(jnp.mean(x * x, axis=-1, keepdims=True) + EPS)


def _resident(shape, index_map):
    return pl.BlockSpec(shape, index_map, pipeline_mode=pl.Buffered(1))


def _adaln_kernel(c_ref, w_ref, b_ref, o_ref):
    c = c_ref[...]
    ca = c * _sigmoid(c)
    o_ref[0] = jnp.dot(ca.astype(_BF16), w_ref[0].astype(_BF16), preferred_element_type=_F32) + b_ref[0]


def _adaln(c, w_ada, b_ada, tn=1024):
    depth, d, n = w_ada.shape
    bsz = c.shape[0]
    return pl.pallas_call(
        _adaln_kernel,
        grid=(depth, n // tn),
        in_specs=[
            pl.BlockSpec((bsz, d), lambda l, j: (0, 0)),
            pl.BlockSpec((1, d, tn), lambda l, j: (l, 0, j)),
            pl.BlockSpec((1, 1, tn), lambda l, j: (l, 0, j)),
        ],
        out_specs=pl.BlockSpec((1, bsz, tn), lambda l, j: (l, 0, j)),
        out_shape=jax.ShapeDtypeStruct((depth, bsz, n), _F32),
        compiler_params=pltpu.CompilerParams(
            dimension_semantics=("arbitrary", "arbitrary"), vmem_limit_bytes=VMEM_LIMIT),
        name="adaln",
    )(c, w_ada, b_ada.reshape(depth, 1, n))


def _modulated_norm(xv, g, scale, shift):
    return _rms_scale(xv) * (g * (1.0 + scale)) + shift


def _in_proj_kernel(x_ref, ada_ref, g_ref, w_ref, q_ref, kv_ref, u_ref, *, d, attn_w, kv_w):
    ada = ada_ref[0]
    h = _modulated_norm(x_ref[0], g_ref[...], ada[:, d:2 * d], ada[:, 0:d])
    p = jnp.dot(h.astype(_BF16), w_ref[...], preferred_element_type=_F32)
    q_ref[0] = (p[:, :attn_w] * (HEAD_DIM ** -0.5 * LOG2E)).astype(_BF16)
    kv_ref[0] = p[:, attn_w:attn_w + kv_w].astype(_BF16)
    u_ref[0] = p[:, attn_w + kv_w:].astype(_BF16)


def _in_proj(x, ada, g, w_bf16, attn_w, kv_w, tm):
    bsz, seq, d = x.shape
    ncols = w_bf16.shape[1]
    ssm_w = ncols - attn_w - kv_w
    kern = functools.partial(_in_proj_kernel, d=d, attn_w=attn_w, kv_w=kv_w)
    return pl.pallas_call(
        kern,
        grid=(bsz, seq // tm),
        in_specs=[
            pl.BlockSpec((1, tm, d), lambda b, i: (b, i, 0)),
            pl.BlockSpec((1, 1, 6 * d), lambda b, i: (b, 0, 0)),
            pl.BlockSpec((1, d), lambda b, i: (0, 0)),
            _resident((d, ncols), lambda b, i: (0, 0)),
        ],
        out_specs=[
            pl.BlockSpec((1, tm, attn_w), lambda b, i: (b, i, 0)),
            pl.BlockSpec((1, tm, kv_w), lambda b, i: (b, i, 0)),
            pl.BlockSpec((1, tm, ssm_w), lambda b, i: (b, i, 0)),
        ],
        out_shape=[
            jax.ShapeDtypeStruct((bsz, seq, attn_w), _BF16),
            jax.ShapeDtypeStruct((bsz, seq, kv_w), _BF16),
            jax.ShapeDtypeStruct((bsz, seq, ssm_w), _BF16),
        ],
        compiler_params=pltpu.CompilerParams(
            dimension_semantics=("arbitrary", "arbitrary"), vmem_limit_bytes=VMEM_LIMIT),
        name="in_proj",
    )(x, ada, g.reshape(1, d), w_bf16)


def _attention_kernel(sink_ref, q_ref, kvc_ref, kvp_ref, g_ref, o_ref, *, n_kv, qb):
    step = pl.program_id(1)
    w = WINDOW
    rows = (qb + 1) * w
    kv = jnp.concatenate([kvp_ref[0], kvc_ref[0]], axis=0).astype(_F32)
    kw = n_kv * HEAD_DIM
    k2 = kv[:, :kw]
    v2 = kv[:, kw:]
    k2r = pltpu.roll(k2, HEAD_DIM, 1)
    v2r = pltpu.roll(v2, HEAD_DIM, 1)
    lo = lax.broadcasted_iota(jnp.int32, (rows, LANES), 1) < HEAD_DIM
    one_lo = jnp.where(lo, 1.0, 0.0).astype(_BF16)
    one_hi = jnp.where(lo, 0.0, 1.0).astype(_BF16)
    kk = (jnp.where(lo, k2, k2r).astype(_BF16), jnp.where(lo, k2r, k2).astype(_BF16))
    top = (jnp.concatenate([jnp.where(lo, v2, 0.0).astype(_BF16), one_lo], axis=1),
           jnp.concatenate([jnp.where(lo, v2r, 0.0).astype(_BF16), one_lo], axis=1))
    bot = (jnp.concatenate([jnp.where(lo, 0.0, v2r).astype(_BF16), one_hi], axis=1),
           jnp.concatenate([jnp.where(lo, 0.0, v2).astype(_BF16), one_hi], axis=1))

    qi = lax.broadcasted_iota(jnp.int32, (w, 2 * w), 0)
    kj = lax.broadcasted_iota(jnp.int32, (w, 2 * w), 1)
    in_band = (kj > qi) & (kj <= qi + w)
    qlane_lo = lax.broadcasted_iota(jnp.int32, (w, LANES), 1) < HEAD_DIM
    pairs_per_kv = KV_RATIO // 2
    n_pairs = n_kv * pairs_per_kv

    def scores(blk):
        out = []
        for pair in range(n_pairs):
            j = pair // pairs_per_kv
            q2 = q_ref[0, blk * w:(blk + 1) * w, pair * LANES:(pair + 1) * LANES]
            zero = jnp.zeros_like(q2)
            kkj = kk[j][blk * w:(blk + 2) * w]
            dn = (((1,), (1,)), ((), ()))
            out.append((lax.dot_general(jnp.where(qlane_lo, q2, zero), kkj, dn, preferred_element_type=_F32),
                        lax.dot_general(jnp.where(qlane_lo, zero, q2), kkj, dn, preferred_element_type=_F32)))
        return out

    def finish(blk, s_blk):
        valid = in_band if blk > 0 else in_band & ((kj >= w) | (step > 0))
        es, sk = [], []
        for pair in range(n_pairs):
            e2, k2_ = [], []
            for half in range(2):
                sink = sink_ref[2 * pair + half]
                s = jnp.where(valid, s_blk[pair][half], NEG)
                m = jnp.maximum(jnp.max(s, axis=-1, keepdims=True), sink)
                e2.append(jnp.exp2(s - m).astype(_BF16))
                k2_.append(jnp.exp2(sink - m))
            es.append(jnp.concatenate(e2, axis=1))
            sk.append(jnp.where(qlane_lo, k2_[0], k2_[1]))
        outs = []
        for pair in range(n_pairs):
            j = pair // pairs_per_kv
            r = jnp.concatenate([top[j][blk * w:(blk + 2) * w], bot[j][blk * w:(blk + 2) * w]], axis=0)
            ox = jnp.dot(es[pair], r, preferred_element_type=_F32)
            outs.append(ox[:, :LANES] / (ox[:, LANES:] + sk[pair]))
        o = jnp.concatenate(outs, axis=1)
        o_ref[0, blk * w:(blk + 1) * w, :] = (_rms_scale(o) * g_ref[...]).astype(_BF16)

    s_next = scores(0)
    for blk in range(qb):
        s_cur = s_next
        if blk + 1 < qb:
            s_next = scores(blk + 1)
        finish(blk, s_cur)


def _attention(q, kv, sinks, g, qb):
    bsz, seq, attn_w = q.shape
    kv_w = kv.shape[-1]
    n_kv = kv_w // (2 * HEAD_DIM)
    assert n_kv * HEAD_DIM == LANES and attn_w == n_kv * KV_RATIO * HEAD_DIM
    tq = qb * WINDOW
    kern = functools.partial(_attention_kernel, n_kv=n_kv, qb=qb)
    return pl.pallas_call(
        kern,
        grid=(bsz, seq // tq),
        in_specs=[
            pl.BlockSpec(memory_space=pltpu.SMEM),
            pl.BlockSpec((1, tq, attn_w), lambda b, n: (b, n, 0)),
            pl.BlockSpec((1, tq, kv_w), lambda b, n: (b, n, 0)),
            pl.BlockSpec((1, WINDOW, kv_w), lambda b, n: (b, jnp.maximum(n * qb - 1, 0), 0)),
            pl.BlockSpec((1, attn_w), lambda b, n: (0, 0)),
        ],
        out_specs=pl.BlockSpec((1, tq, attn_w), lambda b, n: (b, n, 0)),
        out_shape=jax.ShapeDtypeStruct((bsz, seq, attn_w), _BF16),
        compiler_params=pltpu.CompilerParams(
            dimension_semantics=("arbitrary", "arbitrary"), vmem_limit_bytes=VMEM_LIMIT),
        name="swa",
    )(sinks * LOG2E, q, kv, kv, g.reshape(1, attn_w))


def _ssm_kernel(u_ref, wb_ref, cw_ref, ar_ref, ai_ref, d_ref, z_ref, xs_ref, st_ref, il_ref,
                *, bsz, tc, ns):
    @pl.when(pl.program_id(1) == 0)
    def _():
        st_ref[...] = jnp.zeros_like(st_ref)

    tiles = il_ref.shape[0]
    for b in range(bsz):
        ub = u_ref[b].astype(_F32)
        for c in range(tiles):
            il_ref[c, pl.ds(b, tc, stride=bsz), :] = ub[:, c * LANES:(c + 1) * LANES]
    ar = jnp.broadcast_to(ar_ref[0], (bsz, ns))
    ai = jnp.broadcast_to(ai_ref[0], (bsz, ns))
    xr = st_ref[:, 0:ns]
    xi = st_ref[:, ns:2 * ns]
    sr = S5_SUB * bsz
    n_sub = tc // S5_SUB
    u32 = []
    for k in range(n_sub):
        rows = slice(k * sr, (k + 1) * sr)
        uk = jnp.concatenate([il_ref[c, rows, :] for c in range(tiles)], axis=1)
        u32.append(uk)
        xs_ref[rows, :] = jnp.dot(uk.astype(_BF16), wb_ref[0], preferred_element_type=_F32)
    ys = []
    for k in range(n_sub):
        for t in range(k * S5_SUB, (k + 1) * S5_SUB):
            r = slice(t * bsz, (t + 1) * bsz)
            nxr = ar * xr - ai * xi + xs_ref[r, 0:ns]
            nxi = ar * xi + ai * xr + xs_ref[r, ns:2 * ns]
            xs_ref[r, 0:ns] = nxr
            xs_ref[r, ns:2 * ns] = nxi
            xr, xi = nxr, nxi
        rows = slice(k * sr, (k + 1) * sr)
        ys.append(jnp.dot(xs_ref[rows, 0:ns].astype(_BF16), cw_ref[0, 0:ns, :], preferred_element_type=_F32)
                  + jnp.dot(xs_ref[rows, ns:2 * ns].astype(_BF16), cw_ref[0, ns:2 * ns, :],
                            preferred_element_type=_F32)
                  + d_ref[...] * u32[k])
    st_ref[:, 0:ns] = xr
    st_ref[:, ns:2 * ns] = xi
    y = jnp.concatenate(ys, axis=0)
    z = _gelu_tanh(y)
    for c in range(tiles):
        il_ref[c] = z[:, c * LANES:(c + 1) * LANES]
    for b in range(bsz):
        zb = jnp.concatenate([il_ref[c, pl.ds(b, tc, stride=bsz), :] for c in range(tiles)], axis=1)
        z_ref[b] = zb.astype(_BF16)


def _ssm_params(lam_re, lam_im, log_step, b_re, b_im, c_re, c_im, slab):
    g, p = lam_re.shape
    h = SSM_GROUP
    gs = slab // h
    n_slabs = g // gs
    dt = jnp.exp(log_step)[:, None]
    mag = jnp.exp(lam_re * dt)
    ang = lam_im * dt
    ab_re = mag * jnp.cos(ang)
    ab_im = mag * jnp.sin(ang)
    den = lam_re * lam_re + lam_im * lam_im
    f_re = ((ab_re - 1.0) * lam_re + ab_im * lam_im) / den
    f_im = (ab_im * lam_re - (ab_re - 1.0) * lam_im) / den
    bb_re = f_re[..., None] * b_re - f_im[..., None] * b_im
    bb_im = f_re[..., None] * b_im + f_im[..., None] * b_re
    eye = jnp.eye(gs, dtype=_F32)

    def block_diag_in(bb):
        t = bb.reshape(n_slabs, gs, p, h)
        return jnp.einsum('sgph,gk->sghkp', t, eye).reshape(n_slabs, gs * h, gs * p)

    def block_diag_out(cc):
        t = cc.reshape(n_slabs, gs, h, p)
        return jnp.einsum('sghp,gk->sgpkh', t, eye).reshape(n_slabs, gs * p, gs * h)

    wb = jnp.concatenate([block_diag_in(bb_re), block_diag_in(bb_im)], axis=2).astype(_BF16)
    cw = jnp.concatenate([block_diag_out(c_re), block_diag_out(-c_im)], axis=1).astype(_BF16)
    a_re = ab_re.reshape(n_slabs, 1, gs * p)
    a_im = ab_im.reshape(n_slabs, 1, gs * p)
    return wb, cw, a_re, a_im


def _ssm(u, wb, cw, a_re, a_im, d_skip, tc):
    bsz, seq, width = u.shape
    n_slabs, slab, ns2 = wb.shape
    ns = ns2 // 2
    kern = functools.partial(_ssm_kernel, bsz=bsz, tc=tc, ns=ns)
    return pl.pallas_call(
        kern,
        grid=(n_slabs, seq // tc),
        in_specs=[
            pl.BlockSpec((bsz, tc, slab), lambda s, t: (0, t, s)),
            pl.BlockSpec((1, slab, ns2), lambda s, t: (s, 0, 0)),
            pl.BlockSpec((1, ns2, slab), lambda s, t: (s, 0, 0)),
            pl.BlockSpec((1, 1, ns), lambda s, t: (s, 0, 0)),
            pl.BlockSpec((1, 1, ns), lambda s, t: (s, 0, 0)),
            pl.BlockSpec((1, slab), lambda s, t: (0, s)),
        ],
        out_specs=pl.BlockSpec((bsz, tc, slab), lambda s, t: (0, t, s)),
        out_shape=jax.ShapeDtypeStruct((bsz, seq, width), _BF16),
        scratch_shapes=[
            pltpu.VMEM((tc * bsz, ns2), _F32),
            pltpu.VMEM((bsz, ns2), _F32),
            pltpu.VMEM((slab // LANES, tc * bsz, LANES), _F32),
        ],
        compiler_params=pltpu.CompilerParams(
            dimension_semantics=("arbitrary", "arbitrary"), vmem_limit_bytes=VMEM_LIMIT),
        name="s5",
    )(u, wb, cw, a_re, a_im, d_skip.reshape(1, width))


def _post_mix_kernel(x_ref, attn_ref, z_ref, ada_ref, wglu_ref, wout_ref, gssm_ref, gpost_ref,
                     o_ref, *, d, attn_w):
    z = z_ref[0]
    zf = z.astype(_F32)
    ssm = zf * _sigmoid(jnp.dot(z, wglu_ref[...], preferred_element_type=_F32))
    ssm_n = (_rms_scale(ssm) * gssm_ref[...]).astype(_BF16)
    o_ref[0] = (jnp.dot(attn_ref[0], wout_ref[0:attn_w, :], preferred_element_type=_F32)
                + jnp.dot(ssm_n, wout_ref[attn_w:, :], preferred_element_type=_F32))
    gg = (1.0 + ada_ref[0][:, 2 * d:3 * d]) * gpost_ref[...]
    for r in range(0, o_ref.shape[1], NORM_ROWS):
        rows = slice(r, r + NORM_ROWS)
        o_ref[0, rows, :] = x_ref[0, rows, :] + _rms_scale(o_ref[0, rows, :]) * gg


def _post_mix(x, attn_n, z, ada, wglu_bf16, wout_bf16, g_ssm, g_post, tm):
    bsz, seq, d = x.shape
    attn_w = attn_n.shape[-1]
    ssm_w = wglu_bf16.shape[0]
    kern = functools.partial(_post_mix_kernel, d=d, attn_w=attn_w)
    return pl.pallas_call(
        kern,
        grid=(bsz, seq // tm),
        in_specs=[
            pl.BlockSpec((1, tm, d), lambda b, i: (b, i, 0)),
            pl.BlockSpec((1, tm, attn_w), lambda b, i: (b, i, 0)),
            pl.BlockSpec((1, tm, ssm_w), lambda b, i: (b, i, 0)),
            pl.BlockSpec((1, 1, 6 * d), lambda b, i: (b, 0, 0)),
            _resident((ssm_w, ssm_w), lambda b, i: (0, 0)),
            _resident((attn_w + ssm_w, d), lambda b, i: (0, 0)),
            pl.BlockSpec((1, ssm_w), lambda b, i: (0, 0)),
            pl.BlockSpec((1, d), lambda b, i: (0, 0)),
        ],
        out_specs=pl.BlockSpec((1, tm, d), lambda b, i: (b, i, 0)),
        out_shape=jax.ShapeDtypeStruct((bsz, seq, d), _F32),
        compiler_params=pltpu.CompilerParams(
            dimension_semantics=("arbitrary", "arbitrary"), vmem_limit_bytes=VMEM_LIMIT),
        name="post_mix",
    )(x, attn_n, z, ada, wglu_bf16, wout_bf16,
      g_ssm.reshape(1, ssm_w), g_post.reshape(1, d))


HALO = BF16_ROWS
FFN_SUB = 2 * LANES
FFN_ROWS = 64


def _ffn_kernel(x_ref, xh_ref, ada_ref, gpre_ref, wv_ref, wg_ref, cp_ref,
                wd_ref, wdt_ref, gpost_ref, o_ref, h_ref, upv_ref, upg_ref, act_ref, carry_ref, *, d, tm):
    i = pl.program_id(1)
    j = pl.program_id(2)
    ada = ada_ref[0]
    sub = FFN_SUB

    @pl.when(j == 0)
    def _():
        g = gpre_ref[...]
        scale = ada[:, 4 * d:5 * d]
        shift = ada[:, 3 * d:4 * d]
        hh = _modulated_norm(xh_ref[0], g, scale, shift)
        h_ref[0:HALO, :] = jnp.where(i > 0, hh, 0.0).astype(_BF16)
        gs = g * (1.0 + scale)
        for r in range(0, tm, NORM_ROWS):
            xb = x_ref[0, r:r + NORM_ROWS, :]
            h_ref[HALO + r:HALO + r + NORM_ROWS, :] = (_rms_scale(xb) * gs + shift).astype(_BF16)
        o_ref[...] = jnp.zeros_like(o_ref)
        carry_ref[...] = jnp.zeros_like(carry_ref)

    hv = h_ref[...]
    tiles = sub // LANES
    for s in range(2):
        cs = slice(s * sub, (s + 1) * sub)
        rv = jnp.dot(hv, wv_ref[:, cs], preferred_element_type=_F32)
        rg = jnp.dot(hv, wg_ref[:, cs], preferred_element_type=_F32)
        for c in range(tiles):
            upv_ref[s * tiles + c] = rv[:, c * LANES:(c + 1) * LANES]
            upg_ref[s * tiles + c] = rg[:, c * LANES:(c + 1) * LANES]

    def conv(up_ref, col0, t, r):
        cs = slice(col0 + t * LANES, col0 + (t + 1) * LANES)
        out = cp_ref[0, CONV_WIDTH:CONV_WIDTH + 1, cs].astype(_BF16)
        for k in range(CONV_WIDTH):
            off = HALO - (CONV_WIDTH - 1) + k + r
            out = out + (up_ref[t, pl.ds(off, FFN_ROWS, stride=1), :].astype(_BF16)
                         * cp_ref[0, k:k + 1, cs].astype(_BF16))
        return out

    def activation(dst, s, col0):
        for c in range(tiles):
            t = s * tiles + c
            for r in range(0, tm, FFN_ROWS):
                a = _gelu_tanh_x2(conv(upg_ref, 2 * sub, t, r)) * conv(upv_ref, 0, t, r)
                dst[r:r + FFN_ROWS, col0 + c * LANES:col0 + (c + 1) * LANES] = a.astype(_BF16)

    act_ref[:, 0:sub] = carry_ref[...]
    activation(act_ref, 0, sub)
    o_ref[0] += jnp.dot(act_ref[...], wd_ref[...], preferred_element_type=_F32)
    activation(carry_ref, 1, 0)

    @pl.when(j == pl.num_programs(2) - 1)
    def _():
        o_ref[0] += jnp.dot(carry_ref[...], wdt_ref[...], preferred_element_type=_F32)
        gg = (1.0 + ada[:, 5 * d:6 * d]) * gpost_ref[...]
        for r in range(0, tm, NORM_ROWS):
            rows = slice(r, r + NORM_ROWS)
            o_ref[0, rows, :] = x_ref[0, rows, :] + _rms_scale(o_ref[0, rows, :]) * gg


def _ffn(x, ada, g_pre, wup_bf16, conv_w, conv_b, wdown_bf16, g_post, tm):
    bsz, seq, d = x.shape
    d_ff = wdown_bf16.shape[0]
    tf = 2 * FFN_SUB
    nj = d_ff // tf
    assert nj * tf == d_ff
    halo_blocks = tm // HALO
    kern = functools.partial(_ffn_kernel, d=d, tm=tm)
    half = jnp.concatenate([jnp.full((d_ff,), 0.5, _F32), jnp.ones((d_ff,), _F32)])
    cp = jnp.concatenate([conv_w * half, (conv_b * half)[None, :],
                          jnp.zeros((SUBLANES - CONV_WIDTH - 1, 2 * d_ff), _F32)], axis=0)
    cp = jnp.concatenate([cp[:, :d_ff].reshape(SUBLANES, nj, tf), cp[:, d_ff:].reshape(SUBLANES, nj, tf)],
                         axis=2).transpose(1, 0, 2)
    wd = jnp.concatenate([jnp.zeros((FFN_SUB, d), _BF16), wdown_bf16], axis=0)
    return pl.pallas_call(
        kern,
        grid=(bsz, seq // tm, nj),
        in_specs=[
            pl.BlockSpec((1, tm, d), lambda b, i, j: (b, i, 0)),
            pl.BlockSpec((1, HALO, d), lambda b, i, j: (b, jnp.maximum(i * halo_blocks - 1, 0), 0)),
            pl.BlockSpec((1, 1, 6 * d), lambda b, i, j: (b, 0, 0)),
            pl.BlockSpec((1, d), lambda b, i, j: (0, 0)),
            pl.BlockSpec((d, tf), lambda b, i, j: (0, j)),
            pl.BlockSpec((d, tf), lambda b, i, j: (0, nj + j)),
            pl.BlockSpec((1, SUBLANES, 2 * tf), lambda b, i, j: (j, 0, 0)),
            pl.BlockSpec((tf, d), lambda b, i, j: (j, 0)),
            _resident((FFN_SUB, d), lambda b, i, j: (d_ff // FFN_SUB, 0)),
            pl.BlockSpec((1, d), lambda b, i, j: (0, 0)),
        ],
        out_specs=pl.BlockSpec((1, tm, d), lambda b, i, j: (b, i, 0)),
        out_shape=jax.ShapeDtypeStruct((bsz, seq, d), _F32),
        scratch_shapes=[
            pltpu.VMEM((tm + HALO, d), _BF16),
            pltpu.VMEM((tf // LANES, tm + HALO, LANES), _F32),
            pltpu.VMEM((tf // LANES, tm + HALO, LANES), _F32),
            pltpu.VMEM((tm, tf), _BF16),
            pltpu.VMEM((tm, FFN_SUB), _BF16),
        ],
        compiler_params=pltpu.CompilerParams(
            dimension_semantics=("arbitrary", "arbitrary", "arbitrary"),
            vmem_limit_bytes=VMEM_LIMIT),
        name="conv_ffn",
    )(x, x, ada, g_pre.reshape(1, d), wup_bf16, wup_bf16, cp,
      wd, wd, g_post.reshape(1, d))


def _largest_tile(n, cap, quantum):
    best = quantum
    for t in range(quantum, min(n, cap) + 1, quantum):
        if n % t == 0:
            best = t
    return best


def kernel(x, c, w_ada, b_ada, g_pre_mix, g_post_mix, w_in, attn_sinks, lam_re, lam_im, log_step,
           ssm_b_re, ssm_b_im, ssm_c_re, ssm_c_im, ssm_d, w_glu, g_attn_out, g_ssm_out, w_out,
           g_pre_ffn, g_post_ffn, w_up, conv_w, conv_b, w_down):
    bsz, seq, d = x.shape
    depth = w_in.shape[0]
    ssm_w = w_glu.shape[1]
    attn_w = w_out.shape[1] - ssm_w
    kv_w = w_in.shape[2] - attn_w - ssm_w
    d_ff = w_down.shape[1]
    assert bsz == SUBLANES, "the scan keeps one batch row per f32 sublane"

    tm = _largest_tile(seq, 512, WINDOW)
    tc = _largest_tile(seq, 256, S5_SUB)
    slab = 2 * LANES
    qb = _largest_tile(seq, 4 * WINDOW, WINDOW) // WINDOW

    ada_all = _adaln(c, w_ada, b_ada)
    for l in range(depth):
        ada = ada_all[l][:, None, :]
        q, kv, u = _in_proj(x, ada, g_pre_mix[l], w_in[l].astype(_BF16), attn_w, kv_w, tm)
        attn_n = _attention(q, kv, attn_sinks[l], g_attn_out[l], qb)
        wb, cw, a_re, a_im = _ssm_params(lam_re[l], lam_im[l], log_step[l], ssm_b_re[l], ssm_b_im[l],
                                         ssm_c_re[l], ssm_c_im[l], slab)
        z = _ssm(u, wb, cw, a_re, a_im, ssm_d[l].reshape(ssm_w), tc)
        x = _post_mix(x, attn_n, z, ada, w_glu[l].astype(_BF16), w_out[l].astype(_BF16),
                      g_ssm_out[l], g_post_mix[l], tm)
        x = _ffn(x, ada, g_pre_ffn[l], w_up[l].astype(_BF16), conv_w[l], conv_b[l],
                 w_down[l].astype(_BF16), g_post_ffn[l], tm)
    return x
```

```python
import functools
import math

import jax
import jax.numpy as jnp
from jax import lax
from jax.experimental import pallas as pl
from jax.experimental.pallas import tpu as pltpu

HEAD_DIM = 64
KV_RATIO = 8
WINDOW = 128
SSM_GROUP = 16
S5_SUB = 128
STATE = 64
CONV_WIDTH = 3
EPS = 1e-6
NEG = -1e30
LOG2E = math.log2(math.e)

LANES = 128
SUBLANES = 8
BF16_ROWS = 16
NORM_ROWS = BF16_ROWS
VMEM_LIMIT = 56 * 1024 * 1024

_BF16 = jnp.bfloat16
_F32 = jnp.float32


def _gelu_tanh_x2(x):
    c = math.sqrt(2.0 / math.pi)
    return x + x * jnp.tanh(x * (c + (0.044715 * c) * (x * x)))


def _gelu_tanh(x):
    return 0.5 * _gelu_tanh_x2(x)


def _sigmoid(x):
    return 1.0 / (1.0 + jnp.exp(-x))


def _rms_scale(x):
    return x * lax.rsqrt(jnp.mean(x * x, axis=-1, keepdims=True) + EPS)


def _resident(shape, index_map):
    return pl.BlockSpec(shape, index_map, pipeline_mode=pl.Buffered(1))


def _adaln_kernel(c_ref, w_ref, b_ref, o_ref):
    c = c_ref[...]
    ca = c * _sigmoid(c)
    o_ref[0] = jnp.dot(ca.astype(_BF16), w_ref[0].astype(_BF16), preferred_element_type=_F32) + b_ref[0]


def _adaln(c, w_ada, b_ada, tn=1024):
    depth, d, n = w_ada.shape
    bsz = c.shape[0]
    return pl.pallas_call(
        _adaln_kernel,
        grid=(depth, n // tn),
        in_specs=[
            pl.BlockSpec((bsz, d), lambda l, j: (0, 0)),
            pl.BlockSpec((1, d, tn), lambda l, j: (l, 0, j)),
            pl.BlockSpec((1, 1, tn), lambda l, j: (l, 0, j)),
        ],
        out_specs=pl.BlockSpec((1, bsz, tn), lambda l, j: (l, 0, j)),
        out_shape=jax.ShapeDtypeStruct((depth, bsz, n), _F32),
        compiler_params=pltpu.CompilerParams(
            dimension_semantics=("arbitrary", "arbitrary"), vmem_limit_bytes=VMEM_LIMIT),
        name="adaln",
    )(c, w_ada, b_ada.reshape(depth, 1, n))


def _modulated_norm(xv, g, scale, shift):
    return _rms_scale(xv) * (g * (1.0 + scale)) + shift


def _in_proj_kernel(x_ref, ada_ref, g_ref, w_ref, o_ref, *, d, attn_w):
    ada = ada_ref[0]
    h = _modulated_norm(x_ref[0], g_ref[...], ada[:, d:2 * d], ada[:, 0:d])
    p = jnp.dot(h.astype(_BF16), w_ref[...], preferred_element_type=_F32)
    o_ref[0, :, 0:attn_w] = (p[:, :attn_w] * (HEAD_DIM ** -0.5 * LOG2E)).astype(_BF16)
    o_ref[0, :, attn_w:] = p[:, attn_w:].astype(_BF16)


def _in_proj(x, ada, g, w_bf16, attn_w, tm):
    bsz, seq, d = x.shape
    ncols = w_bf16.shape[1]
    kern = functools.partial(_in_proj_kernel, d=d, attn_w=attn_w)
    return pl.pallas_call(
        kern,
        grid=(bsz, seq // tm),
        in_specs=[
            pl.BlockSpec((1, tm, d), lambda b, i: (b, i, 0)),
            pl.BlockSpec((1, 1, 6 * d), lambda b, i: (b, 0, 0)),
            pl.BlockSpec((1, d), lambda b, i: (0, 0)),
            _resident((d, ncols), lambda b, i: (0, 0)),
        ],
        out_specs=pl.BlockSpec((1, tm, ncols), lambda b, i: (b, i, 0)),
        out_shape=jax.ShapeDtypeStruct((bsz, seq, ncols), _BF16),
        compiler_params=pltpu.CompilerParams(
            dimension_semantics=("arbitrary", "arbitrary"), vmem_limit_bytes=VMEM_LIMIT),
        name="in_proj",
    )(x, ada, g.reshape(1, d), w_bf16)


def _attention_kernel(sink_ref, q_ref, kvc_ref, kvp_ref, g_ref, o_ref, *, n_kv, qb):
    step = pl.program_id(1)
    w = WINDOW
    rows = (qb + 1) * w
    kv = jnp.concatenate([kvp_ref[0], kvc_ref[0]], axis=0).astype(_F32)
    kw = n_kv * HEAD_DIM
    k2 = kv[:, :kw]
    v2 = kv[:, kw:]
    k2r = pltpu.roll(k2, HEAD_DIM, 1)
    v2r = pltpu.roll(v2, HEAD_DIM, 1)
    lo = lax.broadcasted_iota(jnp.int32, (rows, LANES), 1) < HEAD_DIM
    one_lo = jnp.where(lo, 1.0, 0.0).astype(_BF16)
    one_hi = jnp.where(lo, 0.0, 1.0).astype(_BF16)
    kk = (jnp.where(lo, k2, k2r).astype(_BF16), jnp.where(lo, k2r, k2).astype(_BF16))
    top = (jnp.concatenate([jnp.where(lo, v2, 0.0).astype(_BF16), one_lo], axis=1),
           jnp.concatenate([jnp.where(lo, v2r, 0.0).astype(_BF16), one_lo], axis=1))
    bot = (jnp.concatenate([jnp.where(lo, 0.0, v2r).astype(_BF16), one_hi], axis=1),
           jnp.concatenate([jnp.where(lo, 0.0, v2).astype(_BF16), one_hi], axis=1))

    qi = lax.broadcasted_iota(jnp.int32, (w, 2 * w), 0)
    kj = lax.broadcasted_iota(jnp.int32, (w, 2 * w), 1)
    in_band = (kj > qi) & (kj <= qi + w)
    qlane_lo = lax.broadcasted_iota(jnp.int32, (w, LANES), 1) < HEAD_DIM
    pairs_per_kv = KV_RATIO // 2
    n_pairs = n_kv * pairs_per_kv

    def scores(blk):
        out = []
        for pair in range(n_pairs):
            j = pair // pairs_per_kv
            q2 = q_ref[0, blk * w:(blk + 1) * w, pair * LANES:(pair + 1) * LANES]
            zero = jnp.zeros_like(q2)
            kkj = kk[j][blk * w:(blk + 2) * w]
            dn = (((1,), (1,)), ((), ()))
            out.append((lax.dot_general(jnp.where(qlane_lo, q2, zero), kkj, dn, preferred_element_type=_F32),
                        lax.dot_general(jnp.where(qlane_lo, zero, q2), kkj, dn, preferred_element_type=_F32)))
        return out

    def finish(blk, s_blk):
        valid = in_band if blk > 0 else in_band & ((kj >= w) | (step > 0))
        es, sk = [], []
        for pair in range(n_pairs):
            e2, k2_ = [], []
            for half in range(2):
                sink = sink_ref[2 * pair + half]
                s = jnp.where(valid, s_blk[pair][half], NEG)
                m = jnp.maximum(jnp.max(s, axis=-1, keepdims=True), sink)
                e2.append(jnp.exp2(s - m).astype(_BF16))
                k2_.append(jnp.exp2(sink - m))
            es.append(jnp.concatenate(e2, axis=1))
            sk.append(jnp.where(qlane_lo, k2_[0], k2_[1]))
        outs = []
        for pair in range(n_pairs):
            j = pair // pairs_per_kv
            r = jnp.concatenate([top[j][blk * w:(blk + 2) * w], bot[j][blk * w:(blk + 2) * w]], axis=0)
            ox = jnp.dot(es[pair], r, preferred_element_type=_F32)
            outs.append(ox[:, :LANES] / (ox[:, LANES:] + sk[pair]))
        o = jnp.concatenate(outs, axis=1)
        o_ref[0, blk * w:(blk + 1) * w, :] = (_rms_scale(o) * g_ref[...]).astype(_BF16)

    s_next = scores(0)
    for blk in range(qb):
        s_cur = s_next
        if blk + 1 < qb:
            s_next = scores(blk + 1)
        finish(blk, s_cur)


def _attention(qkvu, sinks, g, qb, attn_w, kv_w):
    bsz, seq, _ = qkvu.shape
    kvb = attn_w // kv_w
    assert kvb * kv_w == attn_w
    n_kv = kv_w // (2 * HEAD_DIM)
    assert n_kv * HEAD_DIM == LANES and attn_w == n_kv * KV_RATIO * HEAD_DIM
    tq = qb * WINDOW
    kern = functools.partial(_attention_kernel, n_kv=n_kv, qb=qb)
    return pl.pallas_call(
        kern,
        grid=(bsz, seq // tq),
        in_specs=[
            pl.BlockSpec(memory_space=pltpu.SMEM),
            pl.BlockSpec((1, tq, attn_w), lambda b, n: (b, n, 0)),
            pl.BlockSpec((1, tq, kv_w), lambda b, n: (b, n, kvb)),
            pl.BlockSpec((1, WINDOW, kv_w), lambda b, n: (b, jnp.maximum(n * qb - 1, 0), kvb)),
            pl.BlockSpec((1, attn_w), lambda b, n: (0, 0)),
        ],
        out_specs=pl.BlockSpec((1, tq, attn_w), lambda b, n: (b, n, 0)),
        out_shape=jax.ShapeDtypeStruct((bsz, seq, attn_w), _BF16),
        compiler_params=pltpu.CompilerParams(
            dimension_semantics=("arbitrary", "arbitrary"), vmem_limit_bytes=VMEM_LIMIT),
        name="swa",
    )(sinks * LOG2E, qkvu, qkvu, qkvu, g.reshape(1, attn_w))


def _ssm_kernel(u_ref, wb_ref, cw_ref, ar_ref, ai_ref, d_ref, z_ref, xs_ref, st_ref, il_ref,
                *, bsz, tc, ns):
    @pl.when(pl.program_id(1) == 0)
    def _():
        st_ref[...] = jnp.zeros_like(st_ref)

    tiles = il_ref.shape[0]
    for b in range(bsz):
        ub = u_ref[b].astype(_F32)
        for c in range(tiles):
            il_ref[c, pl.ds(b, tc, stride=bsz), :] = ub[:, c * LANES:(c + 1) * LANES]
    ar = jnp.broadcast_to(ar_ref[0], (bsz, ns))
    ai = jnp.broadcast_to(ai_ref[0], (bsz, ns))
    xr = st_ref[:, 0:ns]
    xi = st_ref[:, ns:2 * ns]
    sr = S5_SUB * bsz
    n_sub = tc // S5_SUB
    u32 = []
    for k in range(n_sub):
        rows = slice(k * sr, (k + 1) * sr)
        uk = jnp.concatenate([il_ref[c, rows, :] for c in range(tiles)], axis=1)
        u32.append(uk)
        xs_ref[rows, :] = jnp.dot(uk.astype(_BF16), wb_ref[0], preferred_element_type=_F32)
    ys = []
    for k in range(n_sub):
        for t in range(k * S5_SUB, (k + 1) * S5_SUB):
            r = slice(t * bsz, (t + 1) * bsz)
            nxr = ar * xr - ai * xi + xs_ref[r, 0:ns]
            nxi = ar * xi + ai * xr + xs_ref[r, ns:2 * ns]
            xs_ref[r, 0:ns] = nxr
            xs_ref[r, ns:2 * ns] = nxi
            xr, xi = nxr, nxi
        rows = slice(k * sr, (k + 1) * sr)
        ys.append(jnp.dot(xs_ref[rows, 0:ns].astype(_BF16), cw_ref[0, 0:ns, :], preferred_element_type=_F32)
                  + jnp.dot(xs_ref[rows, ns:2 * ns].astype(_BF16), cw_ref[0, ns:2 * ns, :],
                            preferred_element_type=_F32)
                  + d_ref[...] * u32[k])
    st_ref[:, 0:ns] = xr
    st_ref[:, ns:2 * ns] = xi
    y = jnp.concatenate(ys, axis=0)
    z = _gelu_tanh(y)
    for c in range(tiles):
        il_ref[c] = z[:, c * LANES:(c + 1) * LANES]
    for b in range(bsz):
        zb = jnp.concatenate([il_ref[c, pl.ds(b, tc, stride=bsz), :] for c in range(tiles)], axis=1)
        z_ref[b] = zb.astype(_BF16)


def _ssm_params(lam_re, lam_im, log_step, b_re, b_im, c_re, c_im, slab):
    g, p = lam_re.shape
    h = SSM_GROUP
    gs = slab // h
    n_slabs = g // gs
    dt = jnp.exp(log_step)[:, None]
    mag = jnp.exp(lam_re * dt)
    ang = lam_im * dt
    ab_re = mag * jnp.cos(ang)
    ab_im = mag * jnp.sin(ang)
    den = lam_re * lam_re + lam_im * lam_im
    f_re = ((ab_re - 1.0) * lam_re + ab_im * lam_im) / den
    f_im = (ab_im * lam_re - (ab_re - 1.0) * lam_im) / den
    bb_re = f_re[..., None] * b_re - f_im[..., None] * b_im
    bb_im = f_re[..., None] * b_im + f_im[..., None] * b_re
    eye = jnp.eye(gs, dtype=_F32)

    def block_diag_in(bb):
        t = bb.reshape(n_slabs, gs, p, h)
        return jnp.einsum('sgph,gk->sghkp', t, eye).reshape(n_slabs, gs * h, gs * p)

    def block_diag_out(cc):
        t = cc.reshape(n_slabs, gs, h, p)
        return jnp.einsum('sghp,gk->sgpkh', t, eye).reshape(n_slabs, gs * p, gs * h)

    wb = jnp.concatenate([block_diag_in(bb_re), block_diag_in(bb_im)], axis=2).astype(_BF16)
    cw = jnp.concatenate([block_diag_out(c_re), block_diag_out(-c_im)], axis=1).astype(_BF16)
    a_re = ab_re.reshape(n_slabs, 1, gs * p)
    a_im = ab_im.reshape(n_slabs, 1, gs * p)
    return wb, cw, a_re, a_im


def _ssm(qkvu, wb, cw, a_re, a_im, d_skip, tc):
    bsz, seq, ncols = qkvu.shape
    n_slabs, slab, ns2 = wb.shape
    width = n_slabs * slab
    ub = (ncols - width) // slab
    assert ub * slab == ncols - width
    ns = ns2 // 2
    kern = functools.partial(_ssm_kernel, bsz=bsz, tc=tc, ns=ns)
    return pl.pallas_call(
        kern,
        grid=(n_slabs, seq // tc),
        in_specs=[
            pl.BlockSpec((bsz, tc, slab), lambda s, t: (0, t, ub + s)),
            pl.BlockSpec((1, slab, ns2), lambda s, t: (s, 0, 0)),
            pl.BlockSpec((1, ns2, slab), lambda s, t: (s, 0, 0)),
            pl.BlockSpec((1, 1, ns), lambda s, t: (s, 0, 0)),
            pl.BlockSpec((1, 1, ns), lambda s, t: (s, 0, 0)),
            pl.BlockSpec((1, slab), lambda s, t: (0, s)),
        ],
        out_specs=pl.BlockSpec((bsz, tc, slab), lambda s, t: (0, t, s)),
        out_shape=jax.ShapeDtypeStruct((bsz, seq, width), _BF16),
        scratch_shapes=[
            pltpu.VMEM((tc * bsz, ns2), _F32),
            pltpu.VMEM((bsz, ns2), _F32),
            pltpu.VMEM((slab // LANES, tc * bsz, LANES), _F32),
        ],
        compiler_params=pltpu.CompilerParams(
            dimension_semantics=("arbitrary", "arbitrary"), vmem_limit_bytes=VMEM_LIMIT),
        name="s5",
    )(qkvu, wb, cw, a_re, a_im, d_skip.reshape(1, width))


def _post_mix_kernel(x_ref, attn_ref, z_ref, ada_ref, wglu_ref, wout_ref, gssm_ref, gpost_ref,
                     o_ref, *, d, attn_w):
    z = z_ref[0]
    zf = z.astype(_F32)
    ssm = zf * _sigmoid(jnp.dot(z, wglu_ref[...], preferred_element_type=_F32))
    ssm_n = (_rms_scale(ssm) * gssm_ref[...]).astype(_BF16)
    o_ref[0] = (jnp.dot(attn_ref[0], wout_ref[0:attn_w, :], preferred_element_type=_F32)
                + jnp.dot(ssm_n, wout_ref[attn_w:, :], preferred_element_type=_F32))
    gg = (1.0 + ada_ref[0][:, 2 * d:3 * d]) * gpost_ref[...]
    for r in range(0, o_ref.shape[1], NORM_ROWS):
        rows = slice(r, r + NORM_ROWS)
        o_ref[0, rows, :] = x_ref[0, rows, :] + _rms_scale(o_ref[0, rows, :]) * gg


def _post_mix(x, attn_n, z, ada, wglu_bf16, wout_bf16, g_ssm, g_post, tm):
    bsz, seq, d = x.shape
    attn_w = attn_n.shape[-1]
    ssm_w = wglu_bf16.shape[0]
    kern = functools.partial(_post_mix_kernel, d=d, attn_w=attn_w)
    return pl.pallas_call(
        kern,
        grid=(bsz, seq // tm),
        in_specs=[
            pl.BlockSpec((1, tm, d), lambda b, i: (b, i, 0)),
            pl.BlockSpec((1, tm, attn_w), lambda b, i: (b, i, 0)),
            pl.BlockSpec((1, tm, ssm_w), lambda b, i: (b, i, 0)),
            pl.BlockSpec((1, 1, 6 * d), lambda b, i: (b, 0, 0)),
            _resident((ssm_w, ssm_w), lambda b, i: (0, 0)),
            _resident((attn_w + ssm_w, d), lambda b, i: (0, 0)),
            pl.BlockSpec((1, ssm_w), lambda b, i: (0, 0)),
            pl.BlockSpec((1, d), lambda b, i: (0, 0)),
        ],
        out_specs=pl.BlockSpec((1, tm, d), lambda b, i: (b, i, 0)),
        out_shape=jax.ShapeDtypeStruct((bsz, seq, d), _F32),
        compiler_params=pltpu.CompilerParams(
            dimension_semantics=("arbitrary", "arbitrary"), vmem_limit_bytes=VMEM_LIMIT),
        name="post_mix",
    )(x, attn_n, z, ada, wglu_bf16, wout_bf16,
      g_ssm.reshape(1, ssm_w), g_post.reshape(1, d))


HALO = BF16_ROWS
FFN_SUB = 2 * LANES
FFN_ROWS = 64


def _ffn_kernel(x_ref, xh_ref, ada_ref, gpre_ref, wv_ref, wg_ref, cp_ref,
                wd_ref, wdt_ref, gpost_ref, o_ref, h_ref, upv_ref, upg_ref, act_ref, carry_ref, *, d, tm):
    i = pl.program_id(1)
    j = pl.program_id(2)
    ada = ada_ref[0]
    sub = FFN_SUB

    @pl.when(j == 0)
    def _():
        g = gpre_ref[...]
        scale = ada[:, 4 * d:5 * d]
        shift = ada[:, 3 * d:4 * d]
        hh = _modulated_norm(xh_ref[0], g, scale, shift)
        h_ref[0:HALO, :] = jnp.where(i > 0, hh, 0.0).astype(_BF16)
        gs = g * (1.0 + scale)
        for r in range(0, tm, NORM_ROWS):
            xb = x_ref[0, r:r + NORM_ROWS, :]
            h_ref[HALO + r:HALO + r + NORM_ROWS, :] = (_rms_scale(xb) * gs + shift).astype(_BF16)
        o_ref[...] = jnp.zeros_like(o_ref)
        carry_ref[...] = jnp.zeros_like(carry_ref)

    hv = h_ref[...]
    tiles = sub // LANES
    for s in range(2):
        cs = slice(s * sub, (s + 1) * sub)
        rv = jnp.dot(hv, wv_ref[:, cs], preferred_element_type=_F32)
        rg = jnp.dot(hv, wg_ref[:, cs], preferred_element_type=_F32)
        for c in range(tiles):
            upv_ref[s * tiles + c] = rv[:, c * LANES:(c + 1) * LANES]
            upg_ref[s * tiles + c] = rg[:, c * LANES:(c + 1) * LANES]

    def conv(up_ref, col0, t, r):
        cs = slice(col0 + t * LANES, col0 + (t + 1) * LANES)
        out = cp_ref[0, CONV_WIDTH:CONV_WIDTH + 1, cs].astype(_BF16)
        for k in range(CONV_WIDTH):
            off = HALO - (CONV_WIDTH - 1) + k + r
            out = out + (up_ref[t, pl.ds(off, FFN_ROWS, stride=1), :].astype(_BF16)
                         * cp_ref[0, k:k + 1, cs].astype(_BF16))
        return out

    def activation(dst, s, col0):
        for c in range(tiles):
            t = s * tiles + c
            for r in range(0, tm, FFN_ROWS):
                a = _gelu_tanh_x2(conv(upg_ref, 2 * sub, t, r)) * conv(upv_ref, 0, t, r)
                dst[r:r + FFN_ROWS, col0 + c * LANES:col0 + (c + 1) * LANES] = a.astype(_BF16)

    act_ref[:, 0:sub] = carry_ref[...]
    activation(act_ref, 0, sub)
    o_ref[0] += jnp.dot(act_ref[...], wd_ref[...], preferred_element_type=_F32)
    activation(carry_ref, 1, 0)

    @pl.when(j == pl.num_programs(2) - 1)
    def _():
        o_ref[0] += jnp.dot(carry_ref[...], wdt_ref[...], preferred_element_type=_F32)
        gg = (1.0 + ada[:, 5 * d:6 * d]) * gpost_ref[...]
        for r in range(0, tm, NORM_ROWS):
            rows = slice(r, r + NORM_ROWS)
            o_ref[0, rows, :] = x_ref[0, rows, :] + _rms_scale(o_ref[0, rows, :]) * gg


def _ffn(x, ada, g_pre, wup_bf16, conv_w, conv_b, wdown_bf16, g_post, tm):
    bsz, seq, d = x.shape
    d_ff = wdown_bf16.shape[0]
    tf = 2 * FFN_SUB
    nj = d_ff // tf
    assert nj * tf == d_ff
    halo_blocks = tm // HALO
    kern = functools.partial(_ffn_kernel, d=d, tm=tm)
    half = jnp.concatenate([jnp.full((d_ff,), 0.5, _F32), jnp.ones((d_ff,), _F32)])
    cp = jnp.concatenate([conv_w * half, (conv_b * half)[None, :],
                          jnp.zeros((SUBLANES - CONV_WIDTH - 1, 2 * d_ff), _F32)], axis=0)
    cp = jnp.concatenate([cp[:, :d_ff].reshape(SUBLANES, nj, tf), cp[:, d_ff:].reshape(SUBLANES, nj, tf)],
                         axis=2).transpose(1, 0, 2)
    wd = jnp.concatenate([jnp.zeros((FFN_SUB, d), _BF16), wdown_bf16], axis=0)
    return pl.pallas_call(
        kern,
        grid=(bsz, seq // tm, nj),
        in_specs=[
            pl.BlockSpec((1, tm, d), lambda b, i, j: (b, i, 0)),
            pl.BlockSpec((1, HALO, d), lambda b, i, j: (b, jnp.maximum(i * halo_blocks - 1, 0), 0)),
            pl.BlockSpec((1, 1, 6 * d), lambda b, i, j: (b, 0, 0)),
            pl.BlockSpec((1, d), lambda b, i, j: (0, 0)),
            pl.BlockSpec((d, tf), lambda b, i, j: (0, j)),
            pl.BlockSpec((d, tf), lambda b, i, j: (0, nj + j)),
            pl.BlockSpec((1, SUBLANES, 2 * tf), lambda b, i, j: (j, 0, 0)),
            pl.BlockSpec((tf, d), lambda b, i, j: (j, 0)),
            _resident((FFN_SUB, d), lambda b, i, j: (d_ff // FFN_SUB, 0)),
            pl.BlockSpec((1, d), lambda b, i, j: (0, 0)),
        ],
        out_specs=pl.BlockSpec((1, tm, d), lambda b, i, j: (b, i, 0)),
        out_shape=jax.ShapeDtypeStruct((bsz, seq, d), _F32),
        scratch_shapes=[
            pltpu.VMEM((tm + HALO, d), _BF16),
            pltpu.VMEM((tf // LANES, tm + HALO, LANES), _F32),
            pltpu.VMEM((tf // LANES, tm + HALO, LANES), _F32),
            pltpu.VMEM((tm, tf), _BF16),
            pltpu.VMEM((tm, FFN_SUB), _BF16),
        ],
        compiler_params=pltpu.CompilerParams(
            dimension_semantics=("arbitrary", "arbitrary", "arbitrary"),
            vmem_limit_bytes=VMEM_LIMIT),
        name="conv_ffn",
    )(x, x, ada, g_pre.reshape(1, d), wup_bf16, wup_bf16, cp,
      wd, wd, g_post.reshape(1, d))


def _largest_tile(n, cap, quantum):
    best = quantum
    for t in range(quantum, min(n, cap) + 1, quantum):
        if n % t == 0:
            best = t
    return best


def kernel(x, c, w_ada, b_ada, g_pre_mix, g_post_mix, w_in, attn_sinks, lam_re, lam_im, log_step,
           ssm_b_re, ssm_b_im, ssm_c_re, ssm_c_im, ssm_d, w_glu, g_attn_out, g_ssm_out, w_out,
           g_pre_ffn, g_post_ffn, w_up, conv_w, conv_b, w_down):
    bsz, seq, d = x.shape
    depth = w_in.shape[0]
    ssm_w = w_glu.shape[1]
    attn_w = w_out.shape[1] - ssm_w
    kv_w = w_in.shape[2] - attn_w - ssm_w
    d_ff = w_down.shape[1]
    assert bsz == SUBLANES, "the scan keeps one batch row per f32 sublane"

    tm = _largest_tile(seq, 512, WINDOW)
    tc = _largest_tile(seq, 256, S5_SUB)
    slab = 2 * LANES
    qb = _largest_tile(seq, 4 * WINDOW, WINDOW) // WINDOW

    ada_all = _adaln(c, w_ada, b_ada)
    for l in range(depth):
        ada = ada_all[l][:, None, :]
        qkvu = _in_proj(x, ada, g_pre_mix[l], w_in[l].astype(_BF16), attn_w, tm)
        attn_n = _attention(qkvu, attn_sinks[l], g_attn_out[l], qb, attn_w, kv_w)
        wb, cw, a_re, a_im = _ssm_params(lam_re[l], lam_im[l], log_step[l], ssm_b_re[l], ssm_b_im[l],
                                         ssm_c_re[l], ssm_c_im[l], slab)
        z = _ssm(qkvu, wb, cw, a_re, a_im, ssm_d[l].reshape(ssm_w), tc)
        x = _post_mix(x, attn_n, z, ada, w_glu[l].astype(_BF16), w_out[l].astype(_BF16),
                      g_ssm_out[l], g_post_mix[l], tm)
        x = _ffn(x, ada, g_pre_ffn[l], w_up[l].astype(_BF16), conv_w[l], conv_b[l],
                 w_down[l].astype(_BF16), g_post_ffn[l], tm)
    return x
```
